```python
import math
import jax, jax.numpy as jnp
from jax import lax
import numpy as np

D_MODEL = 2048
BATCH = 4
SEQ = 4096
DEPTH = 1

N_HEADS_A = 16
HEAD_DIM_A = 128
Q_RANK = 512
KV_RANK = 256
N_HEADS_IDX = 16
HEAD_DIM_IDX = 64
TOP_K_MAX = 256
Q_BLOCK = 128
N_BUCKETS = 32
MAX_DISTANCE = 128
SSM_EXPAND = 2
D_INNER = SSM_EXPAND * D_MODEL
SSM_HEAD_DIM = 64
N_HEADS_B = D_INNER // SSM_HEAD_DIM
SSM_GROUPS = 8
D_STATE = 128
CONV_WIDTH = 4
CHUNK = 128
D_XBC = D_INNER + 2 * SSM_GROUPS * D_STATE
D_FF = ((8 * D_MODEL // 3 + 255) // 256) * 256
IN_SPLITS = (Q_RANK, KV_RANK, HEAD_DIM_IDX, N_HEADS_IDX, D_INNER, D_XBC, N_HEADS_B, D_MODEL, D_MODEL)
D_IN_PROJ = sum(IN_SPLITS)
EPS = 1e-6

kernel_name = "hybrid_dsa_ssd_gated_block"


def rms_norm(x, g):
    xf = x.astype(jnp.float32)
    y = xf * lax.rsqrt(jnp.mean(xf * xf, axis=-1, keepdims=True) + EPS)
    return (y * g.astype(jnp.float32)).astype(x.dtype)


def layer_norm(x, g, b):
    xf = x.astype(jnp.float32)
    mu = jnp.mean(xf, axis=-1, keepdims=True)
    xc = xf - mu
    var = jnp.mean(xc * xc, axis=-1, keepdims=True)
    y = xc * lax.rsqrt(var + EPS) * g.astype(jnp.float32) + b.astype(jnp.float32)
    return y.astype(x.dtype)


def t5_bucket(dist):
    n = jnp.maximum(dist, 0)
    max_exact = N_BUCKETS // 2
    nf = jnp.maximum(n, 1).astype(jnp.float32)
    large = max_exact + (jnp.log(nf / max_exact) / math.log(MAX_DISTANCE / max_exact)
                         * (N_BUCKETS - max_exact)).astype(jnp.int32)
    large = jnp.minimum(large, N_BUCKETS - 1)
    return jnp.where(n < max_exact, n, large)


def dsa_branch(c_q, c_kv, k_idx, w_idx, w_uq, w_iq, w_uk, w_uv, rel_bias):
    B, L, _ = c_q.shape
    top_k = min(TOP_K_MAX, L // 4)
    n_blk = L // Q_BLOCK
    q = (c_q @ w_uq).reshape(B, L, N_HEADS_A, HEAD_DIM_A)
    q_idx = (c_q @ w_iq).reshape(B, L, N_HEADS_IDX, HEAD_DIM_IDX)
    q_lat = jnp.einsum('blhd,rhd->blhr', q, w_uk.reshape(KV_RANK, N_HEADS_A, HEAD_DIM_A)) * (HEAD_DIM_A ** -0.5)
    w_idx = w_idx * (N_HEADS_IDX ** -0.5 * HEAD_DIM_IDX ** -0.5)
    s_pos = jnp.arange(L, dtype=jnp.int32)

    def block(i):
        t0 = i * Q_BLOCK
        ql = lax.dynamic_slice_in_dim(q_lat, t0, Q_BLOCK, axis=1)
        qi = lax.dynamic_slice_in_dim(q_idx, t0, Q_BLOCK, axis=1)
        wi = lax.dynamic_slice_in_dim(w_idx, t0, Q_BLOCK, axis=1)
        t_pos = t0 + jnp.arange(Q_BLOCK, dtype=jnp.int32)
        idx_score = jnp.einsum('bqh,bqhs->bqs', wi,
                               jax.nn.relu(jnp.einsum('bqhd,bsd->bqhs', qi, k_idx)))
        causal = s_pos[None, :] <= t_pos[:, None]
        idx_score = jnp.where(causal[None], idx_score.astype(jnp.float32), -jnp.inf)
        _, sel = lax.top_k(idx_score, top_k)
        kv_sel = jax.vmap(lambda ckv, ix: ckv[ix])(c_kv, sel)
        logits = jnp.einsum('bqhr,bqkr->bqhk', ql, kv_sel).astype(jnp.float32)
        bias = rel_bias[t5_bucket(t_pos[None, :, None] - sel)]
        logits = logits + jnp.transpose(bias, (0, 1, 3, 2)).astype(jnp.float32)
        valid = (sel <= t_pos[None, :, None])[:, :, None, :]
        logits = jnp.where(valid, logits, -jnp.inf)
        p = jax.nn.softmax(logits, axis=-1).astype(kv_sel.dtype)
        return jnp.einsum('bqhk,bqkr->bqhr', p, kv_sel)

    o_lat = lax.map(block, jnp.arange(n_blk, dtype=jnp.int32))
    o_lat = jnp.transpose(o_lat, (1, 0, 2, 3, 4)).reshape(B, L, N_HEADS_A, KV_RANK)
    o = jnp.einsum('blhr,rhd->blhd', o_lat, w_uv.reshape(KV_RANK, N_HEADS_A, HEAD_DIM_A))
    return o.reshape(B, L, N_HEADS_A * HEAD_DIM_A)


def ssd_chunked(x, dt, a, b_in, c_in):
    Bsz, L, H, P = x.shape
    nc = L // CHUNK
    HG = H // SSM_GROUPS

    def to_chunks(t):
        return jnp.moveaxis(t.reshape(Bsz, nc, CHUNK, *t.shape[2:]), 1, 0)

    xc = to_chunks(x.astype(jnp.float32).reshape(Bsz, L, SSM_GROUPS, HG, P))
    dtc = to_chunks(dt.reshape(Bsz, L, SSM_GROUPS, HG))
    bc = to_chunks(b_in.astype(jnp.float32))
    cc = to_chunks(c_in.astype(jnp.float32))
    a_g = a.reshape(SSM_GROUPS, HG)
    tri = jnp.tril(jnp.ones((CHUNK, CHUNK), dtype=bool))[None, :, :, None, None]

    def step(state, inp):
        xq, dq, bq, cq = inp
        a_cs = jnp.cumsum(dq * a_g, axis=1)
        seg = a_cs[:, :, None] - a_cs[:, None, :]
        decay = jnp.exp(jnp.where(tri, seg, -jnp.inf))
        cb = jnp.einsum('btgn,bsgn->btsg', cq, bq)
        w = cb[..., None] * decay * dq[:, None]
        y_intra = jnp.einsum('btsgh,bsghp->btghp', w, xq)
        y_inter = jnp.einsum('btgn,bghpn->btghp', cq, state) * jnp.exp(a_cs)[..., None]
        decay_end = jnp.exp(a_cs[:, -1:] - a_cs) * dq
        new_state = (state * jnp.exp(a_cs[:, -1])[..., None, None]
                     + jnp.einsum('bsgh,bsgn,bsghp->bghpn', decay_end, bq, xq))
        return new_state, y_intra + y_inter

    state0 = jnp.zeros((Bsz, SSM_GROUPS, HG, P, D_STATE), jnp.float32)
    _, yc = lax.scan(step, state0, (xc, dtc, bc, cc))
    return jnp.moveaxis(yc, 0, 1).reshape(Bsz, L, H, P)


def ssd_branch(z, xbc, dt_raw, conv_w, conv_b, dt_bias, a_log, d_skip, ssm_norm_g):
    B, L, _ = xbc.shape
    conv = lax.conv_general_dilated(xbc, conv_w[:, None, :], window_strides=(1,),
                                    padding=[(CONV_WIDTH - 1, 0)],
                                    dimension_numbers=('NWC', 'WIO', 'NWC'),
                                    feature_group_count=D_XBC)
    conv = jax.nn.silu(conv + conv_b)
    xs, b_in, c_in = jnp.split(conv, [D_INNER, D_INNER + SSM_GROUPS * D_STATE], axis=-1)
    x_h = xs.reshape(B, L, N_HEADS_B, SSM_HEAD_DIM)
    dt = jax.nn.softplus(dt_raw.astype(jnp.float32) + dt_bias.astype(jnp.float32))
    a = -jnp.exp(a_log.astype(jnp.float32))
    y = ssd_chunked(x_h, dt, a,
                    b_in.reshape(B, L, SSM_GROUPS, D_STATE),
                    c_in.reshape(B, L, SSM_GROUPS, D_STATE))
    y = (y + d_skip.astype(jnp.float32)[:, None] * x_h.astype(jnp.float32)).astype(xbc.dtype)
    y = y.reshape(B, L, D_INNER) * jax.nn.silu(z)
    y = rms_norm(y.reshape(B, L, SSM_GROUPS, D_INNER // SSM_GROUPS),
                 ssm_norm_g.reshape(SSM_GROUPS, D_INNER // SSM_GROUPS))
    return y.reshape(B, L, D_INNER)


def hybrid_mixer(h, w_in, cq_norm_g, ckv_norm_g, kidx_norm_g, kidx_norm_b, w_uq, w_iq, w_uk, w_uv,
                 rel_bias, conv_w, conv_b, dt_bias, a_log, d_skip, ssm_norm_g, w_proj_a, w_proj_b, w_out):
    proj = h @ w_in
    split_points = [int(v) for v in np.cumsum(IN_SPLITS)[:-1]]
    c_q, c_kv, k_idx, w_idx, z, xbc, dt_raw, g_a, g_b = jnp.split(proj, split_points, axis=-1)
    c_q = rms_norm(c_q, cq_norm_g)
    c_kv = rms_norm(c_kv, ckv_norm_g)
    k_idx = layer_norm(k_idx, kidx_norm_g, kidx_norm_b)
    y_a = dsa_branch(c_q, c_kv, k_idx, w_idx, w_uq, w_iq, w_uk, w_uv, rel_bias)
    y_b = ssd_branch(z, xbc, dt_raw, conv_w, conv_b, dt_bias, a_log, d_skip, ssm_norm_g)
    merged = jax.nn.sigmoid(g_a) * (y_a @ w_proj_a) + jax.nn.sigmoid(g_b) * (y_b @ w_proj_b)
    return merged @ w_out


def swiglu(h, w_gate, w_up, w_down):
    return (jax.nn.silu(h @ w_gate) * (h @ w_up)) @ w_down


def setup_inputs(seed: int = 0) -> dict:
    key = jax.random.key(seed)
    ks = jax.random.split(key, 32)
    f32 = jnp.float32

    def nrm(k, shape, scale):
        return jax.random.normal(k, shape, f32) * scale

    def gain(k, shape):
        return 1.0 + 0.05 * jax.random.normal(k, shape, f32)

    NL = DEPTH
    HA = N_HEADS_A * HEAD_DIM_A
    dt0 = jnp.exp(jax.random.uniform(ks[17], (NL, N_HEADS_B), f32, math.log(1e-3), math.log(1e-1)))
    return {
        "x": nrm(ks[0], (BATCH, SEQ, D_MODEL), 1.0),
        "c": nrm(ks[1], (BATCH, D_MODEL), 1.0),
        "w_ada": nrm(ks[2], (NL, D_MODEL, 6 * D_MODEL), 0.2 * D_MODEL ** -0.5),
        "b_ada": nrm(ks[3], (NL, 6 * D_MODEL), 0.02),
        "norm1_g": gain(ks[4], (NL, D_MODEL)),
        "w_in": nrm(ks[5], (NL, D_MODEL, D_IN_PROJ), D_MODEL ** -0.5),
        "cq_norm_g": gain(ks[6], (NL, Q_RANK)),
        "ckv_norm_g": gain(ks[7], (NL, KV_RANK)),
        "kidx_norm_g": gain(ks[8], (NL, HEAD_DIM_IDX)),
        "kidx_norm_b": nrm(ks[9], (NL, HEAD_DIM_IDX), 0.02),
        "w_uq": nrm(ks[10], (NL, Q_RANK, HA), Q_RANK ** -0.5),
        "w_iq": nrm(ks[11], (NL, Q_RANK, N_HEADS_IDX * HEAD_DIM_IDX), Q_RANK ** -0.5),
        "w_uk": nrm(ks[12], (NL, KV_RANK, HA), KV_RANK ** -0.5),
        "w_uv": nrm(ks[13], (NL, KV_RANK, HA), KV_RANK ** -0.5),
        "rel_bias": nrm(ks[14], (N_BUCKETS, N_HEADS_A), 0.5),
        "conv_w": nrm(ks[15], (NL, CONV_WIDTH, D_XBC), CONV_WIDTH ** -0.5),
        "conv_b": nrm(ks[16], (NL, D_XBC), 0.02),
        "dt_bias": dt0 + jnp.log(-jnp.expm1(-dt0)),
        "a_log": jnp.log(jax.random.uniform(ks[18], (NL, N_HEADS_B), f32, 1.0, 16.0)),
        "d_skip": gain(ks[19], (NL, N_HEADS_B)),
        "ssm_norm_g": gain(ks[20], (NL, D_INNER)),
        "w_proj_a": nrm(ks[21], (NL, HA, D_MODEL), HA ** -0.5),
        "w_proj_b": nrm(ks[22], (NL, D_INNER, D_MODEL), D_INNER ** -0.5),
        "w_out": nrm(ks[23], (NL, D_MODEL, D_MODEL), D_MODEL ** -0.5),
        "norm2_g": gain(ks[24], (NL, D_MODEL)),
        "w_gate": nrm(ks[25], (NL, D_MODEL, D_FF), D_MODEL ** -0.5),
        "w_up": nrm(ks[26], (NL, D_MODEL, D_FF), D_MODEL ** -0.5),
        "w_down": nrm(ks[27], (NL, D_FF, D_MODEL), D_FF ** -0.5),
        "final_g": gain(ks[28], (D_MODEL,)),
    }


def reference(x, c, w_ada, b_ada, norm1_g, w_in, cq_norm_g, ckv_norm_g, kidx_norm_g, kidx_norm_b,
              w_uq, w_iq, w_uk, w_uv, rel_bias, conv_w, conv_b, dt_bias, a_log, d_skip, ssm_norm_g,
              w_proj_a, w_proj_b, w_out, norm2_g, w_gate, w_up, w_down, final_g):
    c_act = jax.nn.silu(c)
    for l in range(DEPTH):
        mod = (c_act @ w_ada[l] + b_ada[l])[:, None, :]
        sh1, sc1, g1, sh2, sc2, g2 = jnp.split(mod, 6, axis=-1)
        h = rms_norm(x, norm1_g[l]) * (1.0 + sc1) + sh1
        x = x + g1 * hybrid_mixer(h, w_in[l], cq_norm_g[l], ckv_norm_g[l], kidx_norm_g[l], kidx_norm_b[l],
                                  w_uq[l], w_iq[l], w_uk[l], w_uv[l], rel_bias, conv_w[l], conv_b[l],
                                  dt_bias[l], a_log[l], d_skip[l], ssm_norm_g[l],
                                  w_proj_a[l], w_proj_b[l], w_out[l])
        h = rms_norm(x, norm2_g[l]) * (1.0 + sc2) + sh2
        x = x + g2 * swiglu(h, w_gate[l], w_up[l], w_down[l])
    return rms_norm(x, final_g)
```

```python
import functools
import math

import numpy as np
import jax
import jax.numpy as jnp
from jax import lax
from jax.experimental import pallas as pl
from jax.experimental.pallas import tpu as pltpu

F32 = jnp.float32
BF16 = jnp.bfloat16
I32 = jnp.int32

EPS = 1e-6
LANES = 128
SUBLANES = 8
VMEM_LIMIT = 56 * 2**20

N_HEADS_A = 16
HEAD_DIM_A = 128
Q_RANK = 512
KV_RANK = 256
N_HEADS_IDX = 16
HEAD_DIM_IDX = 64
TOP_K_MAX = 256
Q_BLOCK = 128
KEY_CHUNK = 512
N_BUCKETS = 32
MAX_DISTANCE = 128
SSM_HEAD_DIM = 64
SSM_GROUPS = 8
D_STATE = 128
CONV_WIDTH = 4
CHUNK = 128
NEG = -1e30
NEG_INF_KEY = -2139095041


def _cparams(*sem):
    return pltpu.CompilerParams(dimension_semantics=sem, vmem_limit_bytes=VMEM_LIMIT)


def _silu(x):
    return x * jax.nn.sigmoid(x)


def _split3(v):
    hi = v.astype(BF16)
    r1 = v - hi.astype(F32)
    mid = r1.astype(BF16)
    lo = (r1 - mid.astype(F32)).astype(BF16)
    return hi, mid, lo


def _dot(a, b):
    return jnp.dot(a, b, preferred_element_type=F32)


def _dot_nt(a, b):
    return lax.dot_general(a, b, (((1,), (1,)), ((), ())), preferred_element_type=F32)


def _ada_kernel(c_ref, w_ref, b_ref, o_ref):
    a = _silu(c_ref[...])
    o_ref[...] = _dot(a, w_ref[...]) + b_ref[...]


def _ada(c8, w, b):
    m, d = c8.shape
    n = w.shape[1]
    tn = 1024
    return pl.pallas_call(
        _ada_kernel,
        grid=(n // tn,),
        in_specs=[pl.BlockSpec((m, d), lambda j: (0, 0)),
                  pl.BlockSpec((d, tn), lambda j: (0, j)),
                  pl.BlockSpec((1, tn), lambda j: (0, j))],
        out_specs=pl.BlockSpec((m, tn), lambda j: (0, j)),
        out_shape=jax.ShapeDtypeStruct((m, n), F32),
        compiler_params=_cparams("parallel"),
        name="ada",
    )(c8, w, b)


def _norm_kernel(x_ref, g_ref, sc_ref, sh_ref, o_ref):
    x = x_ref[...]
    ms = jnp.mean(x * x, axis=-1, keepdims=True)
    y = x * lax.rsqrt(ms + EPS) * g_ref[...]
    o_ref[...] = (y * (1.0 + sc_ref[0]) + sh_ref[0]).astype(o_ref.dtype)


def _norm_mod(x2, g, sc, sh, seq):
    t, d = x2.shape
    tm = 512
    per_b = seq // tm
    return pl.pallas_call(
        _norm_kernel,
        grid=(t // tm,),
        in_specs=[pl.BlockSpec((tm, d), lambda i: (i, 0)),
                  pl.BlockSpec((1, d), lambda i: (0, 0)),
                  pl.BlockSpec((1, 1, d), lambda i: (i // per_b, 0, 0)),
                  pl.BlockSpec((1, 1, d), lambda i: (i // per_b, 0, 0))],
        out_specs=pl.BlockSpec((tm, d), lambda i: (i, 0)),
        out_shape=jax.ShapeDtypeStruct((t, d), BF16),
        compiler_params=_cparams("parallel"),
        name="norm_mod",
    )(x2, g, sc, sh)


def _mm_kernel(*refs, nk, n_extra, epilogue):
    a_ref, w_ref = refs[0], refs[1]
    extra = refs[2:2 + n_extra]
    o_ref = refs[2 + n_extra]
    if nk == 1:
        acc = _dot(a_ref[...], w_ref[...])
        o_ref[...] = epilogue(acc, *extra).astype(o_ref.dtype)
        return
    acc_ref = refs[3 + n_extra]
    k = pl.program_id(2)

    @pl.when(k == 0)
    def _():
        acc_ref[...] = jnp.zeros_like(acc_ref)

    acc_ref[...] += _dot(a_ref[...], w_ref[...])

    @pl.when(k == nk - 1)
    def _():
        o_ref[...] = epilogue(acc_ref[...], *extra).astype(o_ref.dtype)


def _matmul(name, a, w, out_dtype, *, tm, tn, tk=None, epilogue=None, extras=(), extra_specs=()):
    m, kdim = a.shape
    n = w.shape[1]
    tk = kdim if tk is None else tk
    nk = kdim // tk
    assert m % tm == 0 and n % tn == 0 and kdim % tk == 0
    if epilogue is None:
        epilogue = lambda acc: acc
    scratch = [pltpu.VMEM((tm, tn), F32)] if nk > 1 else []
    return pl.pallas_call(
        functools.partial(_mm_kernel, nk=nk, n_extra=len(extras), epilogue=epilogue),
        grid=(m // tm, n // tn, nk),
        in_specs=[pl.BlockSpec((tm, tk), lambda i, j, k: (i, k)),
                  pl.BlockSpec((tk, tn), lambda i, j, k: (k, j)),
                  *extra_specs],
        out_specs=pl.BlockSpec((tm, tn), lambda i, j, k: (i, j)),
        out_shape=jax.ShapeDtypeStruct((m, n), out_dtype),
        scratch_shapes=scratch,
        compiler_params=_cparams("parallel", "parallel", "arbitrary"),
        name=name,
    )(a, w, *extras)


def _tile_spec(tm, tn):
    return pl.BlockSpec((tm, tn), lambda i, j, k: (i, j))


def _batch_row_spec(tn, per_b):
    return pl.BlockSpec((1, 1, tn), lambda i, j, k: (i // per_b, 0, j))


def _prep_kernel(small_ref, cqg_ref, ckvg_ref, kg_ref, kb_ref, wuq_ref, wiq_ref, wukt_ref,
                 ql_ref, qi_ref, wt_ref, ckv_ref, kidx_ref, *, tm):
    nqb = tm // Q_BLOCK
    cq = small_ref[:, 0:Q_RANK]
    ms = jnp.mean(cq * cq, axis=-1, keepdims=True)
    cqn = (cq * lax.rsqrt(ms + EPS) * cqg_ref[...]).astype(BF16)

    q = _dot(cqn, wuq_ref[...]).astype(BF16)
    scale = HEAD_DIM_A ** -0.5
    for h in range(N_HEADS_A):
        qlh = (_dot(q[:, h * HEAD_DIM_A:(h + 1) * HEAD_DIM_A], wukt_ref[h]) * scale).astype(BF16)
        for b in range(nqb):
            ql_ref[b, h * Q_BLOCK:(h + 1) * Q_BLOCK, :] = qlh[b * Q_BLOCK:(b + 1) * Q_BLOCK, :]

    qix = _dot(cqn, wiq_ref[...]).astype(BF16)
    for h in range(N_HEADS_IDX):
        for b in range(nqb):
            qi_ref[b, h * Q_BLOCK:(h + 1) * Q_BLOCK, :] = (
                qix[b * Q_BLOCK:(b + 1) * Q_BLOCK, h * LANES:(h + 1) * LANES])

    ckv = small_ref[:, Q_RANK:Q_RANK + KV_RANK]
    ms = jnp.mean(ckv * ckv, axis=-1, keepdims=True)
    ckv_ref[...] = (ckv * lax.rsqrt(ms + EPS) * ckvg_ref[...]).astype(BF16)

    slab = small_ref[:, Q_RANK + KV_RANK:Q_RANK + KV_RANK + LANES]
    lane = lax.broadcasted_iota(I32, slab.shape, 1)
    kmask = lane < HEAD_DIM_IDX
    mu = jnp.sum(jnp.where(kmask, slab, 0.0), axis=-1, keepdims=True) * (1.0 / HEAD_DIM_IDX)
    xc = jnp.where(kmask, slab - mu, 0.0)
    var = jnp.sum(xc * xc, axis=-1, keepdims=True) * (1.0 / HEAD_DIM_IDX)
    y = xc * lax.rsqrt(var + EPS) * kg_ref[...] + kb_ref[...]
    kidx_ref[...] = jnp.where(kmask, y, 0.0).astype(BF16)

    wscale = N_HEADS_IDX ** -0.5 * HEAD_DIM_IDX ** -0.5
    slab_t = slab.T
    wt = slab_t[HEAD_DIM_IDX:HEAD_DIM_IDX + N_HEADS_IDX, :] * wscale
    for b in range(nqb):
        wt_ref[b] = wt[:, b * Q_BLOCK:(b + 1) * Q_BLOCK]


def _prep(small, cqg, ckvg, kg, kb, wuq, wiq, wukt):
    t = small.shape[0]
    tm = 512
    nqb = tm // Q_BLOCK
    nq = t // Q_BLOCK
    hq = N_HEADS_A * Q_BLOCK
    const2 = lambda i: (0, 0)
    return pl.pallas_call(
        functools.partial(_prep_kernel, tm=tm),
        grid=(t // tm,),
        in_specs=[pl.BlockSpec((tm, small.shape[1]), lambda i: (i, 0)),
                  pl.BlockSpec(cqg.shape, const2),
                  pl.BlockSpec(ckvg.shape, const2),
                  pl.BlockSpec(kg.shape, const2),
                  pl.BlockSpec(kb.shape, const2),
                  pl.BlockSpec(wuq.shape, const2),
                  pl.BlockSpec(wiq.shape, const2),
                  pl.BlockSpec(wukt.shape, lambda i: (0, 0, 0))],
        out_specs=[pl.BlockSpec((nqb, hq, KV_RANK), lambda i: (i, 0, 0)),
                   pl.BlockSpec((nqb, hq, LANES), lambda i: (i, 0, 0)),
                   pl.BlockSpec((nqb, N_HEADS_IDX, Q_BLOCK), lambda i: (i, 0, 0)),
                   pl.BlockSpec((tm, KV_RANK), lambda i: (i, 0)),
                   pl.BlockSpec((tm, LANES), lambda i: (i, 0))],
        out_shape=[jax.ShapeDtypeStruct((nq, hq, KV_RANK), BF16),
                   jax.ShapeDtypeStruct((nq, hq, LANES), BF16),
                   jax.ShapeDtypeStruct((nq, N_HEADS_IDX, Q_BLOCK), F32),
                   jax.ShapeDtypeStruct((t, KV_RANK), BF16),
                   jax.ShapeDtypeStruct((t, LANES), BF16)],
        compiler_params=_cparams("parallel"),
        name="dsa_prep",
    )(small, cqg, ckvg, kg, kb, wuq, wiq, wukt)


def _bucket_tiles():
    t = np.arange(Q_BLOCK)[:, None]
    s = np.arange(Q_BLOCK)[None, :]
    max_exact = N_BUCKETS // 2
    tiles = []
    for d in range(3):
        n = np.maximum(t - s + d * Q_BLOCK, 0)
        nf = np.maximum(n, 1).astype(np.float32)
        large = max_exact + (np.log(nf / max_exact) / math.log(MAX_DISTANCE / max_exact)
                             * (N_BUCKETS - max_exact)).astype(np.int32)
        large = np.minimum(large, N_BUCKETS - 1)
        tiles.append(np.where(n < max_exact, n, large))
    assert 2 * Q_BLOCK - (Q_BLOCK - 1) >= MAX_DISTANCE
    tiles[2] = np.full_like(tiles[2], N_BUCKETS - 1)
    return np.stack(tiles).astype(np.int32)


def _bias_kernel(rel_ref, bucket_ref, o_ref):
    h = pl.program_id(0)
    bucket = bucket_ref[...]
    out = jnp.zeros(bucket.shape, F32)
    for b in range(N_BUCKETS):
        out = jnp.where(bucket == b, rel_ref[b, h], out)
    o_ref[:, 0] = out


def _bias_tiles(rel_bias):
    buckets = jnp.asarray(_bucket_tiles())
    return pl.pallas_call(
        _bias_kernel,
        grid=(N_HEADS_A,),
        in_specs=[pl.BlockSpec(memory_space=pltpu.SMEM),
                  pl.BlockSpec((3, Q_BLOCK, Q_BLOCK), lambda h: (0, 0, 0))],
        out_specs=pl.BlockSpec((3, 1, Q_BLOCK, Q_BLOCK), lambda h: (0, h, 0, 0)),
        out_shape=jax.ShapeDtypeStruct((3, N_HEADS_A, Q_BLOCK, Q_BLOCK), F32),
        compiler_params=_cparams("arbitrary"),
        name="rel_bias_tiles",
    )(rel_bias, buckets)


def _attn_kernel(qi_ref, wt_ref, kidx_ref, ckv_ref, ql_ref, bt_ref, wuv_ref, ya_ref,
                 key_s, mb_s, st_s, s_s, p_s, acc_s, m_s, l_s, j_s, *, seq, top_k):
    ck = KEY_CHUNK
    sub = ck // Q_BLOCK
    i = pl.program_id(1)
    nchunk = (i + sub) // sub
    t_row = i * Q_BLOCK + lax.broadcasted_iota(I32, (1, Q_BLOCK), 1)
    qi = qi_ref[0]

    def score_chunk(c, carry):
        k0 = pl.multiple_of(c * ck, ck)
        kc = kidx_ref[0, pl.ds(k0, ck), :]
        st_s[...] = _dot_nt(kc, qi)
        for u in range(sub):
            acc = jnp.zeros((Q_BLOCK, Q_BLOCK), F32)
            for h in range(N_HEADS_IDX):
                sh = st_s[u * Q_BLOCK:(u + 1) * Q_BLOCK, h * Q_BLOCK:(h + 1) * Q_BLOCK]
                acc = acc + jnp.maximum(sh, 0.0) * wt_ref[0, h:h + 1, :]
            s_pos = k0 + u * Q_BLOCK + lax.broadcasted_iota(I32, (Q_BLOCK, Q_BLOCK), 0)
            acc = jnp.where(s_pos <= t_row, acc, -jnp.inf)
            bits = lax.bitcast_convert_type(acc, I32)
            key_s[c, u * Q_BLOCK:(u + 1) * Q_BLOCK, :] = jnp.where(bits < 0, bits ^ 0x7FFFFFFF, bits)
        return carry

    lax.fori_loop(0, nchunk, score_chunk, 0)

    def count(pred):
        def body(c, cnt):
            tile = key_s[c]
            s_pos = c * ck + lax.broadcasted_iota(I32, (ck, Q_BLOCK), 0)
            hit = jnp.where(pred(tile, s_pos), 1, 0).astype(I32)
            return cnt + jnp.sum(hit.reshape(ck // SUBLANES, SUBLANES, Q_BLOCK), axis=0)
        cnt = lax.fori_loop(0, nchunk, body, jnp.zeros((SUBLANES, Q_BLOCK), I32))
        return jnp.sum(cnt, axis=0, keepdims=True)

    def bisect(_, lohi):
        lo, hi = lohi
        mid = (lo >> 1) + (hi >> 1) + (lo & hi & 1)
        ok = count(lambda tile, s_pos: tile >= mid) >= top_k
        return jnp.where(ok, mid, lo), jnp.where(ok, hi, mid)

    lo0 = jnp.full((1, Q_BLOCK), -2**31, I32)
    hi0 = jnp.full((1, Q_BLOCK), 2**31 - 1, I32)
    thr, _ = lax.fori_loop(0, 32, bisect, (lo0, hi0))

    n_ge = count(lambda tile, s_pos: tile >= thr)
    n_gt = count(lambda tile, s_pos: tile > thr)
    need = top_k - n_gt
    tie = (n_ge > top_k) & (thr > NEG_INF_KEY)
    j_s[...] = jnp.full(j_s.shape, seq, I32)

    @pl.when(jnp.max(jnp.where(tie, 1, 0)) > 0)
    def _():
        def jbisect(_, lohi):
            lo, hi = lohi
            mid = (lo + hi) >> 1
            ok = count(lambda tile, s_pos: (tile == thr) & (s_pos <= mid)) >= need
            return jnp.where(ok, lo, mid), jnp.where(ok, mid, hi)
        lo_j = jnp.full((1, Q_BLOCK), -1, I32)
        hi_j = jnp.full((1, Q_BLOCK), seq - 1, I32)
        _, cut = lax.fori_loop(0, max(1, (seq - 1).bit_length()) + 1, jbisect, (lo_j, hi_j))
        j_s[...] = jnp.broadcast_to(jnp.where(tie, cut, seq), j_s.shape)

    jcut = j_s[0:1, :]

    def mask_chunk(c, carry):
        tile = key_s[c]
        s_pos = c * ck + lax.broadcasted_iota(I32, (ck, Q_BLOCK), 0)
        sel = (tile > thr) | ((tile == thr) & (s_pos <= jcut))
        sel = sel & (s_pos <= t_row)
        mbt = jnp.where(sel, 0.0, NEG)
        for u in range(sub):
            mb_s[c, :, u * Q_BLOCK:(u + 1) * Q_BLOCK] = mbt[u * Q_BLOCK:(u + 1) * Q_BLOCK, :].T
        return carry

    lax.fori_loop(0, nchunk, mask_chunk, 0)

    m_s[...] = jnp.full(m_s.shape, -jnp.inf, F32)
    l_s[...] = jnp.zeros(l_s.shape, F32)
    acc_s[...] = jnp.zeros(acc_s.shape, F32)

    def attend_chunk(c, carry):
        k0 = pl.multiple_of(c * ck, ck)
        kc = ckv_ref[0, pl.ds(k0, ck), :]
        s_s[...] = _dot_nt(ql_ref[0], kc)
        mb = mb_s[c]

        def head(h, carry2):
            r0 = pl.multiple_of(h * Q_BLOCK, Q_BLOCK)
            rows = pl.ds(r0, Q_BLOCK)
            x = s_s[rows, :] + mb
            parts = []
            for u in range(sub):
                d = jnp.clip(i - (c * sub + u), 0, 2)
                parts.append(x[:, u * Q_BLOCK:(u + 1) * Q_BLOCK] + bt_ref[d, h])
            x = jnp.concatenate(parts, axis=1)
            m_prev = m_s[rows, :]
            m_new = jnp.maximum(m_prev, jnp.max(x, axis=1, keepdims=True))
            alpha = jnp.exp(m_prev - m_new)
            p = jnp.exp(x - m_new)
            l_s[rows, :] = alpha * l_s[rows, :] + jnp.sum(p, axis=1, keepdims=True)
            m_s[rows, :] = m_new
            p_s[rows, :] = p.astype(BF16)
            acc_s[rows, :] = acc_s[rows, :] * alpha
            return carry2

        lax.fori_loop(0, N_HEADS_A, head, 0)
        acc_s[...] += _dot(p_s[...], kc)
        return carry

    lax.fori_loop(0, nchunk, attend_chunk, 0)

    for h in range(N_HEADS_A):
        rows = slice(h * Q_BLOCK, (h + 1) * Q_BLOCK)
        o = (acc_s[rows, :] / l_s[rows, :]).astype(BF16)
        ya_ref[0, :, h * HEAD_DIM_A:(h + 1) * HEAD_DIM_A] = _dot(o, wuv_ref[h]).astype(ya_ref.dtype)


def _attention(qi, wt, kidx, ckv, ql, bt, wuv, batch, seq):
    nq = seq // Q_BLOCK
    hq = N_HEADS_A * Q_BLOCK
    ck = KEY_CHUNK
    assert seq % ck == 0
    top_k = min(TOP_K_MAX, seq // 4)
    kidx3 = kidx.reshape(batch, seq, LANES)
    ckv3 = ckv.reshape(batch, seq, KV_RANK)
    return pl.pallas_call(
        functools.partial(_attn_kernel, seq=seq, top_k=top_k),
        grid=(batch, nq),
        in_specs=[pl.BlockSpec((1, hq, LANES), lambda b, i: (b * nq + i, 0, 0)),
                  pl.BlockSpec((1, N_HEADS_IDX, Q_BLOCK), lambda b, i: (b * nq + i, 0, 0)),
                  pl.BlockSpec((1, seq, LANES), lambda b, i: (b, 0, 0)),
                  pl.BlockSpec((1, seq, KV_RANK), lambda b, i: (b, 0, 0)),
                  pl.BlockSpec((1, hq, KV_RANK), lambda b, i: (b * nq + i, 0, 0)),
                  pl.BlockSpec(bt.shape, lambda b, i: (0, 0, 0, 0)),
                  pl.BlockSpec(wuv.shape, lambda b, i: (0, 0, 0))],
        out_specs=pl.BlockSpec((1, Q_BLOCK, N_HEADS_A * HEAD_DIM_A), lambda b, i: (b, i, 0)),
        out_shape=jax.ShapeDtypeStruct((batch, seq, N_HEADS_A * HEAD_DIM_A), BF16),
        scratch_shapes=[pltpu.VMEM((seq // ck, ck, Q_BLOCK), I32),
                        pltpu.VMEM((seq // ck, Q_BLOCK, ck), F32),
                        pltpu.VMEM((ck, hq), F32),
                        pltpu.VMEM((hq, ck), F32),
                        pltpu.VMEM((hq, ck), BF16),
                        pltpu.VMEM((hq, KV_RANK), F32),
                        pltpu.VMEM((hq, 1), F32),
                        pltpu.VMEM((hq, 1), F32),
                        pltpu.VMEM((SUBLANES, Q_BLOCK), I32)],
        compiler_params=_cparams("parallel", "arbitrary"),
        name="dsa_attention",
    )(qi, wt, kidx3, ckv3, ql, bt, wuv)


def _ssd_kernel(xbc_ref, zs_ref, dt_ref, convw_ref, convb_ref, dtb_ref, aneg_ref, dskip_ref, ng_ref,
                e_ref, y_ref, xs_s, state_s, ea_s, de_s, *, d_inner):
    c = pl.program_id(1)
    hg = d_inner // SSM_GROUPS
    heads_g = hg // SSM_HEAD_DIM
    b_off = d_inner
    c_off = d_inner + SSM_GROUPS * D_STATE

    @pl.when(c == 0)
    def _():
        state_s[...] = jnp.zeros_like(state_s)
        xs_s[0:SUBLANES, :] = jnp.zeros((SUBLANES, xs_s.shape[1]), F32)

    xs_s[SUBLANES:SUBLANES + CHUNK, :] = xbc_ref[0].astype(F32)

    def conv(a, w):
        acc = convb_ref[:, a:a + w]
        for k in range(CONV_WIDTH):
            r = SUBLANES - (CONV_WIDTH - 1) + k
            acc = acc + xs_s[r:r + CHUNK, a:a + w] * convw_ref[k:k + 1, a:a + w]
        return _silu(acc)

    x = dt_ref[0] + dtb_ref[...]
    dt = jnp.maximum(x, 0.0) + jnp.log1p(jnp.exp(-jnp.abs(x)))
    da = dt * aneg_ref[...]
    tri_i = lax.broadcasted_iota(I32, (CHUNK, CHUNK), 0) >= lax.broadcasted_iota(I32, (CHUNK, CHUNK), 1)
    tri = jnp.where(tri_i, 1.0, 0.0).astype(BF16)
    acs = sum(_dot(tri, part) for part in _split3(da))
    acs_t = acs.T
    dt_t = dt.T
    acs_last = acs[CHUNK - 1:CHUNK, :]
    e = e_ref[...]
    ea_s[...] = sum(_dot(part, e) for part in _split3(jnp.exp(acs)))
    de_s[...] = sum(_dot(part, e) for part in _split3(jnp.exp(acs_last - acs) * dt))
    el = jnp.broadcast_to(jnp.exp(acs_last), (SUBLANES, LANES))
    el_x = sum(_dot(part, e) for part in _split3(el))[0:1, :]
    lane = lax.broadcasted_iota(I32, (CHUNK, LANES), 1)

    for g in range(SSM_GROUPS):
        cols = slice(g * hg, (g + 1) * hg)
        xg = conv(g * hg, hg)
        bg = conv(b_off + g * D_STATE, D_STATE)
        cg = conv(c_off + g * D_STATE, D_STATE).astype(BF16)
        cb = _dot_nt(cg, bg.astype(BF16))
        sg = state_s[g]
        y = _dot(cg, sg.astype(BF16)) * ea_s[:, cols]
        xg_b = xg.astype(BF16)
        tiles = []
        for pr in range(heads_g // 2):
            xp = xg_b[:, pr * LANES:(pr + 1) * LANES]
            outs = []
            for q in range(2):
                h = g * heads_g + pr * 2 + q
                seg = jnp.broadcast_to(acs[:, h:h + 1], (CHUNK, CHUNK)) - acs_t[h:h + 1, :]
                decay = jnp.where(tri_i, jnp.exp(jnp.where(tri_i, seg, 0.0)), 0.0)
                w = cb * decay * dt_t[h:h + 1, :]
                outs.append(_dot(w.astype(BF16), xp))
            tiles.append(jnp.where(lane < SSM_HEAD_DIM, outs[0], outs[1]))
        y = y + jnp.concatenate(tiles, axis=1) + dskip_ref[:, cols] * xg
        xd = (xg * de_s[:, cols]).astype(BF16)
        state_s[g] = sg * el_x[:, cols] + _dot(bg.T.astype(BF16), xd)
        yz = y * zs_ref[0, :, cols].astype(F32)
        ms = jnp.mean(yz * yz, axis=-1, keepdims=True)
        y_ref[0, :, cols] = (yz * lax.rsqrt(ms + EPS) * ng_ref[:, cols]).astype(y_ref.dtype)

    xs_s[0:SUBLANES, :] = xs_s[CHUNK:CHUNK + SUBLANES, :]


def _ssd(xbc, zs, small3, convw, convb, dtb, aneg, dskip_x, ng, expand, dt_block):
    batch, seq, d_xbc = xbc.shape
    d_inner = zs.shape[2]
    nc = seq // CHUNK
    const2 = lambda b, c: (0, 0)
    return pl.pallas_call(
        functools.partial(_ssd_kernel, d_inner=d_inner),
        grid=(batch, nc),
        in_specs=[pl.BlockSpec((1, CHUNK, d_xbc), lambda b, c: (b, c, 0)),
                  pl.BlockSpec((1, CHUNK, d_inner), lambda b, c: (b, c, 0)),
                  pl.BlockSpec((1, CHUNK, LANES), lambda b, c: (b, c, dt_block)),
                  pl.BlockSpec(convw.shape, const2),
                  pl.BlockSpec(convb.shape, const2),
                  pl.BlockSpec(dtb.shape, const2),
                  pl.BlockSpec(aneg.shape, const2),
                  pl.BlockSpec(dskip_x.shape, const2),
                  pl.BlockSpec(ng.shape, const2),
                  pl.BlockSpec(expand.shape, const2)],
        out_specs=pl.BlockSpec((1, CHUNK, d_inner), lambda b, c: (b, c, 0)),
        out_shape=jax.ShapeDtypeStruct((batch, seq, d_inner), BF16),
        scratch_shapes=[pltpu.VMEM((CHUNK + 2 * SUBLANES, d_xbc), F32),
                        pltpu.VMEM((SSM_GROUPS, D_STATE, d_inner // SSM_GROUPS), F32),
                        pltpu.VMEM((CHUNK, d_inner), F32),
                        pltpu.VMEM((CHUNK, d_inner), F32)],
        compiler_params=_cparams("parallel", "arbitrary"),
        name="ssd",
    )(xbc, zs, small3, convw, convb, dtb, aneg, dskip_x, ng, expand)


def _pad_cols(a, n):
    return jnp.pad(a, ((0, 0), (0, n - a.shape[1])))


def _layer(x2, mod, batch, seq, norm1_g, w_in, cq_norm_g, ckv_norm_g, kidx_norm_g, kidx_norm_b, w_uq, w_iq,
           w_uk, w_uv, rel_bias, conv_w, conv_b, dt_bias, a_log, d_skip, ssm_norm_g, w_proj_a, w_proj_b,
           w_out, norm2_g, w_gate, w_up, w_down, final_g):
    t, d = x2.shape
    n_heads_b = dt_bias.shape[0]
    d_inner = n_heads_b * SSM_HEAD_DIM
    d_xbc = d_inner + 2 * SSM_GROUPS * D_STATE
    ha = N_HEADS_A * HEAD_DIM_A
    sh1, sc1, g1, sh2, sc2, g2 = [m.reshape(batch, 1, d) for m in jnp.split(mod, 6, axis=-1)]

    o_z = Q_RANK + KV_RANK + HEAD_DIM_IDX + N_HEADS_IDX
    o_xbc = o_z + d_inner
    o_dt = o_xbc + d_xbc
    o_g = o_dt + n_heads_b
    small_w = 1024
    dt_col = 896
    assert o_z <= dt_col and dt_col + n_heads_b <= small_w and dt_col % LANES == 0
    w_small = jnp.concatenate(
        [_pad_cols(w_in[:, :o_z], dt_col), _pad_cols(w_in[:, o_dt:o_g], small_w - dt_col)], axis=1).astype(BF16)
    w_z = w_in[:, o_z:o_xbc].astype(BF16)
    w_xbc = w_in[:, o_xbc:o_dt].astype(BF16)
    w_g = w_in[:, o_g:].astype(BF16)

    tm = 1024
    per_b = seq // tm
    h = _norm_mod(x2, norm1_g.reshape(1, d), sc1, sh1, seq)
    small = _matmul("in_small", h, w_small, F32, tm=tm, tn=small_w)
    zs = _matmul("in_z", h, w_z, BF16, tm=tm, tn=1024, epilogue=lambda acc: _silu(acc))
    xbc = _matmul("in_xbc", h, w_xbc, BF16, tm=tm, tn=1024)
    sg = _matmul("in_gates", h, w_g, BF16, tm=tm, tn=1024, epilogue=lambda acc: jax.nn.sigmoid(acc))

    w_iq_p = jnp.pad(w_iq.reshape(Q_RANK, N_HEADS_IDX, HEAD_DIM_IDX),
                     ((0, 0), (0, 0), (0, LANES - HEAD_DIM_IDX))).reshape(Q_RANK, N_HEADS_IDX * LANES).astype(BF16)
    w_uk_t = jnp.transpose(w_uk.reshape(KV_RANK, N_HEADS_A, HEAD_DIM_A), (1, 2, 0)).astype(BF16)
    w_uv_h = jnp.transpose(w_uv.reshape(KV_RANK, N_HEADS_A, HEAD_DIM_A), (1, 0, 2)).astype(BF16)
    ql, qi, wt, ckv, kidx = _prep(
        small, cq_norm_g.reshape(1, Q_RANK), ckv_norm_g.reshape(1, KV_RANK),
        _pad_cols(kidx_norm_g.reshape(1, HEAD_DIM_IDX), LANES), _pad_cols(kidx_norm_b.reshape(1, HEAD_DIM_IDX), LANES),
        w_uq.astype(BF16), w_iq_p, w_uk_t)
    bt = _bias_tiles(rel_bias)
    y_a = _attention(qi, wt, kidx, ckv, ql, bt, w_uv_h, batch, seq).reshape(t, ha)

    expand = np.zeros((LANES, d_inner), np.float32)
    for hh in range(n_heads_b):
        expand[hh, hh * SSM_HEAD_DIM:(hh + 1) * SSM_HEAD_DIM] = 1.0
    aneg = _pad_cols(-jnp.exp(a_log.astype(F32)).reshape(1, n_heads_b), LANES)
    y_b = _ssd(xbc.reshape(batch, seq, d_xbc), zs.reshape(batch, seq, d_inner), small.reshape(batch, seq, small_w),
               conv_w, conv_b.reshape(1, d_xbc), _pad_cols(dt_bias.reshape(1, n_heads_b), LANES), aneg,
               jnp.repeat(d_skip, SSM_HEAD_DIM).reshape(1, d_inner), ssm_norm_g.reshape(1, d_inner),
               jnp.asarray(expand, BF16), dt_col // LANES).reshape(t, d_inner)

    tn = 1024
    nj = d // tn
    pa = _matmul("proj_a", y_a, w_proj_a.astype(BF16), F32, tm=tm, tn=tn,
                 epilogue=lambda acc, s_ref: s_ref[...].astype(F32) * acc,
                 extras=(sg,), extra_specs=(_tile_spec(tm, tn),))
    merged = _matmul("proj_b", y_b, w_proj_b.astype(BF16), BF16, tm=tm, tn=tn,
                     epilogue=lambda acc, s_ref, p_ref: p_ref[...] + s_ref[...].astype(F32) * acc,
                     extras=(sg, pa),
                     extra_specs=(pl.BlockSpec((tm, tn), lambda i, j, k: (i, j + nj)), _tile_spec(tm, tn)))
    x1 = _matmul("w_out", merged, w_out.astype(BF16), F32, tm=tm, tn=tn,
                 epilogue=lambda acc, x_ref, g_ref: x_ref[...] + g_ref[0] * acc,
                 extras=(x2, g1), extra_specs=(_tile_spec(tm, tn), _batch_row_spec(tn, per_b)))

    h2 = _norm_mod(x1, norm2_g.reshape(1, d), sc2, sh2, seq)
    d_ff = w_gate.shape[1]
    tf = 512
    up = _matmul("ffn_up", h2, w_up.astype(BF16), BF16, tm=tm, tn=tf)
    act = _matmul("ffn_gate", h2, w_gate.astype(BF16), BF16, tm=tm, tn=tf,
                  epilogue=lambda acc, u_ref: _silu(acc) * u_ref[...].astype(F32),
                  extras=(up,), extra_specs=(_tile_spec(tm, tf),))

    def down_epilogue(acc, x_ref, g_ref, f_ref):
        xo = x_ref[...] + g_ref[0] * acc
        ms = jnp.mean(xo * xo, axis=-1, keepdims=True)
        return xo * lax.rsqrt(ms + EPS) * f_ref[...]

    tmd = 512
    assert d_ff % 4 == 0
    return _matmul("ffn_down", act, w_down.astype(BF16), F32, tm=tmd, tn=d, tk=d_ff // 4,
                   epilogue=down_epilogue, extras=(x1, g2, final_g.reshape(1, d)),
                   extra_specs=(_tile_spec(tmd, d), _batch_row_spec(d, seq // tmd),
                                pl.BlockSpec((1, d), lambda i, j, k: (0, 0))))


def kernel(x, c, w_ada, b_ada, norm1_g, w_in, cq_norm_g, ckv_norm_g, kidx_norm_g, kidx_norm_b, w_uq, w_iq, w_uk, w_uv, rel_bias, conv_w, conv_b, dt_bias, a_log, d_skip, ssm_norm_g, w_proj_a, w_proj_b, w_out, norm2_g, w_gate, w_up, w_down, final_g):
    batch, seq, d = x.shape
    assert w_ada.shape[0] == 1, "single-layer block"
    c8 = jnp.pad(c, ((0, SUBLANES - batch % SUBLANES if batch % SUBLANES else 0), (0, 0)))
    mod = _ada(c8, w_ada[0], b_ada[0].reshape(1, -1))[:batch]
    out = _layer(x.reshape(batch * seq, d), mod, batch, seq, norm1_g[0], w_in[0], cq_norm_g[0], ckv_norm_g[0],
                 kidx_norm_g[0], kidx_norm_b[0], w_uq[0], w_iq[0], w_uk[0], w_uv[0], rel_bias, conv_w[0],
                 conv_b[0], dt_bias[0], a_log[0], d_skip[0], ssm_norm_g[0], w_proj_a[0], w_proj_b[0], w_out[0],
                 norm2_g[0], w_gate[0], w_up[0], w_down[0], final_g)
    return out.reshape(batch, seq, d)
```

```python
import functools
import math

import numpy as np
import jax
import jax.numpy as jnp
from jax import lax
from jax.experimental import pallas as pl
from jax.experimental.pallas import tpu as pltpu

F32 = jnp.float32
BF16 = jnp.bfloat16
I32 = jnp.int32

EPS = 1e-6
LANES = 128
SUBLANES = 8
VMEM_LIMIT = 56 * 2**20

N_HEADS_A = 16
HEAD_DIM_A = 128
Q_RANK = 512
KV_RANK = 256
N_HEADS_IDX = 16
HEAD_DIM_IDX = 64
TOP_K_MAX = 256
Q_BLOCK = 128
KEY_CHUNK = 512
HEAD_GROUP = 4
LOG2E = 1.4426950408889634
N_BUCKETS = 32
MAX_DISTANCE = 128
SSM_HEAD_DIM = 64
SSM_GROUPS = 8
D_STATE = 128
CONV_WIDTH = 4
CHUNK = 128
NEG = -1e30
NEG_INF_KEY = -2139095041


def _cparams(*sem):
    return pltpu.CompilerParams(dimension_semantics=sem, vmem_limit_bytes=VMEM_LIMIT)


def _silu(x):
    return x * jax.nn.sigmoid(x)


def _split3(v):
    hi = v.astype(BF16)
    r1 = v - hi.astype(F32)
    mid = r1.astype(BF16)
    lo = (r1 - mid.astype(F32)).astype(BF16)
    return hi, mid, lo


def _dot(a, b):
    return jnp.dot(a, b, preferred_element_type=F32)


def _dot_nt(a, b):
    return lax.dot_general(a, b, (((1,), (1,)), ((), ())), preferred_element_type=F32)


def _ada_kernel(c_ref, w_ref, b_ref, o_ref):
    a = _silu(c_ref[...])
    o_ref[...] = _dot(a, w_ref[...]) + b_ref[...]


def _ada(c8, w, b):
    m, d = c8.shape
    n = w.shape[1]
    tn = 1024
    return pl.pallas_call(
        _ada_kernel,
        grid=(n // tn,),
        in_specs=[pl.BlockSpec((m, d), lambda j: (0, 0)),
                  pl.BlockSpec((d, tn), lambda j: (0, j)),
                  pl.BlockSpec((1, tn), lambda j: (0, j))],
        out_specs=pl.BlockSpec((m, tn), lambda j: (0, j)),
        out_shape=jax.ShapeDtypeStruct((m, n), F32),
        compiler_params=_cparams("parallel"),
        name="ada",
    )(c8, w, b)


def _norm_kernel(x_ref, g_ref, sc_ref, sh_ref, o_ref):
    x = x_ref[...]
    ms = jnp.mean(x * x, axis=-1, keepdims=True)
    y = x * lax.rsqrt(ms + EPS) * g_ref[...]
    o_ref[...] = (y * (1.0 + sc_ref[0]) + sh_ref[0]).astype(o_ref.dtype)


def _norm_mod(x2, g, sc, sh, seq):
    t, d = x2.shape
    tm = 512
    per_b = seq // tm
    return pl.pallas_call(
        _norm_kernel,
        grid=(t // tm,),
        in_specs=[pl.BlockSpec((tm, d), lambda i: (i, 0)),
                  pl.BlockSpec((1, d), lambda i: (0, 0)),
                  pl.BlockSpec((1, 1, d), lambda i: (i // per_b, 0, 0)),
                  pl.BlockSpec((1, 1, d), lambda i: (i // per_b, 0, 0))],
        out_specs=pl.BlockSpec((tm, d), lambda i: (i, 0)),
        out_shape=jax.ShapeDtypeStruct((t, d), BF16),
        compiler_params=_cparams("parallel"),
        name="norm_mod",
    )(x2, g, sc, sh)


def _mm_kernel(*refs, nk, n_extra, epilogue):
    a_ref, w_ref = refs[0], refs[1]
    extra = refs[2:2 + n_extra]
    o_ref = refs[2 + n_extra]
    if nk == 1:
        acc = _dot(a_ref[...], w_ref[...])
        o_ref[...] = epilogue(acc, *extra).astype(o_ref.dtype)
        return
    acc_ref = refs[3 + n_extra]
    k = pl.program_id(2)

    @pl.when(k == 0)
    def _():
        acc_ref[...] = jnp.zeros_like(acc_ref)

    acc_ref[...] += _dot(a_ref[...], w_ref[...])

    @pl.when(k == nk - 1)
    def _():
        o_ref[...] = epilogue(acc_ref[...], *extra).astype(o_ref.dtype)


def _matmul(name, a, w, out_dtype, *, tm, tn, tk=None, epilogue=None, extras=(), extra_specs=()):
    m, kdim = a.shape
    n = w.shape[1]
    tk = kdim if tk is None else tk
    nk = kdim // tk
    assert m % tm == 0 and n % tn == 0 and kdim % tk == 0
    if epilogue is None:
        epilogue = lambda acc: acc
    scratch = [pltpu.VMEM((tm, tn), F32)] if nk > 1 else []
    return pl.pallas_call(
        functools.partial(_mm_kernel, nk=nk, n_extra=len(extras), epilogue=epilogue),
        grid=(m // tm, n // tn, nk),
        in_specs=[pl.BlockSpec((tm, tk), lambda i, j, k: (i, k)),
                  pl.BlockSpec((tk, tn), lambda i, j, k: (k, j)),
                  *extra_specs],
        out_specs=pl.BlockSpec((tm, tn), lambda i, j, k: (i, j)),
        out_shape=jax.ShapeDtypeStruct((m, n), out_dtype),
        scratch_shapes=scratch,
        compiler_params=_cparams("parallel", "parallel", "arbitrary"),
        name=name,
    )(a, w, *extras)


def _tile_spec(tm, tn):
    return pl.BlockSpec((tm, tn), lambda i, j, k: (i, j))


def _batch_row_spec(tn, per_b):
    return pl.BlockSpec((1, 1, tn), lambda i, j, k: (i // per_b, 0, j))


def _prep_kernel(small_ref, cqg_ref, ckvg_ref, kg_ref, kb_ref, wuq_ref, wiq_ref, wukt_ref,
                 ql_ref, qi_ref, wt_ref, ckv_ref, kidx_ref, *, tm):
    nqb = tm // Q_BLOCK
    cq = small_ref[:, 0:Q_RANK]
    ms = jnp.mean(cq * cq, axis=-1, keepdims=True)
    cqn = (cq * lax.rsqrt(ms + EPS) * cqg_ref[...]).astype(BF16)

    q = _dot(cqn, wuq_ref[...]).astype(BF16)
    scale = HEAD_DIM_A ** -0.5 * LOG2E
    for h in range(N_HEADS_A):
        qlh = (_dot(q[:, h * HEAD_DIM_A:(h + 1) * HEAD_DIM_A], wukt_ref[h]) * scale).astype(BF16)
        for b in range(nqb):
            ql_ref[b, h * Q_BLOCK:(h + 1) * Q_BLOCK, :] = qlh[b * Q_BLOCK:(b + 1) * Q_BLOCK, :]

    qix = _dot(cqn, wiq_ref[...]).astype(BF16)
    for h in range(N_HEADS_IDX):
        for b in range(nqb):
            qi_ref[b, h * Q_BLOCK:(h + 1) * Q_BLOCK, :] = (
                qix[b * Q_BLOCK:(b + 1) * Q_BLOCK, h * LANES:(h + 1) * LANES])

    ckv = small_ref[:, Q_RANK:Q_RANK + KV_RANK]
    ms = jnp.mean(ckv * ckv, axis=-1, keepdims=True)
    ckv_ref[...] = (ckv * lax.rsqrt(ms + EPS) * ckvg_ref[...]).astype(BF16)

    slab = small_ref[:, Q_RANK + KV_RANK:Q_RANK + KV_RANK + LANES]
    lane = lax.broadcasted_iota(I32, slab.shape, 1)
    kmask = lane < HEAD_DIM_IDX
    mu = jnp.sum(jnp.where(kmask, slab, 0.0), axis=-1, keepdims=True) * (1.0 / HEAD_DIM_IDX)
    xc = jnp.where(kmask, slab - mu, 0.0)
    var = jnp.sum(xc * xc, axis=-1, keepdims=True) * (1.0 / HEAD_DIM_IDX)
    y = xc * lax.rsqrt(var + EPS) * kg_ref[...] + kb_ref[...]
    kidx_ref[...] = jnp.where(kmask, y, 0.0).astype(BF16)

    wscale = N_HEADS_IDX ** -0.5 * HEAD_DIM_IDX ** -0.5
    slab_t = slab.T
    wt = slab_t[HEAD_DIM_IDX:HEAD_DIM_IDX + N_HEADS_IDX, :] * wscale
    for b in range(nqb):
        wt_ref[b] = wt[:, b * Q_BLOCK:(b + 1) * Q_BLOCK]


def _prep(small, cqg, ckvg, kg, kb, wuq, wiq, wukt):
    t = small.shape[0]
    tm = 512
    nqb = tm // Q_BLOCK
    nq = t // Q_BLOCK
    hq = N_HEADS_A * Q_BLOCK
    const2 = lambda i: (0, 0)
    return pl.pallas_call(
        functools.partial(_prep_kernel, tm=tm),
        grid=(t // tm,),
        in_specs=[pl.BlockSpec((tm, small.shape[1]), lambda i: (i, 0)),
                  pl.BlockSpec(cqg.shape, const2),
                  pl.BlockSpec(ckvg.shape, const2),
                  pl.BlockSpec(kg.shape, const2),
                  pl.BlockSpec(kb.shape, const2),
                  pl.BlockSpec(wuq.shape, const2),
                  pl.BlockSpec(wiq.shape, const2),
                  pl.BlockSpec(wukt.shape, lambda i: (0, 0, 0))],
        out_specs=[pl.BlockSpec((nqb, hq, KV_RANK), lambda i: (i, 0, 0)),
                   pl.BlockSpec((nqb, hq, LANES), lambda i: (i, 0, 0)),
                   pl.BlockSpec((nqb, N_HEADS_IDX, Q_BLOCK), lambda i: (i, 0, 0)),
                   pl.BlockSpec((tm, KV_RANK), lambda i: (i, 0)),
                   pl.BlockSpec((tm, LANES), lambda i: (i, 0))],
        out_shape=[jax.ShapeDtypeStruct((nq, hq, KV_RANK), BF16),
                   jax.ShapeDtypeStruct((nq, hq, LANES), BF16),
                   jax.ShapeDtypeStruct((nq, N_HEADS_IDX, Q_BLOCK), F32),
                   jax.ShapeDtypeStruct((t, KV_RANK), BF16),
                   jax.ShapeDtypeStruct((t, LANES), BF16)],
        compiler_params=_cparams("parallel"),
        name="dsa_prep",
    )(small, cqg, ckvg, kg, kb, wuq, wiq, wukt)


def _bucket_tiles():
    t = np.arange(Q_BLOCK)[:, None]
    s = np.arange(Q_BLOCK)[None, :]
    max_exact = N_BUCKETS // 2
    tiles = []
    for d in range(3):
        n = np.maximum(t - s + d * Q_BLOCK, 0)
        nf = np.maximum(n, 1).astype(np.float32)
        large = max_exact + (np.log(nf / max_exact) / math.log(MAX_DISTANCE / max_exact)
                             * (N_BUCKETS - max_exact)).astype(np.int32)
        large = np.minimum(large, N_BUCKETS - 1)
        tiles.append(np.where(n < max_exact, n, large))
    assert 2 * Q_BLOCK - (Q_BLOCK - 1) >= MAX_DISTANCE
    tiles[2] = np.full_like(tiles[2], N_BUCKETS - 1)
    return np.stack(tiles).astype(np.int32)


def _bias_kernel(rel_ref, bucket_ref, o_ref):
    h = pl.program_id(0)
    bucket = bucket_ref[...]
    out = jnp.zeros(bucket.shape, F32)
    for b in range(N_BUCKETS):
        out = jnp.where(bucket == b, rel_ref[b, h], out)
    o_ref[:, 0] = (out - rel_ref[N_BUCKETS - 1, h]) * LOG2E


def _bias_tiles(rel_bias):
    buckets = jnp.asarray(_bucket_tiles())
    return pl.pallas_call(
        _bias_kernel,
        grid=(N_HEADS_A,),
        in_specs=[pl.BlockSpec(memory_space=pltpu.SMEM),
                  pl.BlockSpec((3, Q_BLOCK, Q_BLOCK), lambda h: (0, 0, 0))],
        out_specs=pl.BlockSpec((3, 1, Q_BLOCK, Q_BLOCK), lambda h: (0, h, 0, 0)),
        out_shape=jax.ShapeDtypeStruct((3, N_HEADS_A, Q_BLOCK, Q_BLOCK), F32),
        compiler_params=_cparams("arbitrary"),
        name="rel_bias_tiles",
    )(rel_bias, buckets)


def _attn_kernel(qi_ref, wt_ref, kidx_ref, ckv_ref, ql_ref, bt_ref, wuv_ref, ya_ref,
                 key_s, mb_s, st_s, acc_s, m_s, l_s, j_s, *, seq, top_k):
    ck = KEY_CHUNK
    sub = ck // Q_BLOCK
    i = pl.program_id(1)
    nchunk = (i + sub) // sub
    t_row = i * Q_BLOCK + lax.broadcasted_iota(I32, (1, Q_BLOCK), 1)
    qi = qi_ref[0]

    def score_chunk(c, carry):
        k0 = pl.multiple_of(c * ck, ck)
        kc = kidx_ref[0, pl.ds(k0, ck), :]
        st_s[...] = _dot_nt(kc, qi)
        for u in range(sub):
            acc = jnp.zeros((Q_BLOCK, Q_BLOCK), F32)
            for h in range(N_HEADS_IDX):
                sh = st_s[u * Q_BLOCK:(u + 1) * Q_BLOCK, h * Q_BLOCK:(h + 1) * Q_BLOCK]
                acc = acc + jnp.maximum(sh, 0.0) * wt_ref[0, h:h + 1, :]
            s_pos = k0 + u * Q_BLOCK + lax.broadcasted_iota(I32, (Q_BLOCK, Q_BLOCK), 0)
            acc = jnp.where(s_pos <= t_row, acc, -jnp.inf)
            bits = lax.bitcast_convert_type(acc, I32)
            key_s[c, u * Q_BLOCK:(u + 1) * Q_BLOCK, :] = jnp.where(bits < 0, bits ^ 0x7FFFFFFF, bits)
        return carry

    lax.fori_loop(0, nchunk, score_chunk, 0)

    def count(pred):
        def body(c, cnt):
            tile = key_s[c]
            s_pos = c * ck + lax.broadcasted_iota(I32, (ck, Q_BLOCK), 0)
            hit = jnp.where(pred(tile, s_pos), 1, 0).astype(I32)
            return cnt + jnp.sum(hit.reshape(ck // SUBLANES, SUBLANES, Q_BLOCK), axis=0)
        cnt = lax.fori_loop(0, nchunk, body, jnp.zeros((SUBLANES, Q_BLOCK), I32))
        return jnp.sum(cnt, axis=0, keepdims=True)

    def bisect(_, lohi):
        lo, hi = lohi
        mid = (lo >> 1) + (hi >> 1) + (lo & hi & 1)
        ok = count(lambda tile, s_pos: tile >= mid) >= top_k
        return jnp.where(ok, mid, lo), jnp.where(ok, hi, mid)

    lo0 = jnp.full((1, Q_BLOCK), -2**31, I32)
    hi0 = jnp.full((1, Q_BLOCK), 2**31 - 1, I32)
    thr, _ = lax.fori_loop(0, 32, bisect, (lo0, hi0))

    n_ge = count(lambda tile, s_pos: tile >= thr)
    n_gt = count(lambda tile, s_pos: tile > thr)
    need = top_k - n_gt
    tie = (n_ge > top_k) & (thr > NEG_INF_KEY)
    j_s[...] = jnp.full(j_s.shape, seq, I32)

    @pl.when(jnp.max(jnp.where(tie, 1, 0)) > 0)
    def _():
        def jbisect(_, lohi):
            lo, hi = lohi
            mid = (lo + hi) >> 1
            ok = count(lambda tile, s_pos: (tile == thr) & (s_pos <= mid)) >= need
            return jnp.where(ok, lo, mid), jnp.where(ok, mid, hi)
        lo_j = jnp.full((1, Q_BLOCK), -1, I32)
        hi_j = jnp.full((1, Q_BLOCK), seq - 1, I32)
        _, cut = lax.fori_loop(0, max(1, (seq - 1).bit_length()) + 1, jbisect, (lo_j, hi_j))
        j_s[...] = jnp.broadcast_to(jnp.where(tie, cut, seq), j_s.shape)

    jcut = j_s[0:1, :]

    def mask_chunk(c, carry):
        tile = key_s[c]
        s_pos = c * ck + lax.broadcasted_iota(I32, (ck, Q_BLOCK), 0)
        sel = (tile > thr) | ((tile == thr) & (s_pos <= jcut))
        sel = sel & (s_pos <= t_row)
        mbt = jnp.where(sel, 0.0, NEG)
        for u in range(sub):
            mb_s[c, :, u * Q_BLOCK:(u + 1) * Q_BLOCK] = mbt[u * Q_BLOCK:(u + 1) * Q_BLOCK, :].T
        return carry

    lax.fori_loop(0, nchunk, mask_chunk, 0)

    m_s[...] = jnp.full(m_s.shape, -jnp.inf, F32)
    l_s[...] = jnp.zeros(l_s.shape, F32)
    acc_s[...] = jnp.zeros(acc_s.shape, F32)

    def attend_chunk(c, with_bias):
        k0 = pl.multiple_of(c * ck, ck)
        kc = ckv_ref[0, pl.ds(k0, ck), :]
        mb = mb_s[c]
        for g in range(N_HEADS_A // HEAD_GROUP):
            grows = slice(g * HEAD_GROUP * Q_BLOCK, (g + 1) * HEAD_GROUP * Q_BLOCK)
            s = _dot_nt(ql_ref[0, grows, :], kc)
            ps, alphas = [], []
            for hh in range(HEAD_GROUP):
                h = g * HEAD_GROUP + hh
                rows = slice(h * Q_BLOCK, (h + 1) * Q_BLOCK)
                x = s[hh * Q_BLOCK:(hh + 1) * Q_BLOCK, :] + mb
                if with_bias:
                    parts = []
                    for u in range(sub):
                        d = jnp.clip(i - (c * sub + u), 0, 2)
                        parts.append(x[:, u * Q_BLOCK:(u + 1) * Q_BLOCK] + bt_ref[d, h])
                    x = jnp.concatenate(parts, axis=1)
                m_prev = m_s[rows, :]
                m_new = jnp.maximum(m_prev, jnp.max(x, axis=1, keepdims=True))
                alpha = jnp.exp2(m_prev - m_new)
                p = jnp.exp2(x - m_new)
                l_s[rows, :] = alpha * l_s[rows, :] + jnp.sum(p, axis=1, keepdims=True)
                m_s[rows, :] = m_new
                ps.append(p.astype(BF16))
                alphas.append(alpha)
            pv = _dot(jnp.concatenate(ps, axis=0), kc)
            acc_s[grows, :] = jnp.concatenate(alphas, axis=0) * acc_s[grows, :] + pv

    def attend_far(c, carry):
        attend_chunk(c, False)
        return carry

    def attend_near(c, carry):
        attend_chunk(c, True)
        return carry

    nfar = jnp.maximum((i - 1) // sub, 0)
    lax.fori_loop(0, nfar, attend_far, 0)
    lax.fori_loop(nfar, nchunk, attend_near, 0)

    for h in range(N_HEADS_A):
        rows = slice(h * Q_BLOCK, (h + 1) * Q_BLOCK)
        o = (acc_s[rows, :] / l_s[rows, :]).astype(BF16)
        ya_ref[0, :, h * HEAD_DIM_A:(h + 1) * HEAD_DIM_A] = _dot(o, wuv_ref[h]).astype(ya_ref.dtype)


def _attention(qi, wt, kidx, ckv, ql, bt, wuv, batch, seq):
    nq = seq // Q_BLOCK
    hq = N_HEADS_A * Q_BLOCK
    ck = KEY_CHUNK
    assert seq % ck == 0
    top_k = min(TOP_K_MAX, seq // 4)
    kidx3 = kidx.reshape(batch, seq, LANES)
    ckv3 = ckv.reshape(batch, seq, KV_RANK)
    return pl.pallas_call(
        functools.partial(_attn_kernel, seq=seq, top_k=top_k),
        grid=(batch, nq),
        in_specs=[pl.BlockSpec((1, hq, LANES), lambda b, i: (b * nq + i, 0, 0)),
                  pl.BlockSpec((1, N_HEADS_IDX, Q_BLOCK), lambda b, i: (b * nq + i, 0, 0)),
                  pl.BlockSpec((1, seq, LANES), lambda b, i: (b, 0, 0)),
                  pl.BlockSpec((1, seq, KV_RANK), lambda b, i: (b, 0, 0)),
                  pl.BlockSpec((1, hq, KV_RANK), lambda b, i: (b * nq + i, 0, 0)),
                  pl.BlockSpec(bt.shape, lambda b, i: (0, 0, 0, 0)),
                  pl.BlockSpec(wuv.shape, lambda b, i: (0, 0, 0))],
        out_specs=pl.BlockSpec((1, Q_BLOCK, N_HEADS_A * HEAD_DIM_A), lambda b, i: (b, i, 0)),
        out_shape=jax.ShapeDtypeStruct((batch, seq, N_HEADS_A * HEAD_DIM_A), BF16),
        scratch_shapes=[pltpu.VMEM((seq // ck, ck, Q_BLOCK), I32),
                        pltpu.VMEM((seq // ck, Q_BLOCK, ck), F32),
                        pltpu.VMEM((ck, hq), F32),
                        pltpu.VMEM((hq, KV_RANK), F32),
                        pltpu.VMEM((hq, 1), F32),
                        pltpu.VMEM((hq, 1), F32),
                        pltpu.VMEM((SUBLANES, Q_BLOCK), I32)],
        compiler_params=_cparams("parallel", "arbitrary"),
        name="dsa_attention",
    )(qi, wt, kidx3, ckv3, ql, bt, wuv)


def _ssd_kernel(xbc_ref, zs_ref, dt_ref, convw_ref, convb_ref, dtb_ref, aneg_ref, dskip_ref, ng_ref,
                e_ref, y_ref, xs_s, state_s, ea_s, de_s, *, d_inner):
    c = pl.program_id(1)
    hg = d_inner // SSM_GROUPS
    heads_g = hg // SSM_HEAD_DIM
    b_off = d_inner
    c_off = d_inner + SSM_GROUPS * D_STATE

    @pl.when(c == 0)
    def _():
        state_s[...] = jnp.zeros_like(state_s)
        xs_s[0:SUBLANES, :] = jnp.zeros((SUBLANES, xs_s.shape[1]), F32)

    xs_s[SUBLANES:SUBLANES + CHUNK, :] = xbc_ref[0].astype(F32)

    def conv(a, w):
        acc = convb_ref[:, a:a + w]
        for k in range(CONV_WIDTH):
            r = SUBLANES - (CONV_WIDTH - 1) + k
            acc = acc + xs_s[r:r + CHUNK, a:a + w] * convw_ref[k:k + 1, a:a + w]
        return _silu(acc)

    x = dt_ref[0] + dtb_ref[...]
    dt = jnp.maximum(x, 0.0) + jnp.log1p(jnp.exp(-jnp.abs(x)))
    da = dt * aneg_ref[...]
    tri_i = lax.broadcasted_iota(I32, (CHUNK, CHUNK), 0) >= lax.broadcasted_iota(I32, (CHUNK, CHUNK), 1)
    tri = jnp.where(tri_i, 1.0, 0.0).astype(BF16)
    acs = sum(_dot(tri, part) for part in _split3(da))
    acs_t = acs.T
    dt_t = dt.T
    acs_last = acs[CHUNK - 1:CHUNK, :]
    e = e_ref[...]
    ea_s[...] = sum(_dot(part, e) for part in _split3(jnp.exp(acs)))
    de_s[...] = sum(_dot(part, e) for part in _split3(jnp.exp(acs_last - acs) * dt))
    el = jnp.broadcast_to(jnp.exp(acs_last), (SUBLANES, LANES))
    el_x = sum(_dot(part, e) for part in _split3(el))[0:1, :]
    lane = lax.broadcasted_iota(I32, (CHUNK, LANES), 1)

    for g in range(SSM_GROUPS):
        cols = slice(g * hg, (g + 1) * hg)
        xg = conv(g * hg, hg)
        bg = conv(b_off + g * D_STATE, D_STATE)
        cg = conv(c_off + g * D_STATE, D_STATE).astype(BF16)
        cb = _dot_nt(cg, bg.astype(BF16))
        sg = state_s[g]
        y = _dot(cg, sg.astype(BF16)) * ea_s[:, cols]
        xg_b = xg.astype(BF16)
        tiles = []
        for pr in range(heads_g // 2):
            xp = xg_b[:, pr * LANES:(pr + 1) * LANES]
            outs = []
            for q in range(2):
                h = g * heads_g + pr * 2 + q
                seg = jnp.broadcast_to(acs[:, h:h + 1], (CHUNK, CHUNK)) - acs_t[h:h + 1, :]
                decay = jnp.where(tri_i, jnp.exp(jnp.where(tri_i, seg, 0.0)), 0.0)
                w = cb * decay * dt_t[h:h + 1, :]
                outs.append(_dot(w.astype(BF16), xp))
            tiles.append(jnp.where(lane < SSM_HEAD_DIM, outs[0], outs[1]))
        y = y + jnp.concatenate(tiles, axis=1) + dskip_ref[:, cols] * xg
        xd = (xg * de_s[:, cols]).astype(BF16)
        state_s[g] = sg * el_x[:, cols] + _dot(bg.T.astype(BF16), xd)
        yz = y * zs_ref[0, :, cols].astype(F32)
        ms = jnp.mean(yz * yz, axis=-1, keepdims=True)
        y_ref[0, :, cols] = (yz * lax.rsqrt(ms + EPS) * ng_ref[:, cols]).astype(y_ref.dtype)

    xs_s[0:SUBLANES, :] = xs_s[CHUNK:CHUNK + SUBLANES, :]


def _ssd(xbc, zs, small3, convw, convb, dtb, aneg, dskip_x, ng, expand, dt_block):
    batch, seq, d_xbc = xbc.shape
    d_inner = zs.shape[2]
    nc = seq // CHUNK
    const2 = lambda b, c: (0, 0)
    return pl.pallas_call(
        functools.partial(_ssd_kernel, d_inner=d_inner),
        grid=(batch, nc),
        in_specs=[pl.BlockSpec((1, CHUNK, d_xbc), lambda b, c: (b, c, 0)),
                  pl.BlockSpec((1, CHUNK, d_inner), lambda b, c: (b, c, 0)),
                  pl.BlockSpec((1, CHUNK, LANES), lambda b, c: (b, c, dt_block)),
                  pl.BlockSpec(convw.shape, const2),
                  pl.BlockSpec(convb.shape, const2),
                  pl.BlockSpec(dtb.shape, const2),
                  pl.BlockSpec(aneg.shape, const2),
                  pl.BlockSpec(dskip_x.shape, const2),
                  pl.BlockSpec(ng.shape, const2),
                  pl.BlockSpec(expand.shape, const2)],
        out_specs=pl.BlockSpec((1, CHUNK, d_inner), lambda b, c: (b, c, 0)),
        out_shape=jax.ShapeDtypeStruct((batch, seq, d_inner), BF16),
        scratch_shapes=[pltpu.VMEM((CHUNK + 2 * SUBLANES, d_xbc), F32),
                        pltpu.VMEM((SSM_GROUPS, D_STATE, d_inner // SSM_GROUPS), F32),
                        pltpu.VMEM((CHUNK, d_inner), F32),
                        pltpu.VMEM((CHUNK, d_inner), F32)],
        compiler_params=_cparams("parallel", "arbitrary"),
        name="ssd",
    )(xbc, zs, small3, convw, convb, dtb, aneg, dskip_x, ng, expand)


def _pad_cols(a, n):
    return jnp.pad(a, ((0, 0), (0, n - a.shape[1])))


def _layer(x2, mod, batch, seq, norm1_g, w_in, cq_norm_g, ckv_norm_g, kidx_norm_g, kidx_norm_b, w_uq, w_iq,
           w_uk, w_uv, rel_bias, conv_w, conv_b, dt_bias, a_log, d_skip, ssm_norm_g, w_proj_a, w_proj_b,
           w_out, norm2_g, w_gate, w_up, w_down, final_g):
    t, d = x2.shape
    n_heads_b = dt_bias.shape[0]
    d_inner = n_heads_b * SSM_HEAD_DIM
    d_xbc = d_inner + 2 * SSM_GROUPS * D_STATE
    ha = N_HEADS_A * HEAD_DIM_A
    sh1, sc1, g1, sh2, sc2, g2 = [m.reshape(batch, 1, d) for m in jnp.split(mod, 6, axis=-1)]

    o_z = Q_RANK + KV_RANK + HEAD_DIM_IDX + N_HEADS_IDX
    o_xbc = o_z + d_inner
    o_dt = o_xbc + d_xbc
    o_g = o_dt + n_heads_b
    small_w = 1024
    dt_col = 896
    assert o_z <= dt_col and dt_col + n_heads_b <= small_w and dt_col % LANES == 0
    w_small = jnp.concatenate(
        [_pad_cols(w_in[:, :o_z], dt_col), _pad_cols(w_in[:, o_dt:o_g], small_w - dt_col)], axis=1).astype(BF16)
    w_z = w_in[:, o_z:o_xbc].astype(BF16)
    w_xbc = w_in[:, o_xbc:o_dt].astype(BF16)
    w_g = w_in[:, o_g:].astype(BF16)

    tm = 1024
    per_b = seq // tm
    h = _norm_mod(x2, norm1_g.reshape(1, d), sc1, sh1, seq)
    small = _matmul("in_small", h, w_small, F32, tm=tm, tn=small_w)
    zs = _matmul("in_z", h, w_z, BF16, tm=tm, tn=1024, epilogue=lambda acc: _silu(acc))
    xbc = _matmul("in_xbc", h, w_xbc, BF16, tm=tm, tn=1024)
    sg = _matmul("in_gates", h, w_g, BF16, tm=tm, tn=1024, epilogue=lambda acc: jax.nn.sigmoid(acc))

    w_iq_p = jnp.pad(w_iq.reshape(Q_RANK, N_HEADS_IDX, HEAD_DIM_IDX),
                     ((0, 0), (0, 0), (0, LANES - HEAD_DIM_IDX))).reshape(Q_RANK, N_HEADS_IDX * LANES).astype(BF16)
    w_uk_t = jnp.transpose(w_uk.reshape(KV_RANK, N_HEADS_A, HEAD_DIM_A), (1, 2, 0)).astype(BF16)
    w_uv_h = jnp.transpose(w_uv.reshape(KV_RANK, N_HEADS_A, HEAD_DIM_A), (1, 0, 2)).astype(BF16)
    ql, qi, wt, ckv, kidx = _prep(
        small, cq_norm_g.reshape(1, Q_RANK), ckv_norm_g.reshape(1, KV_RANK),
        _pad_cols(kidx_norm_g.reshape(1, HEAD_DIM_IDX), LANES), _pad_cols(kidx_norm_b.reshape(1, HEAD_DIM_IDX), LANES),
        w_uq.astype(BF16), w_iq_p, w_uk_t)
    bt = _bias_tiles(rel_bias)
    y_a = _attention(qi, wt, kidx, ckv, ql, bt, w_uv_h, batch, seq).reshape(t, ha)

    expand = np.zeros((LANES, d_inner), np.float32)
    for hh in range(n_heads_b):
        expand[hh, hh * SSM_HEAD_DIM:(hh + 1) * SSM_HEAD_DIM] = 1.0
    aneg = _pad_cols(-jnp.exp(a_log.astype(F32)).reshape(1, n_heads_b), LANES)
    y_b = _ssd(xbc.reshape(batch, seq, d_xbc), zs.reshape(batch, seq, d_inner), small.reshape(batch, seq, small_w),
               conv_w, conv_b.reshape(1, d_xbc), _pad_cols(dt_bias.reshape(1, n_heads_b), LANES), aneg,
               jnp.repeat(d_skip, SSM_HEAD_DIM).reshape(1, d_inner), ssm_norm_g.reshape(1, d_inner),
               jnp.asarray(expand, BF16), dt_col // LANES).reshape(t, d_inner)

    tn = 1024
    nj = d // tn
    pa = _matmul("proj_a", y_a, w_proj_a.astype(BF16), F32, tm=tm, tn=tn,
                 epilogue=lambda acc, s_ref: s_ref[...].astype(F32) * acc,
                 extras=(sg,), extra_specs=(_tile_spec(tm, tn),))
    merged = _matmul("proj_b", y_b, w_proj_b.astype(BF16), BF16, tm=tm, tn=tn,
                     epilogue=lambda acc, s_ref, p_ref: p_ref[...] + s_ref[...].astype(F32) * acc,
                     extras=(sg, pa),
                     extra_specs=(pl.BlockSpec((tm, tn), lambda i, j, k: (i, j + nj)), _tile_spec(tm, tn)))
    x1 = _matmul("w_out", merged, w_out.astype(BF16), F32, tm=tm, tn=tn,
                 epilogue=lambda acc, x_ref, g_ref: x_ref[...] + g_ref[0] * acc,
                 extras=(x2, g1), extra_specs=(_tile_spec(tm, tn), _batch_row_spec(tn, per_b)))

    h2 = _norm_mod(x1, norm2_g.reshape(1, d), sc2, sh2, seq)
    d_ff = w_gate.shape[1]
    tf = 512
    up = _matmul("ffn_up", h2, w_up.astype(BF16), BF16, tm=tm, tn=tf)
    act = _matmul("ffn_gate", h2, w_gate.astype(BF16), BF16, tm=tm, tn=tf,
                  epilogue=lambda acc, u_ref: _silu(acc) * u_ref[...].astype(F32),
                  extras=(up,), extra_specs=(_tile_spec(tm, tf),))

    def down_epilogue(acc, x_ref, g_ref, f_ref):
        xo = x_ref[...] + g_ref[0] * acc
        ms = jnp.mean(xo * xo, axis=-1, keepdims=True)
        return xo * lax.rsqrt(ms + EPS) * f_ref[...]

    tmd = 512
    assert d_ff % 4 == 0
    return _matmul("ffn_down", act, w_down.astype(BF16), F32, tm=tmd, tn=d, tk=d_ff // 4,
                   epilogue=down_epilogue, extras=(x1, g2, final_g.reshape(1, d)),
                   extra_specs=(_tile_spec(tmd, d), _batch_row_spec(d, seq // tmd),
                                pl.BlockSpec((1, d), lambda i, j, k: (0, 0))))


def kernel(x, c, w_ada, b_ada, norm1_g, w_in, cq_norm_g, ckv_norm_g, kidx_norm_g, kidx_norm_b, w_uq, w_iq, w_uk, w_uv, rel_bias, conv_w, conv_b, dt_bias, a_log, d_skip, ssm_norm_g, w_proj_a, w_proj_b, w_out, norm2_g, w_gate, w_up, w_down, final_g):
    batch, seq, d = x.shape
    assert w_ada.shape[0] == 1, "single-layer block"
    c8 = jnp.pad(c, ((0, SUBLANES - batch % SUBLANES if batch % SUBLANES else 0), (0, 0)))
    mod = _ada(c8, w_ada[0], b_ada[0].reshape(1, -1))[:batch]
    out = _layer(x.reshape(batch * seq, d), mod, batch, seq, norm1_g[0], w_in[0], cq_norm_g[0], ckv_norm_g[0],
                 kidx_norm_g[0], kidx_norm_b[0], w_uq[0], w_iq[0], w_uk[0], w_uv[0], rel_bias, conv_w[0],
                 conv_b[0], dt_bias[0], a_log[0], d_skip[0], ssm_norm_g[0], w_proj_a[0], w_proj_b[0], w_out[0],
                 norm2_g[0], w_gate[0], w_up[0], w_down[0], final_g)
    return out.reshape(batch, seq, d)
```

```python
import functools
import math

import numpy as np
import jax
import jax.numpy as jnp
from jax import lax
from jax.experimental import pallas as pl
from jax.experimental.pallas import tpu as pltpu

F32 = jnp.float32
BF16 = jnp.bfloat16
I32 = jnp.int32
I16 = jnp.int16

EPS = 1e-6
LANES = 128
SUBLANES = 8
VMEM_LIMIT = 56 * 2**20

N_HEADS_A = 16
HEAD_DIM_A = 128
Q_RANK = 512
KV_RANK = 256
N_HEADS_IDX = 16
HEAD_DIM_IDX = 64
TOP_K_MAX = 256
Q_BLOCK = 128
KEY_CHUNK = 512
HEAD_GROUP = 4
SM_ROWS = 64
LOG2E = 1.4426950408889634
N_BUCKETS = 32
MAX_DISTANCE = 128
SSM_HEAD_DIM = 64
SSM_GROUPS = 8
D_STATE = 128
CONV_WIDTH = 4
CHUNK = 128
NEG = -1e30
NEG_INF_KEY = -2139095041


def _cparams(*sem):
    return pltpu.CompilerParams(dimension_semantics=sem, vmem_limit_bytes=VMEM_LIMIT)


def _silu(x):
    return x * jax.nn.sigmoid(x)


def _split3(v):
    hi = v.astype(BF16)
    r1 = v - hi.astype(F32)
    mid = r1.astype(BF16)
    lo = (r1 - mid.astype(F32)).astype(BF16)
    return hi, mid, lo


def _dot(a, b):
    return jnp.dot(a, b, preferred_element_type=F32)


def _dot_nt(a, b):
    return lax.dot_general(a, b, (((1,), (1,)), ((), ())), preferred_element_type=F32)


def _ada_kernel(c_ref, w_ref, b_ref, o_ref):
    a = _silu(c_ref[...])
    o_ref[...] = _dot(a, w_ref[...]) + b_ref[...]


def _ada(c8, w, b):
    m, d = c8.shape
    n = w.shape[1]
    tn = 1024
    return pl.pallas_call(
        _ada_kernel,
        grid=(n // tn,),
        in_specs=[pl.BlockSpec((m, d), lambda j: (0, 0)),
                  pl.BlockSpec((d, tn), lambda j: (0, j)),
                  pl.BlockSpec((1, tn), lambda j: (0, j))],
        out_specs=pl.BlockSpec((m, tn), lambda j: (0, j)),
        out_shape=jax.ShapeDtypeStruct((m, n), F32),
        compiler_params=_cparams("parallel"),
        name="ada",
    )(c8, w, b)


def _norm_kernel(x_ref, g_ref, sc_ref, sh_ref, o_ref):
    x = x_ref[...]
    ms = jnp.mean(x * x, axis=-1, keepdims=True)
    y = x * lax.rsqrt(ms + EPS) * g_ref[...]
    o_ref[...] = (y * (1.0 + sc_ref[0]) + sh_ref[0]).astype(o_ref.dtype)


def _norm_mod(x2, g, sc, sh, seq):
    t, d = x2.shape
    tm = 512
    per_b = seq // tm
    return pl.pallas_call(
        _norm_kernel,
        grid=(t // tm,),
        in_specs=[pl.BlockSpec((tm, d), lambda i: (i, 0)),
                  pl.BlockSpec((1, d), lambda i: (0, 0)),
                  pl.BlockSpec((1, 1, d), lambda i: (i // per_b, 0, 0)),
                  pl.BlockSpec((1, 1, d), lambda i: (i // per_b, 0, 0))],
        out_specs=pl.BlockSpec((tm, d), lambda i: (i, 0)),
        out_shape=jax.ShapeDtypeStruct((t, d), BF16),
        compiler_params=_cparams("parallel"),
        name="norm_mod",
    )(x2, g, sc, sh)


def _mm_kernel(*refs, nk, n_extra, epilogue):
    a_ref, w_ref = refs[0], refs[1]
    extra = refs[2:2 + n_extra]
    o_ref = refs[2 + n_extra]
    if nk == 1:
        acc = _dot(a_ref[...], w_ref[...])
        o_ref[...] = epilogue(acc, *extra).astype(o_ref.dtype)
        return
    acc_ref = refs[3 + n_extra]
    k = pl.program_id(2)

    @pl.when(k == 0)
    def _():
        acc_ref[...] = jnp.zeros_like(acc_ref)

    acc_ref[...] += _dot(a_ref[...], w_ref[...])

    @pl.when(k == nk - 1)
    def _():
        o_ref[...] = epilogue(acc_ref[...], *extra).astype(o_ref.dtype)


def _matmul(name, a, w, out_dtype, *, tm, tn, tk=None, epilogue=None, extras=(), extra_specs=()):
    m, kdim = a.shape
    n = w.shape[1]
    tk = kdim if tk is None else tk
    nk = kdim // tk
    assert m % tm == 0 and n % tn == 0 and kdim % tk == 0
    if epilogue is None:
        epilogue = lambda acc: acc
    scratch = [pltpu.VMEM((tm, tn), F32)] if nk > 1 else []
    return pl.pallas_call(
        functools.partial(_mm_kernel, nk=nk, n_extra=len(extras), epilogue=epilogue),
        grid=(m // tm, n // tn, nk),
        in_specs=[pl.BlockSpec((tm, tk), lambda i, j, k: (i, k)),
                  pl.BlockSpec((tk, tn), lambda i, j, k: (k, j)),
                  *extra_specs],
        out_specs=pl.BlockSpec((tm, tn), lambda i, j, k: (i, j)),
        out_shape=jax.ShapeDtypeStruct((m, n), out_dtype),
        scratch_shapes=scratch,
        compiler_params=_cparams("parallel", "parallel", "arbitrary"),
        name=name,
    )(a, w, *extras)


def _tile_spec(tm, tn):
    return pl.BlockSpec((tm, tn), lambda i, j, k: (i, j))


def _batch_row_spec(tn, per_b):
    return pl.BlockSpec((1, 1, tn), lambda i, j, k: (i // per_b, 0, j))


def _mm2_kernel(*refs, shared_a, n_extra, epilogue):
    if shared_a:
        a1_ref, w1_ref, w2_ref = refs[:3]
        a2_ref, rest = a1_ref, refs[3:]
    else:
        a1_ref, w1_ref, a2_ref, w2_ref = refs[:4]
        rest = refs[4:]
    r1 = _dot(a1_ref[...], w1_ref[...])
    r2 = _dot(a2_ref[...], w2_ref[...])
    rest[n_extra][...] = epilogue(r1, r2, *rest[:n_extra]).astype(rest[n_extra].dtype)


def _matmul2(name, a1, w1, a2, w2, out_dtype, *, tm, tn, epilogue, extras=(), extra_specs=()):
    m, k1 = a1.shape
    n = w1.shape[1]
    assert m % tm == 0 and n % tn == 0 and w2.shape[1] == n
    shared = a2 is None
    ins = [a1, w1] + ([] if shared else [a2]) + [w2]
    specs = [pl.BlockSpec((tm, k1), lambda i, j: (i, 0)), pl.BlockSpec((k1, tn), lambda i, j: (0, j))]
    if not shared:
        specs.append(pl.BlockSpec((tm, a2.shape[1]), lambda i, j: (i, 0)))
    specs.append(pl.BlockSpec((w2.shape[0], tn), lambda i, j: (0, j)))
    return pl.pallas_call(
        functools.partial(_mm2_kernel, shared_a=shared, n_extra=len(extras), epilogue=epilogue),
        grid=(m // tm, n // tn),
        in_specs=specs + list(extra_specs),
        out_specs=pl.BlockSpec((tm, tn), lambda i, j: (i, j)),
        out_shape=jax.ShapeDtypeStruct((m, n), out_dtype),
        compiler_params=_cparams("parallel", "parallel"),
        name=name,
    )(*ins, *extras)


def _prep_kernel(small_ref, cqg_ref, ckvg_ref, kg_ref, kb_ref, wuq_ref, wiq_ref, wukt_ref,
                 ql_ref, qi_ref, wt_ref, ckv_ref, kidx_ref, *, tm):
    nqb = tm // Q_BLOCK
    cq = small_ref[:, 0:Q_RANK]
    ms = jnp.mean(cq * cq, axis=-1, keepdims=True)
    cqn = (cq * lax.rsqrt(ms + EPS) * cqg_ref[...]).astype(BF16)

    q = _dot(cqn, wuq_ref[...]).astype(BF16)
    scale = HEAD_DIM_A ** -0.5 * LOG2E
    for h in range(N_HEADS_A):
        qlh = (_dot(q[:, h * HEAD_DIM_A:(h + 1) * HEAD_DIM_A], wukt_ref[h]) * scale).astype(BF16)
        for b in range(nqb):
            ql_ref[b, h * Q_BLOCK:(h + 1) * Q_BLOCK, :] = qlh[b * Q_BLOCK:(b + 1) * Q_BLOCK, :]

    qix = _dot(cqn, wiq_ref[...]).astype(BF16)
    for h in range(N_HEADS_IDX):
        for b in range(nqb):
            qi_ref[b, h * Q_BLOCK:(h + 1) * Q_BLOCK, :] = (
                qix[b * Q_BLOCK:(b + 1) * Q_BLOCK, h * LANES:(h + 1) * LANES])

    ckv = small_ref[:, Q_RANK:Q_RANK + KV_RANK]
    ms = jnp.mean(ckv * ckv, axis=-1, keepdims=True)
    ckv_ref[...] = (ckv * lax.rsqrt(ms + EPS) * ckvg_ref[...]).astype(BF16)

    slab = small_ref[:, Q_RANK + KV_RANK:Q_RANK + KV_RANK + LANES]
    lane = lax.broadcasted_iota(I32, slab.shape, 1)
    kmask = lane < HEAD_DIM_IDX
    mu = jnp.sum(jnp.where(kmask, slab, 0.0), axis=-1, keepdims=True) * (1.0 / HEAD_DIM_IDX)
    xc = jnp.where(kmask, slab - mu, 0.0)
    var = jnp.sum(xc * xc, axis=-1, keepdims=True) * (1.0 / HEAD_DIM_IDX)
    y = xc * lax.rsqrt(var + EPS) * kg_ref[...] + kb_ref[...]
    kidx_ref[...] = jnp.where(kmask, y, 0.0).astype(BF16)

    wscale = N_HEADS_IDX ** -0.5 * HEAD_DIM_IDX ** -0.5
    slab_t = slab.T
    wt = slab_t[HEAD_DIM_IDX:HEAD_DIM_IDX + N_HEADS_IDX, :] * wscale
    for b in range(nqb):
        wt_ref[b] = wt[:, b * Q_BLOCK:(b + 1) * Q_BLOCK]


def _prep(small, cqg, ckvg, kg, kb, wuq, wiq, wukt):
    t = small.shape[0]
    tm = 512
    nqb = tm // Q_BLOCK
    nq = t // Q_BLOCK
    hq = N_HEADS_A * Q_BLOCK
    const2 = lambda i: (0, 0)
    return pl.pallas_call(
        functools.partial(_prep_kernel, tm=tm),
        grid=(t // tm,),
        in_specs=[pl.BlockSpec((tm, small.shape[1]), lambda i: (i, 0)),
                  pl.BlockSpec(cqg.shape, const2),
                  pl.BlockSpec(ckvg.shape, const2),
                  pl.BlockSpec(kg.shape, const2),
                  pl.BlockSpec(kb.shape, const2),
                  pl.BlockSpec(wuq.shape, const2),
                  pl.BlockSpec(wiq.shape, const2),
                  pl.BlockSpec(wukt.shape, lambda i: (0, 0, 0))],
        out_specs=[pl.BlockSpec((nqb, hq, KV_RANK), lambda i: (i, 0, 0)),
                   pl.BlockSpec((nqb, hq, LANES), lambda i: (i, 0, 0)),
                   pl.BlockSpec((nqb, N_HEADS_IDX, Q_BLOCK), lambda i: (i, 0, 0)),
                   pl.BlockSpec((tm, KV_RANK), lambda i: (i, 0)),
                   pl.BlockSpec((tm, LANES), lambda i: (i, 0))],
        out_shape=[jax.ShapeDtypeStruct((nq, hq, KV_RANK), BF16),
                   jax.ShapeDtypeStruct((nq, hq, LANES), BF16),
                   jax.ShapeDtypeStruct((nq, N_HEADS_IDX, Q_BLOCK), F32),
                   jax.ShapeDtypeStruct((t, KV_RANK), BF16),
                   jax.ShapeDtypeStruct((t, LANES), BF16)],
        compiler_params=_cparams("parallel"),
        name="dsa_prep",
    )(small, cqg, ckvg, kg, kb, wuq, wiq, wukt)


def _bucket_tiles():
    t = np.arange(Q_BLOCK)[:, None]
    s = np.arange(Q_BLOCK)[None, :]
    max_exact = N_BUCKETS // 2
    tiles = []
    for d in range(3):
        n = np.maximum(t - s + d * Q_BLOCK, 0)
        nf = np.maximum(n, 1).astype(np.float32)
        large = max_exact + (np.log(nf / max_exact) / math.log(MAX_DISTANCE / max_exact)
                             * (N_BUCKETS - max_exact)).astype(np.int32)
        large = np.minimum(large, N_BUCKETS - 1)
        tiles.append(np.where(n < max_exact, n, large))
    assert 2 * Q_BLOCK - (Q_BLOCK - 1) >= MAX_DISTANCE
    tiles[2] = np.full_like(tiles[2], N_BUCKETS - 1)
    return np.stack(tiles).astype(np.int32)


def _bias_kernel(rel_ref, bucket_ref, o_ref):
    h = pl.program_id(0)
    bucket = bucket_ref[...]
    out = jnp.zeros(bucket.shape, F32)
    for b in range(N_BUCKETS):
        out = jnp.where(bucket == b, rel_ref[b, h], out)
    o_ref[:, 0] = (out - rel_ref[N_BUCKETS - 1, h]) * LOG2E


def _bias_tiles(rel_bias):
    buckets = jnp.asarray(_bucket_tiles())
    return pl.pallas_call(
        _bias_kernel,
        grid=(N_HEADS_A,),
        in_specs=[pl.BlockSpec(memory_space=pltpu.SMEM),
                  pl.BlockSpec((3, Q_BLOCK, Q_BLOCK), lambda h: (0, 0, 0))],
        out_specs=pl.BlockSpec((3, 1, Q_BLOCK, Q_BLOCK), lambda h: (0, h, 0, 0)),
        out_shape=jax.ShapeDtypeStruct((3, N_HEADS_A, Q_BLOCK, Q_BLOCK), F32),
        compiler_params=_cparams("arbitrary"),
        name="rel_bias_tiles",
    )(rel_bias, buckets)


def _attn_kernel(qi_ref, wt_ref, kidx_ref, ckv_ref, ql_ref, bt_ref, wuv_ref, ya_ref,
                 key_s, hi16_s, lo16_s, c16_s, mb_s, st_s, x_buf, p_buf, al_buf, acc_s, m_s, lp_s, j_s,
                 *, seq, top_k):
    ck = KEY_CHUNK
    sub = ck // Q_BLOCK
    i = pl.program_id(1)
    nchunk = (i + sub) // sub
    t_row = i * Q_BLOCK + lax.broadcasted_iota(I32, (1, Q_BLOCK), 1)
    qi = qi_ref[0]

    def score_chunk(c, carry):
        k0 = pl.multiple_of(c * ck, ck)
        kc = kidx_ref[0, pl.ds(k0, ck), :]
        st_s[...] = _dot_nt(kc, qi)
        for u in range(sub):
            acc = jnp.zeros((Q_BLOCK, Q_BLOCK), F32)
            for h in range(N_HEADS_IDX):
                sh = st_s[u * Q_BLOCK:(u + 1) * Q_BLOCK, h * Q_BLOCK:(h + 1) * Q_BLOCK]
                acc = acc + jnp.maximum(sh, 0.0) * wt_ref[0, h:h + 1, :]
            s_pos = k0 + u * Q_BLOCK + lax.broadcasted_iota(I32, (Q_BLOCK, Q_BLOCK), 0)
            acc = jnp.where(s_pos <= t_row, acc, -jnp.inf)
            bits = lax.bitcast_convert_type(acc, I32)
            key = jnp.where(bits < 0, bits ^ 0x7FFFFFFF, bits)
            urows = slice(u * Q_BLOCK, (u + 1) * Q_BLOCK)
            key_s[c, urows, :] = key
            hi16_s[c, urows, :] = (key >> 16).astype(I16)
            lo16_s[c, urows, :] = ((key & 0xFFFF) - 32768).astype(I16)
        return carry

    lax.fori_loop(0, nchunk, score_chunk, 0)

    def count(pred):
        def body(c, cnt):
            tile = key_s[c]
            s_pos = c * ck + lax.broadcasted_iota(I32, (ck, Q_BLOCK), 0)
            hit = jnp.where(pred(tile, s_pos), 1, 0).astype(I32)
            return cnt + jnp.sum(hit.reshape(ck // SUBLANES, SUBLANES, Q_BLOCK), axis=0)
        cnt = lax.fori_loop(0, nchunk, body, jnp.zeros((SUBLANES, Q_BLOCK), I32))
        return jnp.sum(cnt, axis=0, keepdims=True)

    pack_rows = 2 * SUBLANES
    one16 = jnp.ones((pack_rows, Q_BLOCK), I16)
    zero16 = jnp.zeros((pack_rows, Q_BLOCK), I16)

    def pack16(v):
        return jnp.broadcast_to(v, (pack_rows, Q_BLOCK)).astype(I16)

    def bisect16(arr_s):
        def step(_, lohi):
            lo, hi = lohi
            mid = (lo + hi) >> 1
            thr16 = pack16(mid)

            def body(c, cnt):
                tile = arr_s[c]
                hits = [jnp.where(tile[r * pack_rows:(r + 1) * pack_rows, :] >= thr16, one16, zero16)
                        for r in range(ck // pack_rows)]
                while len(hits) > 1:
                    hits = [hits[k] + hits[k + 1] for k in range(0, len(hits), 2)]
                return cnt + hits[0]

            cnt = lax.fori_loop(0, nchunk, body, zero16)
            ok = jnp.sum(cnt.astype(I32), axis=0, keepdims=True) >= top_k
            return jnp.where(ok, mid, lo), jnp.where(ok, hi, mid)

        lo, _ = lax.fori_loop(0, 16, step, (jnp.full((1, Q_BLOCK), -2**15, I32), jnp.full((1, Q_BLOCK), 2**15, I32)))
        return lo

    prefix = bisect16(hi16_s)
    p16 = pack16(prefix)[None]

    def low_chunk(c, carry):
        hi = hi16_s[c].reshape(ck // pack_rows, pack_rows, Q_BLOCK)
        lo = lo16_s[c].reshape(ck // pack_rows, pack_rows, Q_BLOCK)
        c16 = jnp.where(hi > p16, jnp.full((), 2**15 - 1, I16), jnp.where(hi == p16, lo, jnp.full((), -2**15, I16)))
        c16_s[c] = c16.reshape(ck, Q_BLOCK)
        return carry

    lax.fori_loop(0, nchunk, low_chunk, 0)
    thr = prefix * 65536 + (bisect16(c16_s) + 32768)

    n_ge = count(lambda tile, s_pos: tile >= thr)
    n_gt = count(lambda tile, s_pos: tile > thr)
    need = top_k - n_gt
    tie = (n_ge > top_k) & (thr > NEG_INF_KEY)
    j_s[...] = jnp.full(j_s.shape, seq, I32)

    @pl.when(jnp.max(jnp.where(tie, 1, 0)) > 0)
    def _():
        def jbisect(_, lohi):
            lo, hi = lohi
            mid = (lo + hi) >> 1
            ok = count(lambda tile, s_pos: (tile == thr) & (s_pos <= mid)) >= need
            return jnp.where(ok, lo, mid), jnp.where(ok, mid, hi)
        lo_j = jnp.full((1, Q_BLOCK), -1, I32)
        hi_j = jnp.full((1, Q_BLOCK), seq - 1, I32)
        _, cut = lax.fori_loop(0, max(1, (seq - 1).bit_length()) + 1, jbisect, (lo_j, hi_j))
        j_s[...] = jnp.broadcast_to(jnp.where(tie, cut, seq), j_s.shape)

    jcut = j_s[0:1, :]

    def mask_chunk(c, carry):
        tile = key_s[c]
        s_pos = c * ck + lax.broadcasted_iota(I32, (ck, Q_BLOCK), 0)
        sel = (tile > thr) | ((tile == thr) & (s_pos <= jcut))
        sel = sel & (s_pos <= t_row)
        mbt = jnp.where(sel, 0.0, NEG)
        for u in range(sub):
            mb_s[c * sub + u] = mbt[u * Q_BLOCK:(u + 1) * Q_BLOCK, :].T
        return carry

    lax.fori_loop(0, nchunk, mask_chunk, 0)
    neg_tile = seq // Q_BLOCK
    mb_s[neg_tile] = jnp.full((Q_BLOCK, Q_BLOCK), NEG, F32)

    m_s[...] = jnp.full(m_s.shape, -jnp.inf, F32)
    lp_s[...] = jnp.zeros(lp_s.shape, F32)
    acc_s[...] = jnp.zeros(acc_s.shape, F32)

    ngroup = N_HEADS_A // HEAD_GROUP
    grows_n = HEAD_GROUP * Q_BLOCK
    nparts = grows_n // SM_ROWS
    first_start = i + 1 - sub * nchunk

    def chunk_start(c):
        return jnp.maximum(first_start + c * sub, 0)

    def load_kv(c):
        k0 = pl.multiple_of(chunk_start(c) * Q_BLOCK, Q_BLOCK)
        return ckv_ref[0, pl.ds(k0, ck), :]

    def part_rows(g, part):
        h = g * HEAD_GROUP + part * SM_ROWS // Q_BLOCK
        q0 = (part * SM_ROWS) % Q_BLOCK
        rows = slice(h * Q_BLOCK + q0, h * Q_BLOCK + q0 + SM_ROWS)
        return h, q0, rows, slice(part * SM_ROWS, (part + 1) * SM_ROWS)

    def stage_a(c, g, near):
        start = chunk_start(c)
        shared = (c == 0) & (nchunk > 1)
        tiles = []
        for u in range(sub):
            blk = start + u
            tiles.append(jnp.where(shared & (blk >= first_start + sub), neg_tile, blk))
        s = _dot_nt(ql_ref[0, g * grows_n:(g + 1) * grows_n, :], load_kv(c))
        for part in range(nparts):
            h, q0, rows, lrows = part_rows(g, part)
            mx = None
            for u in range(sub):
                cols = slice(u * Q_BLOCK, (u + 1) * Q_BLOCK)
                x = s[lrows, cols] + mb_s[tiles[u], q0:q0 + SM_ROWS, :]
                if near:
                    x = x + bt_ref[jnp.clip(i - (start + u), 0, 2), h, q0:q0 + SM_ROWS, :]
                x_buf[g, lrows, cols] = x
                mx = x if mx is None else jnp.maximum(mx, x)
            m_prev = m_s[rows, :]
            m_cur = jnp.broadcast_to(jnp.max(mx, axis=1, keepdims=True), (SM_ROWS, LANES))
            m_new = jnp.maximum(m_prev, m_cur)
            al_buf[g, lrows, :] = jnp.exp2(m_prev - m_new)
            m_s[rows, :] = m_new

    def stage_b(g):
        for part in range(nparts):
            _, _, rows, lrows = part_rows(g, part)
            m_new = m_s[rows, :]
            ps = [jnp.exp2(x_buf[g, lrows, u * Q_BLOCK:(u + 1) * Q_BLOCK] - m_new) for u in range(sub)]
            lp_s[rows, :] = al_buf[g, lrows, :] * lp_s[rows, :] + ((ps[0] + ps[1]) + (ps[2] + ps[3]))
            p_buf[g, lrows, :] = jnp.concatenate(ps, axis=1).astype(BF16)

    def stage_c(c, g):
        grows = slice(g * grows_n, (g + 1) * grows_n)
        alpha = jnp.concatenate([al_buf[g]] * (KV_RANK // LANES), axis=1)
        acc_s[grows, :] = alpha * acc_s[grows, :] + _dot(p_buf[g], load_kv(c))

    assert sub == 4 and ngroup == 4

    def chunk_steps(c, near, first):
        for k in range(ngroup):
            stage_a(c, k, near)
            if k >= 1 or not first:
                stage_b((k - 1) % ngroup)
            if k >= 2:
                stage_c(c, k - 2)
            elif not first:
                stage_c(c - 1, k + 2)

    def drain(c):
        stage_b(ngroup - 1)
        stage_c(c, ngroup - 2)
        stage_c(c, ngroup - 1)

    last = nchunk - 1

    @pl.when(nchunk == 1)
    def _():
        chunk_steps(last, True, True)
        drain(last)

    @pl.when(nchunk > 1)
    def _():
        chunk_steps(0, False, True)

        def far_body(c, carry):
            chunk_steps(c, False, False)
            return carry

        lax.fori_loop(1, last, far_body, 0)
        chunk_steps(last, True, False)
        drain(last)

    for h in range(N_HEADS_A):
        rows = slice(h * Q_BLOCK, (h + 1) * Q_BLOCK)
        l = jnp.sum(lp_s[rows, :], axis=1, keepdims=True)
        o = (acc_s[rows, :] / l).astype(BF16)
        ya_ref[0, :, h * HEAD_DIM_A:(h + 1) * HEAD_DIM_A] = _dot(o, wuv_ref[h]).astype(ya_ref.dtype)


def _attention(qi, wt, kidx, ckv, ql, bt, wuv, batch, seq):
    nq = seq // Q_BLOCK
    hq = N_HEADS_A * Q_BLOCK
    ck = KEY_CHUNK
    grp = HEAD_GROUP * Q_BLOCK
    ngrp = N_HEADS_A // HEAD_GROUP
    assert seq % ck == 0
    top_k = min(TOP_K_MAX, seq // 4)
    kidx3 = kidx.reshape(batch, seq, LANES)
    ckv3 = ckv.reshape(batch, seq, KV_RANK)
    return pl.pallas_call(
        functools.partial(_attn_kernel, seq=seq, top_k=top_k),
        grid=(batch, nq),
        in_specs=[pl.BlockSpec((1, hq, LANES), lambda b, i: (b * nq + i, 0, 0)),
                  pl.BlockSpec((1, N_HEADS_IDX, Q_BLOCK), lambda b, i: (b * nq + i, 0, 0)),
                  pl.BlockSpec((1, seq, LANES), lambda b, i: (b, 0, 0)),
                  pl.BlockSpec((1, seq, KV_RANK), lambda b, i: (b, 0, 0)),
                  pl.BlockSpec((1, hq, KV_RANK), lambda b, i: (b * nq + i, 0, 0)),
                  pl.BlockSpec(bt.shape, lambda b, i: (0, 0, 0, 0)),
                  pl.BlockSpec(wuv.shape, lambda b, i: (0, 0, 0))],
        out_specs=pl.BlockSpec((1, Q_BLOCK, N_HEADS_A * HEAD_DIM_A), lambda b, i: (b, i, 0)),
        out_shape=jax.ShapeDtypeStruct((batch, seq, N_HEADS_A * HEAD_DIM_A), BF16),
        scratch_shapes=[pltpu.VMEM((seq // ck, ck, Q_BLOCK), I32),
                        pltpu.VMEM((seq // ck, ck, Q_BLOCK), I16),
                        pltpu.VMEM((seq // ck, ck, Q_BLOCK), I16),
                        pltpu.VMEM((seq // ck, ck, Q_BLOCK), I16),
                        pltpu.VMEM((nq + 1, Q_BLOCK, Q_BLOCK), F32),
                        pltpu.VMEM((ck, hq), F32),
                        pltpu.VMEM((ngrp, grp, ck), F32),
                        pltpu.VMEM((ngrp, grp, ck), BF16),
                        pltpu.VMEM((ngrp, grp, LANES), F32),
                        pltpu.VMEM((hq, KV_RANK), F32),
                        pltpu.VMEM((hq, LANES), F32),
                        pltpu.VMEM((hq, LANES), F32),
                        pltpu.VMEM((SUBLANES, Q_BLOCK), I32)],
        compiler_params=_cparams("parallel", "arbitrary"),
        name="dsa_attention",
    )(qi, wt, kidx3, ckv3, ql, bt, wuv)


def _xbc_conv_kernel(a_ref, w_ref, cw_ref, cb_ref, o_ref, tail_s, *, tm, per_b):
    i, j = pl.program_id(0), pl.program_id(1)
    y = _dot(a_ref[...], w_ref[...])
    @pl.when(i % per_b == 0)
    def _():
        tail_s[j] = jnp.zeros(tail_s.shape[1:], F32)

    prev = tail_s[j]
    tail_s[j] = y[tm - SUBLANES:tm, :]
    ext = jnp.concatenate([prev, y], axis=0)
    acc = ext * cw_ref[0:1, :]
    for k in range(1, CONV_WIDTH):
        acc = pltpu.roll(acc, 1, axis=0) + ext * cw_ref[k:k + 1, :]
    o_ref[...] = _silu(acc[SUBLANES:, :] + cb_ref[...]).astype(o_ref.dtype)


def _xbc_conv(a, w, conv_w, conv_b, seq, *, tm, tn):
    m, kdim = a.shape
    n = w.shape[1]
    assert m % tm == 0 and n % tn == 0 and seq % tm == 0
    return pl.pallas_call(
        functools.partial(_xbc_conv_kernel, tm=tm, per_b=seq // tm),
        grid=(m // tm, n // tn),
        in_specs=[pl.BlockSpec((tm, kdim), lambda i, j: (i, 0)),
                  pl.BlockSpec((kdim, tn), lambda i, j: (0, j)),
                  pl.BlockSpec((CONV_WIDTH, tn), lambda i, j: (0, j)),
                  pl.BlockSpec((1, tn), lambda i, j: (0, j))],
        out_specs=pl.BlockSpec((tm, tn), lambda i, j: (i, j)),
        out_shape=jax.ShapeDtypeStruct((m, n), BF16),
        scratch_shapes=[pltpu.VMEM((n // tn, SUBLANES, tn), F32)],
        compiler_params=_cparams("arbitrary", "arbitrary"),
        name="in_xbc_conv",
    )(a, w, conv_w, conv_b)


def _ssd_kernel(xc_ref, zs_ref, dt_ref, dtb_ref, aneg_ref, dskip_ref, ng_ref, e_ref, y_ref,
                state_s, de_s, *, d_inner):
    c = pl.program_id(1)
    hg = d_inner // SSM_GROUPS
    heads_g = hg // SSM_HEAD_DIM
    b_off = d_inner
    c_off = d_inner + SSM_GROUPS * D_STATE

    @pl.when(c == 0)
    def _():
        state_s[...] = jnp.zeros_like(state_s)

    x = dt_ref[0] + dtb_ref[...]
    dt = jnp.maximum(x, 0.0) + jnp.log1p(jnp.exp(-jnp.abs(x)))
    da = dt * aneg_ref[...]
    tri_i = lax.broadcasted_iota(I32, (CHUNK, CHUNK), 0) >= lax.broadcasted_iota(I32, (CHUNK, CHUNK), 1)
    tri = jnp.where(tri_i, 1.0, 0.0).astype(BF16)
    acs = sum(_dot(tri, part) for part in _split3(da))
    acs_t = acs.T
    dt_t = dt.T
    acs_last = acs[CHUNK - 1:CHUNK, :]
    e = e_ref[...]
    de_s[...] = sum(_dot(part, e) for part in _split3(jnp.exp(acs_last - acs) * dt))
    el = jnp.broadcast_to(jnp.exp(acs_last), (SUBLANES, LANES))
    el_x = sum(_dot(part, e) for part in _split3(el))[0:1, :]
    lane = lax.broadcasted_iota(I32, (CHUNK, LANES), 1)

    for g in range(SSM_GROUPS):
        cols = slice(g * hg, (g + 1) * hg)
        xg = xc_ref[0, :, cols]
        bg = xc_ref[0, :, b_off + g * D_STATE:b_off + (g + 1) * D_STATE]
        cg = xc_ref[0, :, c_off + g * D_STATE:c_off + (g + 1) * D_STATE]
        cb = _dot_nt(cg, bg)
        cg_f = cg.astype(F32)
        sg = state_s[g]
        sg_b = sg.astype(BF16)
        tiles = []
        for pr in range(heads_g // 2):
            pcols = slice(pr * LANES, (pr + 1) * LANES)
            rhs = jnp.concatenate([xg[:, pcols], sg_b[:, pcols]], axis=0)
            outs = []
            for q in range(2):
                h = g * heads_g + pr * 2 + q
                col = jnp.broadcast_to(acs[:, h:h + 1], (CHUNK, CHUNK))
                decay = jnp.where(tri_i, jnp.exp(jnp.where(tri_i, col - acs_t[h:h + 1, :], 0.0)), 0.0)
                w = cb * decay * dt_t[h:h + 1, :]
                lhs = jnp.concatenate([w.astype(BF16), (cg_f * jnp.exp(col)).astype(BF16)], axis=1)
                outs.append(_dot(lhs, rhs))
            tiles.append(jnp.where(lane < SSM_HEAD_DIM, outs[0], outs[1]))
        xg_f = xg.astype(F32)
        y = jnp.concatenate(tiles, axis=1) + dskip_ref[:, cols] * xg_f
        xd = (xg_f * de_s[:, cols]).astype(BF16)
        state_s[g] = sg * el_x[:, cols] + _dot(bg.astype(F32).T.astype(BF16), xd)
        yz = y * zs_ref[0, :, cols].astype(F32)
        ms = jnp.mean(yz * yz, axis=-1, keepdims=True)
        y_ref[0, :, cols] = (yz * lax.rsqrt(ms + EPS) * ng_ref[:, cols]).astype(y_ref.dtype)


def _ssd(xc, zs, small3, dtb, aneg, dskip_x, ng, expand, dt_block):
    batch, seq, d_xbc = xc.shape
    d_inner = zs.shape[2]
    nc = seq // CHUNK
    const2 = lambda b, c: (0, 0)
    return pl.pallas_call(
        functools.partial(_ssd_kernel, d_inner=d_inner),
        grid=(batch, nc),
        in_specs=[pl.BlockSpec((1, CHUNK, d_xbc), lambda b, c: (b, c, 0)),
                  pl.BlockSpec((1, CHUNK, d_inner), lambda b, c: (b, c, 0)),
                  pl.BlockSpec((1, CHUNK, LANES), lambda b, c: (b, c, dt_block)),
                  pl.BlockSpec(dtb.shape, const2),
                  pl.BlockSpec(aneg.shape, const2),
                  pl.BlockSpec(dskip_x.shape, const2),
                  pl.BlockSpec(ng.shape, const2),
                  pl.BlockSpec(expand.shape, const2)],
        out_specs=pl.BlockSpec((1, CHUNK, d_inner), lambda b, c: (b, c, 0)),
        out_shape=jax.ShapeDtypeStruct((batch, seq, d_inner), BF16),
        scratch_shapes=[pltpu.VMEM((SSM_GROUPS, D_STATE, d_inner // SSM_GROUPS), F32),
                        pltpu.VMEM((CHUNK, d_inner), F32)],
        compiler_params=_cparams("parallel", "arbitrary"),
        name="ssd",
    )(xc, zs, small3, dtb, aneg, dskip_x, ng, expand)


def _pad_cols(a, n):
    return jnp.pad(a, ((0, 0), (0, n - a.shape[1])))


def _layer(x2, mod, batch, seq, norm1_g, w_in, cq_norm_g, ckv_norm_g, kidx_norm_g, kidx_norm_b, w_uq, w_iq,
           w_uk, w_uv, rel_bias, conv_w, conv_b, dt_bias, a_log, d_skip, ssm_norm_g, w_proj_a, w_proj_b,
           w_out, norm2_g, w_gate, w_up, w_down, final_g):
    t, d = x2.shape
    n_heads_b = dt_bias.shape[0]
    d_inner = n_heads_b * SSM_HEAD_DIM
    d_xbc = d_inner + 2 * SSM_GROUPS * D_STATE
    ha = N_HEADS_A * HEAD_DIM_A
    sh1, sc1, g1, sh2, sc2, g2 = [m.reshape(batch, 1, d) for m in jnp.split(mod, 6, axis=-1)]

    o_z = Q_RANK + KV_RANK + HEAD_DIM_IDX + N_HEADS_IDX
    o_xbc = o_z + d_inner
    o_dt = o_xbc + d_xbc
    o_g = o_dt + n_heads_b
    small_w = 1024
    dt_col = 896
    assert o_z <= dt_col and dt_col + n_heads_b <= small_w and dt_col % LANES == 0
    w_small = jnp.concatenate(
        [_pad_cols(w_in[:, :o_z], dt_col), _pad_cols(w_in[:, o_dt:o_g], small_w - dt_col)], axis=1).astype(BF16)
    w_z = w_in[:, o_z:o_xbc].astype(BF16)
    w_xbc = w_in[:, o_xbc:o_dt].astype(BF16)
    w_g = w_in[:, o_g:].astype(BF16)

    tm = 1024
    per_b = seq // tm
    h = _norm_mod(x2, norm1_g.reshape(1, d), sc1, sh1, seq)
    small = _matmul("in_small", h, w_small, F32, tm=tm, tn=small_w)
    zs = _matmul("in_z", h, w_z, BF16, tm=tm, tn=1024, epilogue=lambda acc: _silu(acc))
    xc = _xbc_conv(h, w_xbc, conv_w, conv_b.reshape(1, d_xbc), seq, tm=tm, tn=1024)
    sg = _matmul("in_gates", h, w_g, BF16, tm=tm, tn=1024, epilogue=lambda acc: jax.nn.sigmoid(acc))

    w_iq_p = jnp.pad(w_iq.reshape(Q_RANK, N_HEADS_IDX, HEAD_DIM_IDX),
                     ((0, 0), (0, 0), (0, LANES - HEAD_DIM_IDX))).reshape(Q_RANK, N_HEADS_IDX * LANES).astype(BF16)
    w_uk_t = jnp.transpose(w_uk.reshape(KV_RANK, N_HEADS_A, HEAD_DIM_A), (1, 2, 0)).astype(BF16)
    w_uv_h = jnp.transpose(w_uv.reshape(KV_RANK, N_HEADS_A, HEAD_DIM_A), (1, 0, 2)).astype(BF16)
    ql, qi, wt, ckv, kidx = _prep(
        small, cq_norm_g.reshape(1, Q_RANK), ckv_norm_g.reshape(1, KV_RANK),
        _pad_cols(kidx_norm_g.reshape(1, HEAD_DIM_IDX), LANES), _pad_cols(kidx_norm_b.reshape(1, HEAD_DIM_IDX), LANES),
        w_uq.astype(BF16), w_iq_p, w_uk_t)
    bt = _bias_tiles(rel_bias)
    y_a = _attention(qi, wt, kidx, ckv, ql, bt, w_uv_h, batch, seq).reshape(t, ha)

    expand = np.zeros((LANES, d_inner), np.float32)
    for hh in range(n_heads_b):
        expand[hh, hh * SSM_HEAD_DIM:(hh + 1) * SSM_HEAD_DIM] = 1.0
    aneg = _pad_cols(-jnp.exp(a_log.astype(F32)).reshape(1, n_heads_b), LANES)
    y_b = _ssd(xc.reshape(batch, seq, d_xbc), zs.reshape(batch, seq, d_inner), small.reshape(batch, seq, small_w),
               _pad_cols(dt_bias.reshape(1, n_heads_b), LANES), aneg,
               jnp.repeat(d_skip, SSM_HEAD_DIM).reshape(1, d_inner), ssm_norm_g.reshape(1, d_inner),
               jnp.asarray(expand, BF16), dt_col // LANES).reshape(t, d_inner)

    tn = 1024
    tp = 512
    merged = _matmul2("proj_ab", y_a, w_proj_a.astype(BF16), y_b, w_proj_b.astype(BF16), BF16, tm=tm, tn=tp,
                      epilogue=lambda ra, rb, sa_ref, sb_ref: (sa_ref[...].astype(F32) * ra
                                                               + sb_ref[...].astype(F32) * rb),
                      extras=(sg, sg),
                      extra_specs=(pl.BlockSpec((tm, tp), lambda i, j: (i, j)),
                                   pl.BlockSpec((tm, tp), lambda i, j: (i, j + d // tp))))
    x1 = _matmul("w_out", merged, w_out.astype(BF16), F32, tm=tm, tn=tn,
                 epilogue=lambda acc, x_ref, g_ref: x_ref[...] + g_ref[0] * acc,
                 extras=(x2, g1), extra_specs=(_tile_spec(tm, tn), _batch_row_spec(tn, per_b)))

    h2 = _norm_mod(x1, norm2_g.reshape(1, d), sc2, sh2, seq)
    d_ff = w_gate.shape[1]
    tf = 512
    act = _matmul2("ffn_gate_up", h2, w_gate.astype(BF16), None, w_up.astype(BF16), BF16, tm=tm, tn=tf,
                   epilogue=lambda rg, ru: _silu(rg) * ru)

    def down_epilogue(acc, x_ref, g_ref, f_ref):
        xo = x_ref[...] + g_ref[0] * acc
        ms = jnp.mean(xo * xo, axis=-1, keepdims=True)
        return xo * lax.rsqrt(ms + EPS) * f_ref[...]

    tmd = 512
    assert d_ff % 4 == 0
    return _matmul("ffn_down", act, w_down.astype(BF16), F32, tm=tmd, tn=d, tk=d_ff // 4,
                   epilogue=down_epilogue, extras=(x1, g2, final_g.reshape(1, d)),
                   extra_specs=(_tile_spec(tmd, d), _batch_row_spec(d, seq // tmd),
                                pl.BlockSpec((1, d), lambda i, j, k: (0, 0))))


def kernel(x, c, w_ada, b_ada, norm1_g, w_in, cq_norm_g, ckv_norm_g, kidx_norm_g, kidx_norm_b, w_uq, w_iq, w_uk, w_uv, rel_bias, conv_w, conv_b, dt_bias, a_log, d_skip, ssm_norm_g, w_proj_a, w_proj_b, w_out, norm2_g, w_gate, w_up, w_down, final_g):
    batch, seq, d = x.shape
    assert w_ada.shape[0] == 1, "single-layer block"
    c8 = jnp.pad(c, ((0, SUBLANES - batch % SUBLANES if batch % SUBLANES else 0), (0, 0)))
    mod = _ada(c8, w_ada[0], b_ada[0].reshape(1, -1))[:batch]
    out = _layer(x.reshape(batch * seq, d), mod, batch, seq, norm1_g[0], w_in[0], cq_norm_g[0], ckv_norm_g[0],
                 kidx_norm_g[0], kidx_norm_b[0], w_uq[0], w_iq[0], w_uk[0], w_uv[0], rel_bias, conv_w[0],
                 conv_b[0], dt_bias[0], a_log[0], d_skip[0], ssm_norm_g[0], w_proj_a[0], w_proj_b[0], w_out[0],
                 norm2_g[0], w_gate[0], w_up[0], w_down[0], final_g)
    return out.reshape(batch, seq, d)
```

```python
import functools
import math

import numpy as np
import jax
import jax.numpy as jnp
from jax import lax
from jax.experimental import pallas as pl
from jax.experimental.pallas import tpu as pltpu

F32 = jnp.float32
BF16 = jnp.bfloat16
I32 = jnp.int32
I16 = jnp.int16

EPS = 1e-6
LANES = 128
SUBLANES = 8
VMEM_LIMIT = 56 * 2**20

N_HEADS_A = 16
HEAD_DIM_A = 128
Q_RANK = 512
KV_RANK = 256
N_HEADS_IDX = 16
HEAD_DIM_IDX = 64
TOP_K_MAX = 256
Q_BLOCK = 128
KEY_CHUNK = 512
HEAD_GROUP = 4
SM_ROWS = 64
LOG2E = 1.4426950408889634
N_BUCKETS = 32
MAX_DISTANCE = 128
SSM_HEAD_DIM = 64
SSM_GROUPS = 8
D_STATE = 128
CONV_WIDTH = 4
CONV_STRIP = 256
CONV_ROWS = 64
CHUNK = 128
NEG = -1e30
NEG_INF_KEY = -2139095041


def _cparams(*sem):
    return pltpu.CompilerParams(dimension_semantics=sem, vmem_limit_bytes=VMEM_LIMIT)


def _silu(x):
    return x * jax.nn.sigmoid(x)


def _split3(v):
    hi = v.astype(BF16)
    r1 = v - hi.astype(F32)
    mid = r1.astype(BF16)
    lo = (r1 - mid.astype(F32)).astype(BF16)
    return hi, mid, lo


def _dot(a, b):
    return jnp.dot(a, b, preferred_element_type=F32)


def _dot_nt(a, b):
    return lax.dot_general(a, b, (((1,), (1,)), ((), ())), preferred_element_type=F32)


def _ada_kernel(c_ref, w_ref, b_ref, o_ref):
    a = _silu(c_ref[...])
    o_ref[...] = _dot(a, w_ref[...]) + b_ref[...]


def _ada(c8, w, b):
    m, d = c8.shape
    n = w.shape[1]
    tn = 1024
    return pl.pallas_call(
        _ada_kernel,
        grid=(n // tn,),
        in_specs=[pl.BlockSpec((m, d), lambda j: (0, 0)),
                  pl.BlockSpec((d, tn), lambda j: (0, j)),
                  pl.BlockSpec((1, tn), lambda j: (0, j))],
        out_specs=pl.BlockSpec((m, tn), lambda j: (0, j)),
        out_shape=jax.ShapeDtypeStruct((m, n), F32),
        compiler_params=_cparams("parallel"),
        name="ada",
    )(c8, w, b)


def _norm_kernel(x_ref, g_ref, sc_ref, sh_ref, o_ref):
    x = x_ref[...]
    ms = jnp.mean(x * x, axis=-1, keepdims=True)
    y = x * lax.rsqrt(ms + EPS) * g_ref[...]
    o_ref[...] = (y * (1.0 + sc_ref[0]) + sh_ref[0]).astype(o_ref.dtype)


def _norm_mod(x2, g, sc, sh, seq):
    t, d = x2.shape
    tm = 512
    per_b = seq // tm
    return pl.pallas_call(
        _norm_kernel,
        grid=(t // tm,),
        in_specs=[pl.BlockSpec((tm, d), lambda i: (i, 0)),
                  pl.BlockSpec((1, d), lambda i: (0, 0)),
                  pl.BlockSpec((1, 1, d), lambda i: (i // per_b, 0, 0)),
                  pl.BlockSpec((1, 1, d), lambda i: (i // per_b, 0, 0))],
        out_specs=pl.BlockSpec((tm, d), lambda i: (i, 0)),
        out_shape=jax.ShapeDtypeStruct((t, d), BF16),
        compiler_params=_cparams("parallel"),
        name="norm_mod",
    )(x2, g, sc, sh)


def _mm_kernel(*refs, nk, n_extra, epilogue):
    a_ref, w_ref = refs[0], refs[1]
    extra = refs[2:2 + n_extra]
    o_ref = refs[2 + n_extra]
    if nk == 1:
        acc = _dot(a_ref[...], w_ref[...])
        o_ref[...] = epilogue(acc, *extra).astype(o_ref.dtype)
        return
    k = pl.program_id(2)

    @pl.when(k == 0)
    def _():
        o_ref[...] = _dot(a_ref[...], w_ref[...])

    @pl.when((k > 0) & (k < nk - 1))
    def _():
        o_ref[...] += _dot(a_ref[...], w_ref[...])

    @pl.when(k == nk - 1)
    def _():
        o_ref[...] = epilogue(o_ref[...] + _dot(a_ref[...], w_ref[...]), *extra)


def _matmul(name, a, w, out_dtype, *, tm, tn, tk=None, epilogue=None, extras=(), extra_specs=()):
    m, kdim = a.shape
    n = w.shape[1]
    tk = kdim if tk is None else tk
    nk = kdim // tk
    assert m % tm == 0 and n % tn == 0 and kdim % tk == 0
    if epilogue is None:
        epilogue = lambda acc: acc
    assert nk == 1 or (nk >= 2 and out_dtype == F32), "split K accumulates in the f32 output block"
    return pl.pallas_call(
        functools.partial(_mm_kernel, nk=nk, n_extra=len(extras), epilogue=epilogue),
        grid=(m // tm, n // tn, nk),
        in_specs=[pl.BlockSpec((tm, tk), lambda i, j, k: (i, k)),
                  pl.BlockSpec((tk, tn), lambda i, j, k: (k, j)),
                  *extra_specs],
        out_specs=pl.BlockSpec((tm, tn), lambda i, j, k: (i, j)),
        out_shape=jax.ShapeDtypeStruct((m, n), out_dtype),
        compiler_params=_cparams("parallel", "parallel", "arbitrary"),
        name=name,
    )(a, w, *extras)


def _tile_spec(tm, tn):
    return pl.BlockSpec((tm, tn), lambda i, j, k: (i, j))


def _batch_row_spec(tn, per_b):
    return pl.BlockSpec((1, 1, tn), lambda i, j, k: (i // per_b, 0, j))


def _mm2_kernel(*refs, shared_a, n_extra, epilogue):
    if shared_a:
        a1_ref, w1_ref, w2_ref = refs[:3]
        a2_ref, rest = a1_ref, refs[3:]
    else:
        a1_ref, w1_ref, a2_ref, w2_ref = refs[:4]
        rest = refs[4:]
    r1 = _dot(a1_ref[...], w1_ref[...])
    r2 = _dot(a2_ref[...], w2_ref[...])
    rest[n_extra][...] = epilogue(r1, r2, *rest[:n_extra]).astype(rest[n_extra].dtype)


def _matmul2(name, a1, w1, a2, w2, out_dtype, *, tm, tn, epilogue, extras=(), extra_specs=()):
    m, k1 = a1.shape
    n = w1.shape[1]
    assert m % tm == 0 and n % tn == 0 and w2.shape[1] == n
    shared = a2 is None
    ins = [a1, w1] + ([] if shared else [a2]) + [w2]
    specs = [pl.BlockSpec((tm, k1), lambda i, j: (i, 0)), pl.BlockSpec((k1, tn), lambda i, j: (0, j))]
    if not shared:
        specs.append(pl.BlockSpec((tm, a2.shape[1]), lambda i, j: (i, 0)))
    specs.append(pl.BlockSpec((w2.shape[0], tn), lambda i, j: (0, j)))
    return pl.pallas_call(
        functools.partial(_mm2_kernel, shared_a=shared, n_extra=len(extras), epilogue=epilogue),
        grid=(m // tm, n // tn),
        in_specs=specs + list(extra_specs),
        out_specs=pl.BlockSpec((tm, tn), lambda i, j: (i, j)),
        out_shape=jax.ShapeDtypeStruct((m, n), out_dtype),
        compiler_params=_cparams("parallel", "parallel"),
        name=name,
    )(*ins, *extras)


def _prep_kernel(small_ref, cqg_ref, ckvg_ref, kg_ref, kb_ref, wuq_ref, wiq_ref, wukt_ref,
                 ql_ref, qi_ref, wt_ref, ckv_ref, kidx_ref, *, tm):
    nqb = tm // Q_BLOCK
    cq = small_ref[:, 0:Q_RANK]
    ms = jnp.mean(cq * cq, axis=-1, keepdims=True)
    cqn = (cq * lax.rsqrt(ms + EPS) * cqg_ref[...]).astype(BF16)

    q = _dot(cqn, wuq_ref[...]).astype(BF16)
    scale = HEAD_DIM_A ** -0.5 * LOG2E
    for h in range(N_HEADS_A):
        qlh = (_dot(q[:, h * HEAD_DIM_A:(h + 1) * HEAD_DIM_A], wukt_ref[h]) * scale).astype(BF16)
        for b in range(nqb):
            ql_ref[b, h * Q_BLOCK:(h + 1) * Q_BLOCK, :] = qlh[b * Q_BLOCK:(b + 1) * Q_BLOCK, :]

    qix = _dot(cqn, wiq_ref[...]).astype(BF16)
    for h in range(N_HEADS_IDX):
        for b in range(nqb):
            qi_ref[b, h * Q_BLOCK:(h + 1) * Q_BLOCK, :] = (
                qix[b * Q_BLOCK:(b + 1) * Q_BLOCK, h * LANES:(h + 1) * LANES])

    ckv = small_ref[:, Q_RANK:Q_RANK + KV_RANK]
    ms = jnp.mean(ckv * ckv, axis=-1, keepdims=True)
    ckv_ref[...] = (ckv * lax.rsqrt(ms + EPS) * ckvg_ref[...]).astype(BF16)

    slab = small_ref[:, Q_RANK + KV_RANK:Q_RANK + KV_RANK + LANES]
    lane = lax.broadcasted_iota(I32, slab.shape, 1)
    kmask = lane < HEAD_DIM_IDX
    mu = jnp.sum(jnp.where(kmask, slab, 0.0), axis=-1, keepdims=True) * (1.0 / HEAD_DIM_IDX)
    xc = jnp.where(kmask, slab - mu, 0.0)
    var = jnp.sum(xc * xc, axis=-1, keepdims=True) * (1.0 / HEAD_DIM_IDX)
    y = xc * lax.rsqrt(var + EPS) * kg_ref[...] + kb_ref[...]
    kidx_ref[...] = jnp.where(kmask, y, 0.0).astype(BF16)

    wscale = N_HEADS_IDX ** -0.5 * HEAD_DIM_IDX ** -0.5
    slab_t = slab.T
    wt = slab_t[HEAD_DIM_IDX:HEAD_DIM_IDX + N_HEADS_IDX, :] * wscale
    for b in range(nqb):
        wt_ref[b] = wt[:, b * Q_BLOCK:(b + 1) * Q_BLOCK]


def _prep(small, cqg, ckvg, kg, kb, wuq, wiq, wukt):
    t = small.shape[0]
    tm = 512
    nqb = tm // Q_BLOCK
    nq = t // Q_BLOCK
    hq = N_HEADS_A * Q_BLOCK
    const2 = lambda i: (0, 0)
    return pl.pallas_call(
        functools.partial(_prep_kernel, tm=tm),
        grid=(t // tm,),
        in_specs=[pl.BlockSpec((tm, small.shape[1]), lambda i: (i, 0)),
                  pl.BlockSpec(cqg.shape, const2),
                  pl.BlockSpec(ckvg.shape, const2),
                  pl.BlockSpec(kg.shape, const2),
                  pl.BlockSpec(kb.shape, const2),
                  pl.BlockSpec(wuq.shape, const2),
                  pl.BlockSpec(wiq.shape, const2),
                  pl.BlockSpec(wukt.shape, lambda i: (0, 0, 0))],
        out_specs=[pl.BlockSpec((nqb, hq, KV_RANK), lambda i: (i, 0, 0)),
                   pl.BlockSpec((nqb, hq, LANES), lambda i: (i, 0, 0)),
                   pl.BlockSpec((nqb, N_HEADS_IDX, Q_BLOCK), lambda i: (i, 0, 0)),
                   pl.BlockSpec((tm, KV_RANK), lambda i: (i, 0)),
                   pl.BlockSpec((tm, LANES), lambda i: (i, 0))],
        out_shape=[jax.ShapeDtypeStruct((nq, hq, KV_RANK), BF16),
                   jax.ShapeDtypeStruct((nq, hq, LANES), BF16),
                   jax.ShapeDtypeStruct((nq, N_HEADS_IDX, Q_BLOCK), F32),
                   jax.ShapeDtypeStruct((t, KV_RANK), BF16),
                   jax.ShapeDtypeStruct((t, LANES), BF16)],
        compiler_params=_cparams("parallel"),
        name="dsa_prep",
    )(small, cqg, ckvg, kg, kb, wuq, wiq, wukt)


def _bucket_tiles():
    t = np.arange(Q_BLOCK)[:, None]
    s = np.arange(Q_BLOCK)[None, :]
    max_exact = N_BUCKETS // 2
    tiles = []
    for d in range(3):
        n = np.maximum(t - s + d * Q_BLOCK, 0)
        nf = np.maximum(n, 1).astype(np.float32)
        large = max_exact + (np.log(nf / max_exact) / math.log(MAX_DISTANCE / max_exact)
                             * (N_BUCKETS - max_exact)).astype(np.int32)
        large = np.minimum(large, N_BUCKETS - 1)
        tiles.append(np.where(n < max_exact, n, large))
    assert 2 * Q_BLOCK - (Q_BLOCK - 1) >= MAX_DISTANCE
    tiles[2] = np.full_like(tiles[2], N_BUCKETS - 1)
    return np.stack(tiles).astype(np.int32)


def _bias_kernel(rel_ref, bucket_ref, o_ref):
    h = pl.program_id(0)
    bucket = bucket_ref[...]
    out = jnp.zeros(bucket.shape, F32)
    for b in range(N_BUCKETS):
        out = jnp.where(bucket == b, rel_ref[b, h], out)
    o_ref[:, 0] = (out - rel_ref[N_BUCKETS - 1, h]) * LOG2E


def _bias_tiles(rel_bias):
    buckets = jnp.asarray(_bucket_tiles())
    return pl.pallas_call(
        _bias_kernel,
        grid=(N_HEADS_A,),
        in_specs=[pl.BlockSpec(memory_space=pltpu.SMEM),
                  pl.BlockSpec((3, Q_BLOCK, Q_BLOCK), lambda h: (0, 0, 0))],
        out_specs=pl.BlockSpec((3, 1, Q_BLOCK, Q_BLOCK), lambda h: (0, h, 0, 0)),
        out_shape=jax.ShapeDtypeStruct((3, N_HEADS_A, Q_BLOCK, Q_BLOCK), F32),
        compiler_params=_cparams("arbitrary"),
        name="rel_bias_tiles",
    )(rel_bias, buckets)


def _attn_kernel(qi_ref, wt_ref, kidx_ref, ckv_ref, ql_ref, bt_ref, wuv_ref, ya_ref,
                 key_s, hi16_s, lo16_s, c16_s, mb_s, st_s, x_buf, p_buf, al_buf, acc_s, m_s, lp_s, j_s,
                 *, seq, top_k):
    ck = KEY_CHUNK
    sub = ck // Q_BLOCK
    i = pl.program_id(1)
    nchunk = (i + sub) // sub
    t_row = i * Q_BLOCK + lax.broadcasted_iota(I32, (1, Q_BLOCK), 1)
    qi = qi_ref[0]

    def score_chunk(c, carry):
        k0 = pl.multiple_of(c * ck, ck)
        kc = kidx_ref[0, pl.ds(k0, ck), :]
        st_s[...] = _dot_nt(kc, qi)
        for u in range(sub):
            acc = jnp.zeros((Q_BLOCK, Q_BLOCK), F32)
            for h in range(N_HEADS_IDX):
                sh = st_s[u * Q_BLOCK:(u + 1) * Q_BLOCK, h * Q_BLOCK:(h + 1) * Q_BLOCK]
                acc = acc + jnp.maximum(sh, 0.0) * wt_ref[0, h:h + 1, :]
            s_pos = k0 + u * Q_BLOCK + lax.broadcasted_iota(I32, (Q_BLOCK, Q_BLOCK), 0)
            acc = jnp.where(s_pos <= t_row, acc, -jnp.inf)
            bits = lax.bitcast_convert_type(acc, I32)
            key = jnp.where(bits < 0, bits ^ 0x7FFFFFFF, bits)
            urows = slice(u * Q_BLOCK, (u + 1) * Q_BLOCK)
            key_s[c, urows, :] = key
            hi16_s[c, urows, :] = (key >> 16).astype(I16)
            lo16_s[c, urows, :] = ((key & 0xFFFF) - 32768).astype(I16)
        return carry

    lax.fori_loop(0, nchunk, score_chunk, 0)

    def count(pred):
        def body(c, cnt):
            tile = key_s[c]
            s_pos = c * ck + lax.broadcasted_iota(I32, (ck, Q_BLOCK), 0)
            hit = jnp.where(pred(tile, s_pos), 1, 0).astype(I32)
            return cnt + jnp.sum(hit.reshape(ck // SUBLANES, SUBLANES, Q_BLOCK), axis=0)
        cnt = lax.fori_loop(0, nchunk, body, jnp.zeros((SUBLANES, Q_BLOCK), I32))
        return jnp.sum(cnt, axis=0, keepdims=True)

    pack_rows = 2 * SUBLANES
    one16 = jnp.ones((pack_rows, Q_BLOCK), I16)
    zero16 = jnp.zeros((pack_rows, Q_BLOCK), I16)

    def pack16(v):
        return jnp.broadcast_to(v, (pack_rows, Q_BLOCK)).astype(I16)

    def bisect16(arr_s, n_lo0):
        def step(_, state):
            lo, hi, n_lo = state
            mid = (lo + hi) >> 1
            thr16 = pack16(mid)

            def body(c, cnt):
                tile = arr_s[c]
                hits = [jnp.where(tile[r * pack_rows:(r + 1) * pack_rows, :] >= thr16, one16, zero16)
                        for r in range(ck // pack_rows)]
                while len(hits) > 1:
                    hits = [hits[k] + hits[k + 1] for k in range(0, len(hits), 2)]
                return cnt + hits[0]

            cnt = lax.fori_loop(0, nchunk, body, zero16)
            n_mid = jnp.sum(cnt.astype(I32), axis=0, keepdims=True)
            ok = n_mid >= top_k
            return jnp.where(ok, mid, lo), jnp.where(ok, hi, mid), jnp.where(ok, n_mid, n_lo)

        lo, _, n_lo = lax.fori_loop(
            0, 16, step, (jnp.full((1, Q_BLOCK), -2**15, I32), jnp.full((1, Q_BLOCK), 2**15, I32), n_lo0))
        return lo, n_lo

    prefix, n_prefix = bisect16(hi16_s, jnp.broadcast_to(nchunk * ck, (1, Q_BLOCK)).astype(I32))
    p16 = pack16(prefix)[None]

    def low_chunk(c, carry):
        hi = hi16_s[c].reshape(ck // pack_rows, pack_rows, Q_BLOCK)
        lo = lo16_s[c].reshape(ck // pack_rows, pack_rows, Q_BLOCK)
        c16 = jnp.where(hi > p16, jnp.full((), 2**15 - 1, I16), jnp.where(hi == p16, lo, jnp.full((), -2**15, I16)))
        c16_s[c] = c16.reshape(ck, Q_BLOCK)
        return carry

    lax.fori_loop(0, nchunk, low_chunk, 0)
    low, n_ge = bisect16(c16_s, n_prefix)
    thr = prefix * 65536 + (low + 32768)
    tie = (n_ge > top_k) & (thr > NEG_INF_KEY)
    j_s[...] = jnp.full(j_s.shape, seq, I32)

    @pl.when(jnp.max(jnp.where(tie, 1, 0)) > 0)
    def _():
        need = top_k - count(lambda tile, s_pos: tile > thr)

        def jbisect(_, lohi):
            lo, hi = lohi
            mid = (lo + hi) >> 1
            ok = count(lambda tile, s_pos: (tile == thr) & (s_pos <= mid)) >= need
            return jnp.where(ok, lo, mid), jnp.where(ok, mid, hi)
        lo_j = jnp.full((1, Q_BLOCK), -1, I32)
        hi_j = jnp.full((1, Q_BLOCK), seq - 1, I32)
        _, cut = lax.fori_loop(0, max(1, (seq - 1).bit_length()) + 1, jbisect, (lo_j, hi_j))
        j_s[...] = jnp.broadcast_to(jnp.where(tie, cut, seq), j_s.shape)

    jcut = j_s[0:1, :]

    def mask_chunk(c, carry):
        tile = key_s[c]
        s_pos = c * ck + lax.broadcasted_iota(I32, (ck, Q_BLOCK), 0)
        sel = (tile > thr) | ((tile == thr) & (s_pos <= jcut))
        sel = sel & (s_pos <= t_row)
        mbt = jnp.where(sel, 0.0, NEG)
        for u in range(sub):
            mb_s[c * sub + u] = mbt[u * Q_BLOCK:(u + 1) * Q_BLOCK, :].T
        return carry

    lax.fori_loop(0, nchunk, mask_chunk, 0)
    neg_tile = seq // Q_BLOCK
    mb_s[neg_tile] = jnp.full((Q_BLOCK, Q_BLOCK), NEG, F32)

    m_s[...] = jnp.full(m_s.shape, -jnp.inf, F32)
    lp_s[...] = jnp.zeros(lp_s.shape, F32)
    acc_s[...] = jnp.zeros(acc_s.shape, F32)

    ngroup = N_HEADS_A // HEAD_GROUP
    grows_n = HEAD_GROUP * Q_BLOCK
    nparts = grows_n // SM_ROWS
    first_start = i + 1 - sub * nchunk

    def chunk_start(c):
        return jnp.maximum(first_start + c * sub, 0)

    def load_kv(c):
        k0 = pl.multiple_of(chunk_start(c) * Q_BLOCK, Q_BLOCK)
        return ckv_ref[0, pl.ds(k0, ck), :]

    def part_rows(g, part):
        h = g * HEAD_GROUP + part * SM_ROWS // Q_BLOCK
        q0 = (part * SM_ROWS) % Q_BLOCK
        rows = slice(h * Q_BLOCK + q0, h * Q_BLOCK + q0 + SM_ROWS)
        return h, q0, rows, slice(part * SM_ROWS, (part + 1) * SM_ROWS)

    def stage_a(c, g, near):
        start = chunk_start(c)
        shared = (c == 0) & (nchunk > 1)
        tiles = []
        for u in range(sub):
            blk = start + u
            tiles.append(jnp.where(shared & (blk >= first_start + sub), neg_tile, blk))
        s = _dot_nt(ql_ref[0, g * grows_n:(g + 1) * grows_n, :], load_kv(c))
        for part in range(nparts):
            h, q0, rows, lrows = part_rows(g, part)
            mx = None
            for u in range(sub):
                cols = slice(u * Q_BLOCK, (u + 1) * Q_BLOCK)
                x = s[lrows, cols] + mb_s[tiles[u], q0:q0 + SM_ROWS, :]
                if near:
                    x = x + bt_ref[jnp.clip(i - (start + u), 0, 2), h, q0:q0 + SM_ROWS, :]
                x_buf[g, lrows, cols] = x
                mx = x if mx is None else jnp.maximum(mx, x)
            m_prev = m_s[rows, :]
            m_cur = jnp.broadcast_to(jnp.max(mx, axis=1, keepdims=True), (SM_ROWS, LANES))
            m_new = jnp.maximum(m_prev, m_cur)
            al_buf[g, lrows, :] = jnp.exp2(m_prev - m_new)
            m_s[rows, :] = m_new

    def stage_b(g):
        for part in range(nparts):
            _, _, rows, lrows = part_rows(g, part)
            m_new = m_s[rows, :]
            ps = [jnp.exp2(x_buf[g, lrows, u * Q_BLOCK:(u + 1) * Q_BLOCK] - m_new) for u in range(sub)]
            lp_s[rows, :] = al_buf[g, lrows, :] * lp_s[rows, :] + ((ps[0] + ps[1]) + (ps[2] + ps[3]))
            p_buf[g, lrows, :] = jnp.concatenate(ps, axis=1).astype(BF16)

    def stage_c(c, g):
        grows = slice(g * grows_n, (g + 1) * grows_n)
        alpha = jnp.concatenate([al_buf[g]] * (KV_RANK // LANES), axis=1)
        acc_s[grows, :] = alpha * acc_s[grows, :] + _dot(p_buf[g], load_kv(c))

    assert sub == 4 and ngroup >= 2

    def chunk_steps(c, near, first):
        for k in range(ngroup):
            stage_a(c, k, near)
            if k >= 1 or not first:
                stage_b((k - 1) % ngroup)
            if k >= 2:
                stage_c(c, k - 2)
            elif not first:
                stage_c(c - 1, k - 2 + ngroup)

    def drain(c):
        stage_b(ngroup - 1)
        stage_c(c, ngroup - 2)
        stage_c(c, ngroup - 1)

    last = nchunk - 1

    @pl.when(nchunk == 1)
    def _():
        chunk_steps(last, True, True)
        drain(last)

    @pl.when(nchunk > 1)
    def _():
        chunk_steps(0, False, True)

        def far_body(c, carry):
            chunk_steps(c, False, False)
            return carry

        lax.fori_loop(1, last, far_body, 0)
        chunk_steps(last, True, False)
        drain(last)

    for h in range(N_HEADS_A):
        rows = slice(h * Q_BLOCK, (h + 1) * Q_BLOCK)
        l = jnp.sum(lp_s[rows, :], axis=1, keepdims=True)
        o = (acc_s[rows, :] / l).astype(BF16)
        ya_ref[0, :, h * HEAD_DIM_A:(h + 1) * HEAD_DIM_A] = _dot(o, wuv_ref[h]).astype(ya_ref.dtype)


def _attention(qi, wt, kidx, ckv, ql, bt, wuv, batch, seq):
    nq = seq // Q_BLOCK
    hq = N_HEADS_A * Q_BLOCK
    ck = KEY_CHUNK
    grp = HEAD_GROUP * Q_BLOCK
    ngrp = N_HEADS_A // HEAD_GROUP
    assert seq % ck == 0
    top_k = min(TOP_K_MAX, seq // 4)
    kidx3 = kidx.reshape(batch, seq, LANES)
    ckv3 = ckv.reshape(batch, seq, KV_RANK)
    return pl.pallas_call(
        functools.partial(_attn_kernel, seq=seq, top_k=top_k),
        grid=(batch, nq),
        in_specs=[pl.BlockSpec((1, hq, LANES), lambda b, i: (b * nq + i, 0, 0)),
                  pl.BlockSpec((1, N_HEADS_IDX, Q_BLOCK), lambda b, i: (b * nq + i, 0, 0)),
                  pl.BlockSpec((1, seq, LANES), lambda b, i: (b, 0, 0)),
                  pl.BlockSpec((1, seq, KV_RANK), lambda b, i: (b, 0, 0)),
                  pl.BlockSpec((1, hq, KV_RANK), lambda b, i: (b * nq + i, 0, 0)),
                  pl.BlockSpec(bt.shape, lambda b, i: (0, 0, 0, 0)),
                  pl.BlockSpec(wuv.shape, lambda b, i: (0, 0, 0))],
        out_specs=pl.BlockSpec((1, Q_BLOCK, N_HEADS_A * HEAD_DIM_A), lambda b, i: (b, i, 0)),
        out_shape=jax.ShapeDtypeStruct((batch, seq, N_HEADS_A * HEAD_DIM_A), BF16),
        scratch_shapes=[pltpu.VMEM((seq // ck, ck, Q_BLOCK), I32),
                        pltpu.VMEM((seq // ck, ck, Q_BLOCK), I16),
                        pltpu.VMEM((seq // ck, ck, Q_BLOCK), I16),
                        pltpu.VMEM((seq // ck, ck, Q_BLOCK), I16),
                        pltpu.VMEM((nq + 1, Q_BLOCK, Q_BLOCK), F32),
                        pltpu.VMEM((ck, hq), F32),
                        pltpu.VMEM((ngrp, grp, ck), F32),
                        pltpu.VMEM((ngrp, grp, ck), BF16),
                        pltpu.VMEM((ngrp, grp, LANES), F32),
                        pltpu.VMEM((hq, KV_RANK), F32),
                        pltpu.VMEM((hq, LANES), F32),
                        pltpu.VMEM((hq, LANES), F32),
                        pltpu.VMEM((SUBLANES, Q_BLOCK), I32)],
        compiler_params=_cparams("parallel", "arbitrary"),
        name="dsa_attention",
    )(qi, wt, kidx3, ckv3, ql, bt, wuv)


def _xbc_conv_kernel(a_ref, w_ref, cw_ref, cb_ref, o_ref, tail_s, ya_s, yb_s, *, tm, per_b):
    i, j = pl.program_id(0), pl.program_id(1)

    @pl.when(i % per_b == 0)
    def _():
        tail_s[j] = jnp.zeros(tail_s.shape[1:], F32)

    nstrip = o_ref.shape[1] // CONV_STRIP
    y_bufs = (ya_s, yb_s)
    row0 = pl.multiple_of(jnp.minimum(i, 0) * SUBLANES, SUBLANES)

    def matmul_strip(s):
        cols = slice(s * CONV_STRIP, (s + 1) * CONV_STRIP)
        buf = y_bufs[s % 2]
        buf[0:SUBLANES, :] = tail_s[j, :, cols]
        buf[SUBLANES:SUBLANES + tm, :] = _dot(a_ref[...], w_ref[:, cols])

    def conv_strip(s):
        cols = slice(s * CONV_STRIP, (s + 1) * CONV_STRIP)
        buf = y_bufs[s % 2]
        tail_s[j, :, cols] = buf[pl.ds(row0 + tm, SUBLANES), :]
        for r in range(0, tm, CONV_ROWS):
            ext = buf[pl.ds(row0 + r, CONV_ROWS + SUBLANES), :]
            acc = ext * cw_ref[0:1, cols]
            for k in range(1, CONV_WIDTH):
                acc = pltpu.roll(acc, 1, axis=0) + ext * cw_ref[k:k + 1, cols]
            o_ref[r:r + CONV_ROWS, cols] = _silu(acc[SUBLANES:, :] + cb_ref[:, cols]).astype(o_ref.dtype)

    for s in range(nstrip):
        matmul_strip(s)
        if s >= 1:
            conv_strip(s - 1)
    conv_strip(nstrip - 1)


def _xbc_conv(a, w, conv_w, conv_b, seq, *, tm, tn):
    m, kdim = a.shape
    n = w.shape[1]
    assert m % tm == 0 and n % tn == 0 and seq % tm == 0
    return pl.pallas_call(
        functools.partial(_xbc_conv_kernel, tm=tm, per_b=seq // tm),
        grid=(m // tm, n // tn),
        in_specs=[pl.BlockSpec((tm, kdim), lambda i, j: (i, 0)),
                  pl.BlockSpec((kdim, tn), lambda i, j: (0, j)),
                  pl.BlockSpec((CONV_WIDTH, tn), lambda i, j: (0, j)),
                  pl.BlockSpec((1, tn), lambda i, j: (0, j))],
        out_specs=pl.BlockSpec((tm, tn), lambda i, j: (i, j)),
        out_shape=jax.ShapeDtypeStruct((m, n), BF16),
        scratch_shapes=[pltpu.VMEM((n // tn, SUBLANES, tn), F32),
                        pltpu.VMEM((SUBLANES + tm, CONV_STRIP), F32),
                        pltpu.VMEM((SUBLANES + tm, CONV_STRIP), F32)],
        compiler_params=_cparams("arbitrary", "arbitrary"),
        name="in_xbc_conv",
    )(a, w, conv_w, conv_b)


def _ssd_kernel(xc_ref, zs_ref, dt_ref, dtb_ref, aneg_ref, dskip_ref, ng_ref, e_ref, y_ref,
                state_s, de_s, *, d_inner):
    c = pl.program_id(1)
    hg = d_inner // SSM_GROUPS
    heads_g = hg // SSM_HEAD_DIM
    b_off = d_inner
    c_off = d_inner + SSM_GROUPS * D_STATE

    @pl.when(c == 0)
    def _():
        state_s[...] = jnp.zeros_like(state_s)

    x = dt_ref[0] + dtb_ref[...]
    dt = jnp.maximum(x, 0.0) + jnp.log1p(jnp.exp(-jnp.abs(x)))
    da = dt * aneg_ref[...]
    tri_i = lax.broadcasted_iota(I32, (CHUNK, CHUNK), 0) >= lax.broadcasted_iota(I32, (CHUNK, CHUNK), 1)
    tri = jnp.where(tri_i, 1.0, 0.0).astype(BF16)
    acs = sum(_dot(tri, part) for part in _split3(da))
    acs_t = acs.T
    dt_t = dt.T
    acs_last = acs[CHUNK - 1:CHUNK, :]
    e = e_ref[...]
    de_s[...] = sum(_dot(part, e) for part in _split3(jnp.exp(acs_last - acs) * dt))
    el = jnp.broadcast_to(jnp.exp(acs_last), (SUBLANES, LANES))
    el_x = sum(_dot(part, e) for part in _split3(el))[0:1, :]
    lane = lax.broadcasted_iota(I32, (CHUNK, LANES), 1)

    for g in range(SSM_GROUPS):
        cols = slice(g * hg, (g + 1) * hg)
        xg = xc_ref[0, :, cols]
        bg = xc_ref[0, :, b_off + g * D_STATE:b_off + (g + 1) * D_STATE]
        cg = xc_ref[0, :, c_off + g * D_STATE:c_off + (g + 1) * D_STATE]
        cb = _dot_nt(cg, bg)
        cg_f = cg.astype(F32)
        sg = state_s[g]
        sg_b = sg.astype(BF16)
        tiles = []
        for pr in range(heads_g // 2):
            pcols = slice(pr * LANES, (pr + 1) * LANES)
            rhs = jnp.concatenate([xg[:, pcols], sg_b[:, pcols]], axis=0)
            outs = []
            for q in range(2):
                h = g * heads_g + pr * 2 + q
                col = jnp.broadcast_to(acs[:, h:h + 1], (CHUNK, CHUNK))
                decay = jnp.exp(jnp.where(tri_i, col - acs_t[h:h + 1, :], NEG))
                w = cb * decay * dt_t[h:h + 1, :]
                lhs = jnp.concatenate([w.astype(BF16), (cg_f * jnp.exp(col)).astype(BF16)], axis=1)
                outs.append(_dot(lhs, rhs))
            tiles.append(jnp.where(lane < SSM_HEAD_DIM, outs[0], outs[1]))
        xg_f = xg.astype(F32)
        y = jnp.concatenate(tiles, axis=1) + dskip_ref[:, cols] * xg_f
        xd = (xg_f * de_s[:, cols]).astype(BF16)
        state_s[g] = sg * el_x[:, cols] + _dot(bg.astype(F32).T.astype(BF16), xd)
        yz = y * zs_ref[0, :, cols].astype(F32)
        ms = jnp.mean(yz * yz, axis=-1, keepdims=True)
        y_ref[0, :, cols] = (yz * lax.rsqrt(ms + EPS) * ng_ref[:, cols]).astype(y_ref.dtype)


def _ssd(xc, zs, small3, dtb, aneg, dskip_x, ng, expand, dt_block):
    batch, seq, d_xbc = xc.shape
    d_inner = zs.shape[2]
    nc = seq // CHUNK
    const2 = lambda b, c: (0, 0)
    return pl.pallas_call(
        functools.partial(_ssd_kernel, d_inner=d_inner),
        grid=(batch, nc),
        in_specs=[pl.BlockSpec((1, CHUNK, d_xbc), lambda b, c: (b, c, 0)),
                  pl.BlockSpec((1, CHUNK, d_inner), lambda b, c: (b, c, 0)),
                  pl.BlockSpec((1, CHUNK, LANES), lambda b, c: (b, c, dt_block)),
                  pl.BlockSpec(dtb.shape, const2),
                  pl.BlockSpec(aneg.shape, const2),
                  pl.BlockSpec(dskip_x.shape, const2),
                  pl.BlockSpec(ng.shape, const2),
                  pl.BlockSpec(expand.shape, const2)],
        out_specs=pl.BlockSpec((1, CHUNK, d_inner), lambda b, c: (b, c, 0)),
        out_shape=jax.ShapeDtypeStruct((batch, seq, d_inner), BF16),
        scratch_shapes=[pltpu.VMEM((SSM_GROUPS, D_STATE, d_inner // SSM_GROUPS), F32),
                        pltpu.VMEM((CHUNK, d_inner), F32)],
        compiler_params=_cparams("parallel", "arbitrary"),
        name="ssd",
    )(xc, zs, small3, dtb, aneg, dskip_x, ng, expand)


def _pad_cols(a, n):
    return jnp.pad(a, ((0, 0), (0, n - a.shape[1])))


def _layer(x2, mod, batch, seq, norm1_g, w_in, cq_norm_g, ckv_norm_g, kidx_norm_g, kidx_norm_b, w_uq, w_iq,
           w_uk, w_uv, rel_bias, conv_w, conv_b, dt_bias, a_log, d_skip, ssm_norm_g, w_proj_a, w_proj_b,
           w_out, norm2_g, w_gate, w_up, w_down, final_g):
    t, d = x2.shape
    n_heads_b = dt_bias.shape[0]
    d_inner = n_heads_b * SSM_HEAD_DIM
    d_xbc = d_inner + 2 * SSM_GROUPS * D_STATE
    ha = N_HEADS_A * HEAD_DIM_A
    sh1, sc1, g1, sh2, sc2, g2 = [m.reshape(batch, 1, d) for m in jnp.split(mod, 6, axis=-1)]

    o_z = Q_RANK + KV_RANK + HEAD_DIM_IDX + N_HEADS_IDX
    o_xbc = o_z + d_inner
    o_dt = o_xbc + d_xbc
    o_g = o_dt + n_heads_b
    small_w = 1024
    dt_col = 896
    assert o_z <= dt_col and dt_col + n_heads_b <= small_w and dt_col % LANES == 0
    w_small = jnp.concatenate(
        [_pad_cols(w_in[:, :o_z], dt_col), _pad_cols(w_in[:, o_dt:o_g], small_w - dt_col)], axis=1).astype(BF16)
    w_z = w_in[:, o_z:o_xbc].astype(BF16)
    w_xbc = w_in[:, o_xbc:o_dt].astype(BF16)
    w_g = w_in[:, o_g:].astype(BF16)

    tm = 1024
    per_b = seq // tm
    h = _norm_mod(x2, norm1_g.reshape(1, d), sc1, sh1, seq)
    small = _matmul("in_small", h, w_small, F32, tm=tm, tn=small_w)
    zs = _matmul("in_z", h, w_z, BF16, tm=tm, tn=1024, epilogue=lambda acc: _silu(acc))
    xc = _xbc_conv(h, w_xbc, conv_w, conv_b.reshape(1, d_xbc), seq, tm=tm, tn=1024)
    sg = _matmul("in_gates", h, w_g, BF16, tm=tm, tn=1024, epilogue=lambda acc: jax.nn.sigmoid(acc))

    w_iq_p = jnp.pad(w_iq.reshape(Q_RANK, N_HEADS_IDX, HEAD_DIM_IDX),
                     ((0, 0), (0, 0), (0, LANES - HEAD_DIM_IDX))).reshape(Q_RANK, N_HEADS_IDX * LANES).astype(BF16)
    w_uk_t = jnp.transpose(w_uk.reshape(KV_RANK, N_HEADS_A, HEAD_DIM_A), (1, 2, 0)).astype(BF16)
    w_uv_h = jnp.transpose(w_uv.reshape(KV_RANK, N_HEADS_A, HEAD_DIM_A), (1, 0, 2)).astype(BF16)
    ql, qi, wt, ckv, kidx = _prep(
        small, cq_norm_g.reshape(1, Q_RANK), ckv_norm_g.reshape(1, KV_RANK),
        _pad_cols(kidx_norm_g.reshape(1, HEAD_DIM_IDX), LANES), _pad_cols(kidx_norm_b.reshape(1, HEAD_DIM_IDX), LANES),
        w_uq.astype(BF16), w_iq_p, w_uk_t)
    bt = _bias_tiles(rel_bias)
    y_a = _attention(qi, wt, kidx, ckv, ql, bt, w_uv_h, batch, seq).reshape(t, ha)

    expand = np.zeros((LANES, d_inner), np.float32)
    for hh in range(n_heads_b):
        expand[hh, hh * SSM_HEAD_DIM:(hh + 1) * SSM_HEAD_DIM] = 1.0
    aneg = _pad_cols(-jnp.exp(a_log.astype(F32)).reshape(1, n_heads_b), LANES)
    y_b = _ssd(xc.reshape(batch, seq, d_xbc), zs.reshape(batch, seq, d_inner), small.reshape(batch, seq, small_w),
               _pad_cols(dt_bias.reshape(1, n_heads_b), LANES), aneg,
               jnp.repeat(d_skip, SSM_HEAD_DIM).reshape(1, d_inner), ssm_norm_g.reshape(1, d_inner),
               jnp.asarray(expand, BF16), dt_col // LANES).reshape(t, d_inner)

    tn = 1024
    tp = 512
    merged = _matmul2("proj_ab", y_a, w_proj_a.astype(BF16), y_b, w_proj_b.astype(BF16), BF16, tm=tm, tn=tp,
                      epilogue=lambda ra, rb, sa_ref, sb_ref: (sa_ref[...].astype(F32) * ra
                                                               + sb_ref[...].astype(F32) * rb),
                      extras=(sg, sg),
                      extra_specs=(pl.BlockSpec((tm, tp), lambda i, j: (i, j)),
                                   pl.BlockSpec((tm, tp), lambda i, j: (i, j + d // tp))))
    x1 = _matmul("w_out", merged, w_out.astype(BF16), F32, tm=tm, tn=tn,
                 epilogue=lambda acc, x_ref, g_ref: x_ref[...] + g_ref[0] * acc,
                 extras=(x2, g1), extra_specs=(_tile_spec(tm, tn), _batch_row_spec(tn, per_b)))

    h2 = _norm_mod(x1, norm2_g.reshape(1, d), sc2, sh2, seq)
    d_ff = w_gate.shape[1]
    tf = 512
    act = _matmul2("ffn_gate_up", h2, w_gate.astype(BF16), None, w_up.astype(BF16), BF16, tm=tm, tn=tf,
                   epilogue=lambda rg, ru: _silu(rg) * ru)

    def down_epilogue(acc, x_ref, g_ref, f_ref):
        xo = x_ref[...] + g_ref[0] * acc
        ms = jnp.mean(xo * xo, axis=-1, keepdims=True)
        return xo * lax.rsqrt(ms + EPS) * f_ref[...]

    tmd = 1024
    assert d_ff % 4 == 0
    return _matmul("ffn_down", act, w_down.astype(BF16), F32, tm=tmd, tn=d, tk=d_ff // 4,
                   epilogue=down_epilogue, extras=(x1, g2, final_g.reshape(1, d)),
                   extra_specs=(_tile_spec(tmd, d), _batch_row_spec(d, seq // tmd),
                                pl.BlockSpec((1, d), lambda i, j, k: (0, 0))))


def kernel(x, c, w_ada, b_ada, norm1_g, w_in, cq_norm_g, ckv_norm_g, kidx_norm_g, kidx_norm_b, w_uq, w_iq, w_uk, w_uv, rel_bias, conv_w, conv_b, dt_bias, a_log, d_skip, ssm_norm_g, w_proj_a, w_proj_b, w_out, norm2_g, w_gate, w_up, w_down, final_g):
    batch, seq, d = x.shape
    assert w_ada.shape[0] == 1, "single-layer block"
    c8 = jnp.pad(c, ((0, SUBLANES - batch % SUBLANES if batch % SUBLANES else 0), (0, 0)))
    mod = _ada(c8, w_ada[0], b_ada[0].reshape(1, -1))[:batch]
    out = _layer(x.reshape(batch * seq, d), mod, batch, seq, norm1_g[0], w_in[0], cq_norm_g[0], ckv_norm_g[0],
                 kidx_norm_g[0], kidx_norm_b[0], w_uq[0], w_iq[0], w_uk[0], w_uv[0], rel_bias, conv_w[0],
                 conv_b[0], dt_bias[0], a_log[0], d_skip[0], ssm_norm_g[0], w_proj_a[0], w_proj_b[0], w_out[0],
                 norm2_g[0], w_gate[0], w_up[0], w_down[0], final_g)
    return out.reshape(batch, seq, d)
```

```python
import functools
import math

import numpy as np
import jax
import jax.numpy as jnp
from jax import lax
from jax.experimental import pallas as pl
from jax.experimental.pallas import tpu as pltpu

F32 = jnp.float32
BF16 = jnp.bfloat16
I32 = jnp.int32
I16 = jnp.int16

EPS = 1e-6
LANES = 128
SUBLANES = 8
VMEM_LIMIT = 56 * 2**20

N_HEADS_A = 16
HEAD_DIM_A = 128
Q_RANK = 512
KV_RANK = 256
N_HEADS_IDX = 16
HEAD_DIM_IDX = 64
TOP_K_MAX = 256
Q_BLOCK = 128
KEY_CHUNK = 512
HEAD_GROUP = 4
SM_ROWS = 64
LOG2E = 1.4426950408889634
N_BUCKETS = 32
MAX_DISTANCE = 128
SSM_HEAD_DIM = 64
SSM_GROUPS = 8
D_STATE = 128
NORM_ROWS = 32
CONV_WIDTH = 4
CONV_STRIP = 256
CONV_ROWS = 64
CHUNK = 128
NEG = -1e30
NEG_INF_KEY = -2139095041


def _cparams(*sem):
    return pltpu.CompilerParams(dimension_semantics=sem, vmem_limit_bytes=VMEM_LIMIT)


def _silu(x):
    return x * jax.nn.sigmoid(x)


def _split3(v):
    hi = v.astype(BF16)
    r1 = v - hi.astype(F32)
    mid = r1.astype(BF16)
    lo = (r1 - mid.astype(F32)).astype(BF16)
    return hi, mid, lo


def _dot(a, b):
    return jnp.dot(a, b, preferred_element_type=F32)


def _dot_nt(a, b):
    return lax.dot_general(a, b, (((1,), (1,)), ((), ())), preferred_element_type=F32)


def _ada_kernel(c_ref, w_ref, b_ref, o_ref):
    a = _silu(c_ref[...])
    o_ref[...] = _dot(a, w_ref[...]) + b_ref[...]


def _ada(c8, w, b):
    m, d = c8.shape
    n = w.shape[1]
    tn = 1024
    return pl.pallas_call(
        _ada_kernel,
        grid=(n // tn,),
        in_specs=[pl.BlockSpec((m, d), lambda j: (0, 0)),
                  pl.BlockSpec((d, tn), lambda j: (0, j)),
                  pl.BlockSpec((1, tn), lambda j: (0, j))],
        out_specs=pl.BlockSpec((m, tn), lambda j: (0, j)),
        out_shape=jax.ShapeDtypeStruct((m, n), F32),
        compiler_params=_cparams("parallel"),
        name="ada",
    )(c8, w, b)


def _norm_kernel(x_ref, g_ref, sc_ref, sh_ref, o_ref):
    x = x_ref[...]
    ms = jnp.mean(x * x, axis=-1, keepdims=True)
    y = x * lax.rsqrt(ms + EPS) * g_ref[...]
    o_ref[...] = (y * (1.0 + sc_ref[0]) + sh_ref[0]).astype(o_ref.dtype)


def _norm_mod(x2, g, sc, sh, seq):
    t, d = x2.shape
    tm = 512
    per_b = seq // tm
    return pl.pallas_call(
        _norm_kernel,
        grid=(t // tm,),
        in_specs=[pl.BlockSpec((tm, d), lambda i: (i, 0)),
                  pl.BlockSpec((1, d), lambda i: (0, 0)),
                  pl.BlockSpec((1, 1, d), lambda i: (i // per_b, 0, 0)),
                  pl.BlockSpec((1, 1, d), lambda i: (i // per_b, 0, 0))],
        out_specs=pl.BlockSpec((tm, d), lambda i: (i, 0)),
        out_shape=jax.ShapeDtypeStruct((t, d), BF16),
        compiler_params=_cparams("parallel"),
        name="norm_mod",
    )(x2, g, sc, sh)


def _mm_kernel(*refs, nk, n_extra, epilogue):
    a_ref, w_ref = refs[0], refs[1]
    extra = refs[2:2 + n_extra]
    o_ref = refs[2 + n_extra]
    if nk == 1:
        acc = _dot(a_ref[...], w_ref[...])
        o_ref[...] = epilogue(acc, *extra).astype(o_ref.dtype)
        return
    k = pl.program_id(2)

    @pl.when(k == 0)
    def _():
        o_ref[...] = _dot(a_ref[...], w_ref[...])

    @pl.when((k > 0) & (k < nk - 1))
    def _():
        o_ref[...] += _dot(a_ref[...], w_ref[...])

    @pl.when(k == nk - 1)
    def _():
        o_ref[...] = epilogue(o_ref[...] + _dot(a_ref[...], w_ref[...]), *extra)


def _matmul(name, a, w, out_dtype, *, tm, tn, tk=None, epilogue=None, extras=(), extra_specs=()):
    m, kdim = a.shape
    n = w.shape[1]
    tk = kdim if tk is None else tk
    nk = kdim // tk
    assert m % tm == 0 and n % tn == 0 and kdim % tk == 0
    if epilogue is None:
        epilogue = lambda acc: acc
    assert nk == 1 or (nk >= 2 and out_dtype == F32), "split K accumulates in the f32 output block"
    return pl.pallas_call(
        functools.partial(_mm_kernel, nk=nk, n_extra=len(extras), epilogue=epilogue),
        grid=(m // tm, n // tn, nk),
        in_specs=[pl.BlockSpec((tm, tk), lambda i, j, k: (i, k)),
                  pl.BlockSpec((tk, tn), lambda i, j, k: (k, j)),
                  *extra_specs],
        out_specs=pl.BlockSpec((tm, tn), lambda i, j, k: (i, j)),
        out_shape=jax.ShapeDtypeStruct((m, n), out_dtype),
        compiler_params=_cparams("parallel", "parallel", "arbitrary"),
        name=name,
    )(a, w, *extras)


def _tile_spec(tm, tn):
    return pl.BlockSpec((tm, tn), lambda i, j, k: (i, j))


def _batch_row_spec(tn, per_b):
    return pl.BlockSpec((1, 1, tn), lambda i, j, k: (i // per_b, 0, j))


def _mm2_kernel(a1_ref, w1_ref, a2_ref, w2_ref, *rest, epilogue):
    r1 = _dot(a1_ref[...], w1_ref[...])
    r2 = _dot(a2_ref[...], w2_ref[...])
    *extra, o_ref = rest
    o_ref[...] = epilogue(r1, r2, *extra).astype(o_ref.dtype)


def _matmul2(name, a1, w1, a2, w2, out_dtype, *, tm, tn, epilogue, extras=(), extra_specs=()):
    m, k1 = a1.shape
    k2 = a2.shape[1]
    n = w1.shape[1]
    assert m % tm == 0 and n % tn == 0 and w2.shape[1] == n
    return pl.pallas_call(
        functools.partial(_mm2_kernel, epilogue=epilogue),
        grid=(m // tm, n // tn),
        in_specs=[pl.BlockSpec((tm, k1), lambda i, j: (i, 0)),
                  pl.BlockSpec((k1, tn), lambda i, j: (0, j)),
                  pl.BlockSpec((tm, k2), lambda i, j: (i, 0)),
                  pl.BlockSpec((k2, tn), lambda i, j: (0, j)),
                  *extra_specs],
        out_specs=pl.BlockSpec((tm, tn), lambda i, j: (i, j)),
        out_shape=jax.ShapeDtypeStruct((m, n), out_dtype),
        compiler_params=_cparams("parallel", "parallel"),
        name=name,
    )(a1, w1, a2, w2, *extras)


def _ffn_in_kernel(x_ref, g_ref, sc_ref, sh_ref, wg_ref, wu_ref, o_ref, h_s):
    @pl.when(pl.program_id(1) == 0)
    def _():
        gain = g_ref[...] * (1.0 + sc_ref[0])

        def rows(r, carry):
            sl = pl.ds(pl.multiple_of(r * NORM_ROWS, NORM_ROWS), NORM_ROWS)
            x = x_ref[sl, :]
            ms = jnp.mean(x * x, axis=-1, keepdims=True)
            h_s[sl, :] = (x * lax.rsqrt(ms + EPS) * gain + sh_ref[0]).astype(h_s.dtype)
            return carry

        lax.fori_loop(0, x_ref.shape[0] // NORM_ROWS, rows, 0, unroll=4)

    h = h_s[...]
    rg = _dot(h, wg_ref[...].astype(BF16))
    ru = _dot(h, wu_ref[...].astype(BF16))
    o_ref[...] = (_silu(rg) * ru).astype(o_ref.dtype)


def _ffn_in(x, g, sc, sh, w_gate, w_up, seq, *, tm, tn):
    t, d = x.shape
    f = w_gate.shape[1]
    assert t % tm == 0 and f % tn == 0 and seq % tm == 0
    per_b = seq // tm
    return pl.pallas_call(
        _ffn_in_kernel,
        grid=(t // tm, f // tn),
        in_specs=[pl.BlockSpec((tm, d), lambda i, j: (i, 0)),
                  pl.BlockSpec((1, d), lambda i, j: (0, 0)),
                  pl.BlockSpec((1, 1, d), lambda i, j: (i // per_b, 0, 0)),
                  pl.BlockSpec((1, 1, d), lambda i, j: (i // per_b, 0, 0)),
                  pl.BlockSpec((d, tn), lambda i, j: (0, j)),
                  pl.BlockSpec((d, tn), lambda i, j: (0, j))],
        out_specs=pl.BlockSpec((tm, tn), lambda i, j: (i, j)),
        out_shape=jax.ShapeDtypeStruct((t, f), BF16),
        scratch_shapes=[pltpu.VMEM((tm, d), BF16)],
        compiler_params=_cparams("parallel", "arbitrary"),
        name="ffn_norm_gate_up",
    )(x, g, sc, sh, w_gate, w_up)


def _prep_kernel(small_ref, cqg_ref, ckvg_ref, kg_ref, kb_ref, wuq_ref, wiq_ref, wukt_ref,
                 ql_ref, qi_ref, wt_ref, ckv_ref, kidx_ref, *, tm):
    nqb = tm // Q_BLOCK
    cq = small_ref[:, 0:Q_RANK]
    ms = jnp.mean(cq * cq, axis=-1, keepdims=True)
    cqn = (cq * lax.rsqrt(ms + EPS) * cqg_ref[...]).astype(BF16)

    q = _dot(cqn, wuq_ref[...]).astype(BF16)
    scale = HEAD_DIM_A ** -0.5 * LOG2E
    for h in range(N_HEADS_A):
        qlh = (_dot(q[:, h * HEAD_DIM_A:(h + 1) * HEAD_DIM_A], wukt_ref[h]) * scale).astype(BF16)
        for b in range(nqb):
            ql_ref[b, h * Q_BLOCK:(h + 1) * Q_BLOCK, :] = qlh[b * Q_BLOCK:(b + 1) * Q_BLOCK, :]

    qix = _dot(cqn, wiq_ref[...]).astype(BF16)
    for h in range(N_HEADS_IDX):
        for b in range(nqb):
            qi_ref[b, h * Q_BLOCK:(h + 1) * Q_BLOCK, :] = (
                qix[b * Q_BLOCK:(b + 1) * Q_BLOCK, h * LANES:(h + 1) * LANES])

    ckv = small_ref[:, Q_RANK:Q_RANK + KV_RANK]
    ms = jnp.mean(ckv * ckv, axis=-1, keepdims=True)
    ckv_ref[...] = (ckv * lax.rsqrt(ms + EPS) * ckvg_ref[...]).astype(BF16)

    slab = small_ref[:, Q_RANK + KV_RANK:Q_RANK + KV_RANK + LANES]
    lane = lax.broadcasted_iota(I32, slab.shape, 1)
    kmask = lane < HEAD_DIM_IDX
    mu = jnp.sum(jnp.where(kmask, slab, 0.0), axis=-1, keepdims=True) * (1.0 / HEAD_DIM_IDX)
    xc = jnp.where(kmask, slab - mu, 0.0)
    var = jnp.sum(xc * xc, axis=-1, keepdims=True) * (1.0 / HEAD_DIM_IDX)
    y = xc * lax.rsqrt(var + EPS) * kg_ref[...] + kb_ref[...]
    kidx_ref[...] = jnp.where(kmask, y, 0.0).astype(BF16)

    wscale = N_HEADS_IDX ** -0.5 * HEAD_DIM_IDX ** -0.5
    slab_t = slab.T
    wt = slab_t[HEAD_DIM_IDX:HEAD_DIM_IDX + N_HEADS_IDX, :] * wscale
    for b in range(nqb):
        wt_ref[b] = wt[:, b * Q_BLOCK:(b + 1) * Q_BLOCK]


def _prep(small, cqg, ckvg, kg, kb, wuq, wiq, wukt):
    t = small.shape[0]
    tm = 512
    nqb = tm // Q_BLOCK
    nq = t // Q_BLOCK
    hq = N_HEADS_A * Q_BLOCK
    const2 = lambda i: (0, 0)
    return pl.pallas_call(
        functools.partial(_prep_kernel, tm=tm),
        grid=(t // tm,),
        in_specs=[pl.BlockSpec((tm, small.shape[1]), lambda i: (i, 0)),
                  pl.BlockSpec(cqg.shape, const2),
                  pl.BlockSpec(ckvg.shape, const2),
                  pl.BlockSpec(kg.shape, const2),
                  pl.BlockSpec(kb.shape, const2),
                  pl.BlockSpec(wuq.shape, const2),
                  pl.BlockSpec(wiq.shape, const2),
                  pl.BlockSpec(wukt.shape, lambda i: (0, 0, 0))],
        out_specs=[pl.BlockSpec((nqb, hq, KV_RANK), lambda i: (i, 0, 0)),
                   pl.BlockSpec((nqb, hq, LANES), lambda i: (i, 0, 0)),
                   pl.BlockSpec((nqb, N_HEADS_IDX, Q_BLOCK), lambda i: (i, 0, 0)),
                   pl.BlockSpec((tm, KV_RANK), lambda i: (i, 0)),
                   pl.BlockSpec((tm, LANES), lambda i: (i, 0))],
        out_shape=[jax.ShapeDtypeStruct((nq, hq, KV_RANK), BF16),
                   jax.ShapeDtypeStruct((nq, hq, LANES), BF16),
                   jax.ShapeDtypeStruct((nq, N_HEADS_IDX, Q_BLOCK), F32),
                   jax.ShapeDtypeStruct((t, KV_RANK), BF16),
                   jax.ShapeDtypeStruct((t, LANES), BF16)],
        compiler_params=_cparams("parallel"),
        name="dsa_prep",
    )(small, cqg, ckvg, kg, kb, wuq, wiq, wukt)


def _bucket_tiles():
    t = np.arange(Q_BLOCK)[:, None]
    s = np.arange(Q_BLOCK)[None, :]
    max_exact = N_BUCKETS // 2
    tiles = []
    for d in range(3):
        n = np.maximum(t - s + d * Q_BLOCK, 0)
        nf = np.maximum(n, 1).astype(np.float32)
        large = max_exact + (np.log(nf / max_exact) / math.log(MAX_DISTANCE / max_exact)
                             * (N_BUCKETS - max_exact)).astype(np.int32)
        large = np.minimum(large, N_BUCKETS - 1)
        tiles.append(np.where(n < max_exact, n, large))
    assert 2 * Q_BLOCK - (Q_BLOCK - 1) >= MAX_DISTANCE
    tiles[2] = np.full_like(tiles[2], N_BUCKETS - 1)
    return np.stack(tiles).astype(np.int32)


def _bias_kernel(rel_ref, bucket_ref, o_ref):
    h = pl.program_id(0)
    bucket = bucket_ref[...]
    out = jnp.zeros(bucket.shape, F32)
    for b in range(N_BUCKETS):
        out = jnp.where(bucket == b, rel_ref[b, h], out)
    o_ref[:, 0] = (out - rel_ref[N_BUCKETS - 1, h]) * LOG2E


def _bias_tiles(rel_bias):
    buckets = jnp.asarray(_bucket_tiles())
    return pl.pallas_call(
        _bias_kernel,
        grid=(N_HEADS_A,),
        in_specs=[pl.BlockSpec(memory_space=pltpu.SMEM),
                  pl.BlockSpec((3, Q_BLOCK, Q_BLOCK), lambda h: (0, 0, 0))],
        out_specs=pl.BlockSpec((3, 1, Q_BLOCK, Q_BLOCK), lambda h: (0, h, 0, 0)),
        out_shape=jax.ShapeDtypeStruct((3, N_HEADS_A, Q_BLOCK, Q_BLOCK), F32),
        compiler_params=_cparams("arbitrary"),
        name="rel_bias_tiles",
    )(rel_bias, buckets)


def _attn_kernel(qi_ref, wt_ref, kidx_ref, ckv_ref, ql_ref, bt_ref, wuv_ref, ya_ref,
                 key_s, hi16_s, lo16_s, c16_s, mb_s, st_s, x_buf, p_buf, al_buf, acc_s, m_s, lp_s, j_s,
                 *, seq, top_k):
    ck = KEY_CHUNK
    sub = ck // Q_BLOCK
    i = pl.program_id(1)
    nchunk = (i + sub) // sub
    t_row = i * Q_BLOCK + lax.broadcasted_iota(I32, (1, Q_BLOCK), 1)
    qi = qi_ref[0]

    def score_chunk(c, carry):
        k0 = pl.multiple_of(c * ck, ck)
        kc = kidx_ref[0, pl.ds(k0, ck), :]
        st_s[...] = _dot_nt(kc, qi)
        for u in range(sub):
            acc = jnp.zeros((Q_BLOCK, Q_BLOCK), F32)
            for h in range(N_HEADS_IDX):
                sh = st_s[u * Q_BLOCK:(u + 1) * Q_BLOCK, h * Q_BLOCK:(h + 1) * Q_BLOCK]
                acc = acc + jnp.maximum(sh, 0.0) * wt_ref[0, h:h + 1, :]
            s_pos = k0 + u * Q_BLOCK + lax.broadcasted_iota(I32, (Q_BLOCK, Q_BLOCK), 0)
            acc = jnp.where(s_pos <= t_row, acc, -jnp.inf)
            bits = lax.bitcast_convert_type(acc, I32)
            key = jnp.where(bits < 0, bits ^ 0x7FFFFFFF, bits)
            urows = slice(u * Q_BLOCK, (u + 1) * Q_BLOCK)
            key_s[c, urows, :] = key
            hi16_s[c, urows, :] = (key >> 16).astype(I16)
            lo16_s[c, urows, :] = ((key & 0xFFFF) - 32768).astype(I16)
        return carry

    lax.fori_loop(0, nchunk, score_chunk, 0)

    def count(pred):
        def body(c, cnt):
            tile = key_s[c]
            s_pos = c * ck + lax.broadcasted_iota(I32, (ck, Q_BLOCK), 0)
            hit = jnp.where(pred(tile, s_pos), 1, 0).astype(I32)
            return cnt + jnp.sum(hit.reshape(ck // SUBLANES, SUBLANES, Q_BLOCK), axis=0)
        cnt = lax.fori_loop(0, nchunk, body, jnp.zeros((SUBLANES, Q_BLOCK), I32))
        return jnp.sum(cnt, axis=0, keepdims=True)

    pack_rows = 2 * SUBLANES
    one16 = jnp.ones((pack_rows, Q_BLOCK), I16)
    zero16 = jnp.zeros((pack_rows, Q_BLOCK), I16)

    def pack16(v):
        return jnp.broadcast_to(v, (pack_rows, Q_BLOCK)).astype(I16)

    def bisect16_static(arr_s, n, n_lo0):
        def step(_, state):
            lo, hi, n_lo = state
            mid = (lo + hi) >> 1
            thr16 = pack16(mid)
            hits = [jnp.where(arr_s[c, r * pack_rows:(r + 1) * pack_rows, :] >= thr16, one16, zero16)
                    for c in range(n) for r in range(ck // pack_rows)]
            while len(hits) > 1:
                hits = [hits[k] + hits[k + 1] for k in range(0, len(hits) - 1, 2)] + hits[len(hits) & ~1:]
            n_mid = jnp.sum(hits[0].astype(I32), axis=0, keepdims=True)
            ok = n_mid >= top_k
            return jnp.where(ok, mid, lo), jnp.where(ok, hi, mid), jnp.where(ok, n_mid, n_lo)

        lo, _, n_lo = lax.fori_loop(
            0, 16, step, (jnp.full((1, Q_BLOCK), -2**15, I32), jnp.full((1, Q_BLOCK), 2**15, I32), n_lo0))
        return lo, n_lo

    def bisect16(arr_s, n_lo0):
        branches = [functools.partial(bisect16_static, arr_s, n) for n in range(1, seq // ck + 1)]
        return lax.switch(nchunk - 1, branches, n_lo0)

    prefix, n_prefix = bisect16(hi16_s, jnp.broadcast_to(nchunk * ck, (1, Q_BLOCK)).astype(I32))
    p16 = pack16(prefix)[None]

    def low_chunk(c, carry):
        hi = hi16_s[c].reshape(ck // pack_rows, pack_rows, Q_BLOCK)
        lo = lo16_s[c].reshape(ck // pack_rows, pack_rows, Q_BLOCK)
        c16 = jnp.where(hi > p16, jnp.full((), 2**15 - 1, I16), jnp.where(hi == p16, lo, jnp.full((), -2**15, I16)))
        c16_s[c] = c16.reshape(ck, Q_BLOCK)
        return carry

    lax.fori_loop(0, nchunk, low_chunk, 0)
    low, n_ge = bisect16(c16_s, n_prefix)
    thr = prefix * 65536 + (low + 32768)
    tie = (n_ge > top_k) & (thr > NEG_INF_KEY)
    j_s[...] = jnp.full(j_s.shape, seq, I32)

    @pl.when(jnp.max(jnp.where(tie, 1, 0)) > 0)
    def _():
        need = top_k - count(lambda tile, s_pos: tile > thr)

        def jbisect(_, lohi):
            lo, hi = lohi
            mid = (lo + hi) >> 1
            ok = count(lambda tile, s_pos: (tile == thr) & (s_pos <= mid)) >= need
            return jnp.where(ok, lo, mid), jnp.where(ok, mid, hi)
        lo_j = jnp.full((1, Q_BLOCK), -1, I32)
        hi_j = jnp.full((1, Q_BLOCK), seq - 1, I32)
        _, cut = lax.fori_loop(0, max(1, (seq - 1).bit_length()) + 1, jbisect, (lo_j, hi_j))
        j_s[...] = jnp.broadcast_to(jnp.where(tie, cut, seq), j_s.shape)

    jcut = j_s[0:1, :]

    def mask_chunk(c, carry):
        tile = key_s[c]
        s_pos = c * ck + lax.broadcasted_iota(I32, (ck, Q_BLOCK), 0)
        sel = (tile > thr) | ((tile == thr) & (s_pos <= jcut))
        sel = sel & (s_pos <= t_row)
        mbt = jnp.where(sel, 0.0, NEG)
        for u in range(sub):
            mb_s[c * sub + u] = mbt[u * Q_BLOCK:(u + 1) * Q_BLOCK, :].T
        return carry

    lax.fori_loop(0, nchunk, mask_chunk, 0)
    neg_tile = seq // Q_BLOCK
    mb_s[neg_tile] = jnp.full((Q_BLOCK, Q_BLOCK), NEG, F32)

    m_s[...] = jnp.full(m_s.shape, -jnp.inf, F32)
    lp_s[...] = jnp.zeros(lp_s.shape, F32)
    acc_s[...] = jnp.zeros(acc_s.shape, F32)

    ngroup = N_HEADS_A // HEAD_GROUP
    grows_n = HEAD_GROUP * Q_BLOCK
    nparts = grows_n // SM_ROWS
    first_start = i + 1 - sub * nchunk

    def chunk_start(c):
        return jnp.maximum(first_start + c * sub, 0)

    def load_kv(c):
        k0 = pl.multiple_of(chunk_start(c) * Q_BLOCK, Q_BLOCK)
        return ckv_ref[0, pl.ds(k0, ck), :]

    def part_rows(g, part):
        h = g * HEAD_GROUP + part * SM_ROWS // Q_BLOCK
        q0 = (part * SM_ROWS) % Q_BLOCK
        rows = slice(h * Q_BLOCK + q0, h * Q_BLOCK + q0 + SM_ROWS)
        return h, q0, rows, slice(part * SM_ROWS, (part + 1) * SM_ROWS)

    def stage_a(c, g, near):
        start = chunk_start(c)
        shared = (c == 0) & (nchunk > 1)
        tiles = []
        for u in range(sub):
            blk = start + u
            tiles.append(jnp.where(shared & (blk >= first_start + sub), neg_tile, blk))
        s = _dot_nt(ql_ref[0, g * grows_n:(g + 1) * grows_n, :], load_kv(c))
        for part in range(nparts):
            h, q0, rows, lrows = part_rows(g, part)
            mx = None
            for u in range(sub):
                cols = slice(u * Q_BLOCK, (u + 1) * Q_BLOCK)
                x = s[lrows, cols] + mb_s[tiles[u], q0:q0 + SM_ROWS, :]
                if near:
                    x = x + bt_ref[jnp.clip(i - (start + u), 0, 2), h, q0:q0 + SM_ROWS, :]
                x_buf[g, lrows, cols] = x
                mx = x if mx is None else jnp.maximum(mx, x)
            m_prev = m_s[rows, :]
            m_cur = jnp.broadcast_to(jnp.max(mx, axis=1, keepdims=True), (SM_ROWS, LANES))
            m_new = jnp.maximum(m_prev, m_cur)
            al_buf[g, lrows, :] = jnp.exp2(m_prev - m_new)
            m_s[rows, :] = m_new

    def stage_b(g):
        for part in range(nparts):
            _, _, rows, lrows = part_rows(g, part)
            m_new = m_s[rows, :]
            ps = [jnp.exp2(x_buf[g, lrows, u * Q_BLOCK:(u + 1) * Q_BLOCK] - m_new) for u in range(sub)]
            lp_s[rows, :] = al_buf[g, lrows, :] * lp_s[rows, :] + ((ps[0] + ps[1]) + (ps[2] + ps[3]))
            p_buf[g, lrows, :] = jnp.concatenate(ps, axis=1).astype(BF16)

    def stage_c(c, g):
        grows = slice(g * grows_n, (g + 1) * grows_n)
        alpha = jnp.concatenate([al_buf[g]] * (KV_RANK // LANES), axis=1)
        acc_s[grows, :] = alpha * acc_s[grows, :] + _dot(p_buf[g], load_kv(c))

    assert sub == 4 and ngroup >= 2

    def chunk_steps(c, near, first):
        for k in range(ngroup):
            stage_a(c, k, near)
            if k >= 1 or not first:
                stage_b((k - 1) % ngroup)
            if k >= 2:
                stage_c(c, k - 2)
            elif not first:
                stage_c(c - 1, k - 2 + ngroup)

    def drain(c):
        stage_b(ngroup - 1)
        stage_c(c, ngroup - 2)
        stage_c(c, ngroup - 1)

    last = nchunk - 1

    @pl.when(nchunk == 1)
    def _():
        chunk_steps(last, True, True)
        drain(last)

    @pl.when(nchunk > 1)
    def _():
        chunk_steps(0, False, True)

        def far_body(c, carry):
            chunk_steps(c, False, False)
            return carry

        lax.fori_loop(1, last, far_body, 0)
        chunk_steps(last, True, False)
        drain(last)

    for h in range(N_HEADS_A):
        rows = slice(h * Q_BLOCK, (h + 1) * Q_BLOCK)
        l = jnp.sum(lp_s[rows, :], axis=1, keepdims=True)
        o = (acc_s[rows, :] / l).astype(BF16)
        ya_ref[0, :, h * HEAD_DIM_A:(h + 1) * HEAD_DIM_A] = _dot(o, wuv_ref[h]).astype(ya_ref.dtype)


def _attention(qi, wt, kidx, ckv, ql, bt, wuv, batch, seq):
    nq = seq // Q_BLOCK
    hq = N_HEADS_A * Q_BLOCK
    ck = KEY_CHUNK
    grp = HEAD_GROUP * Q_BLOCK
    ngrp = N_HEADS_A // HEAD_GROUP
    assert seq % ck == 0
    top_k = min(TOP_K_MAX, seq // 4)
    kidx3 = kidx.reshape(batch, seq, LANES)
    ckv3 = ckv.reshape(batch, seq, KV_RANK)
    return pl.pallas_call(
        functools.partial(_attn_kernel, seq=seq, top_k=top_k),
        grid=(batch, nq),
        in_specs=[pl.BlockSpec((1, hq, LANES), lambda b, i: (b * nq + i, 0, 0)),
                  pl.BlockSpec((1, N_HEADS_IDX, Q_BLOCK), lambda b, i: (b * nq + i, 0, 0)),
                  pl.BlockSpec((1, seq, LANES), lambda b, i: (b, 0, 0)),
                  pl.BlockSpec((1, seq, KV_RANK), lambda b, i: (b, 0, 0)),
                  pl.BlockSpec((1, hq, KV_RANK), lambda b, i: (b * nq + i, 0, 0)),
                  pl.BlockSpec(bt.shape, lambda b, i: (0, 0, 0, 0)),
                  pl.BlockSpec(wuv.shape, lambda b, i: (0, 0, 0))],
        out_specs=pl.BlockSpec((1, Q_BLOCK, N_HEADS_A * HEAD_DIM_A), lambda b, i: (b, i, 0)),
        out_shape=jax.ShapeDtypeStruct((batch, seq, N_HEADS_A * HEAD_DIM_A), BF16),
        scratch_shapes=[pltpu.VMEM((seq // ck, ck, Q_BLOCK), I32),
                        pltpu.VMEM((seq // ck, ck, Q_BLOCK), I16),
                        pltpu.VMEM((seq // ck, ck, Q_BLOCK), I16),
                        pltpu.VMEM((seq // ck, ck, Q_BLOCK), I16),
                        pltpu.VMEM((nq + 1, Q_BLOCK, Q_BLOCK), F32),
                        pltpu.VMEM((ck, hq), F32),
                        pltpu.VMEM((ngrp, grp, ck), F32),
                        pltpu.VMEM((ngrp, grp, ck), BF16),
                        pltpu.VMEM((ngrp, grp, LANES), F32),
                        pltpu.VMEM((hq, KV_RANK), F32),
                        pltpu.VMEM((hq, LANES), F32),
                        pltpu.VMEM((hq, LANES), F32),
                        pltpu.VMEM((SUBLANES, Q_BLOCK), I32)],
        compiler_params=_cparams("parallel", "arbitrary"),
        name="dsa_attention",
    )(qi, wt, kidx3, ckv3, ql, bt, wuv)


def _xbc_conv_kernel(a_ref, w_ref, cw_ref, cb_ref, o_ref, tail_s, ya_s, yb_s, *, tm, per_b):
    i, j = pl.program_id(0), pl.program_id(1)

    @pl.when(i % per_b == 0)
    def _():
        tail_s[j] = jnp.zeros(tail_s.shape[1:], F32)

    nstrip = o_ref.shape[1] // CONV_STRIP
    y_bufs = (ya_s, yb_s)
    row0 = pl.multiple_of(jnp.minimum(i, 0) * SUBLANES, SUBLANES)

    def matmul_strip(s):
        cols = slice(s * CONV_STRIP, (s + 1) * CONV_STRIP)
        buf = y_bufs[s % 2]
        buf[0:SUBLANES, :] = tail_s[j, :, cols]
        buf[SUBLANES:SUBLANES + tm, :] = _dot(a_ref[...], w_ref[:, cols])

    def conv_strip(s):
        cols = slice(s * CONV_STRIP, (s + 1) * CONV_STRIP)
        buf = y_bufs[s % 2]
        tail_s[j, :, cols] = buf[pl.ds(row0 + tm, SUBLANES), :]
        for r in range(0, tm, CONV_ROWS):
            ext = buf[pl.ds(row0 + r, CONV_ROWS + SUBLANES), :]
            acc = ext * cw_ref[0:1, cols]
            for k in range(1, CONV_WIDTH):
                acc = pltpu.roll(acc, 1, axis=0) + ext * cw_ref[k:k + 1, cols]
            o_ref[r:r + CONV_ROWS, cols] = _silu(acc[SUBLANES:, :] + cb_ref[:, cols]).astype(o_ref.dtype)

    for s in range(nstrip):
        matmul_strip(s)
        if s >= 1:
            conv_strip(s - 1)
    conv_strip(nstrip - 1)


def _xbc_conv(a, w, conv_w, conv_b, seq, *, tm, tn):
    m, kdim = a.shape
    n = w.shape[1]
    assert m % tm == 0 and n % tn == 0 and seq % tm == 0
    return pl.pallas_call(
        functools.partial(_xbc_conv_kernel, tm=tm, per_b=seq // tm),
        grid=(m // tm, n // tn),
        in_specs=[pl.BlockSpec((tm, kdim), lambda i, j: (i, 0)),
                  pl.BlockSpec((kdim, tn), lambda i, j: (0, j)),
                  pl.BlockSpec((CONV_WIDTH, tn), lambda i, j: (0, j)),
                  pl.BlockSpec((1, tn), lambda i, j: (0, j))],
        out_specs=pl.BlockSpec((tm, tn), lambda i, j: (i, j)),
        out_shape=jax.ShapeDtypeStruct((m, n), BF16),
        scratch_shapes=[pltpu.VMEM((n // tn, SUBLANES, tn), F32),
                        pltpu.VMEM((SUBLANES + tm, CONV_STRIP), F32),
                        pltpu.VMEM((SUBLANES + tm, CONV_STRIP), F32)],
        compiler_params=_cparams("arbitrary", "arbitrary"),
        name="in_xbc_conv",
    )(a, w, conv_w, conv_b)


def _ssd_kernel(xc_ref, zs_ref, dt_ref, dtb_ref, aneg_ref, dskip_ref, ng_ref, e_ref, y_ref,
                state_s, de_s, *, d_inner):
    c = pl.program_id(1)
    hg = d_inner // SSM_GROUPS
    heads_g = hg // SSM_HEAD_DIM
    b_off = d_inner
    c_off = d_inner + SSM_GROUPS * D_STATE

    @pl.when(c == 0)
    def _():
        state_s[...] = jnp.zeros_like(state_s)

    x = dt_ref[0] + dtb_ref[...]
    dt = jnp.maximum(x, 0.0) + jnp.log1p(jnp.exp(-jnp.abs(x)))
    da = dt * aneg_ref[...]
    tri_i = lax.broadcasted_iota(I32, (CHUNK, CHUNK), 0) >= lax.broadcasted_iota(I32, (CHUNK, CHUNK), 1)
    tri = jnp.where(tri_i, 1.0, 0.0).astype(BF16)
    acs = sum(_dot(tri, part) for part in _split3(da))
    acs_t = acs.T
    dt_t = dt.T
    acs_last = acs[CHUNK - 1:CHUNK, :]
    e = e_ref[...]
    de_s[...] = sum(_dot(part, e) for part in _split3(jnp.exp(acs_last - acs) * dt))
    el = jnp.broadcast_to(jnp.exp(acs_last), (SUBLANES, LANES))
    el_x = sum(_dot(part, e) for part in _split3(el))[0:1, :]
    lane = lax.broadcasted_iota(I32, (CHUNK, LANES), 1)

    for g in range(SSM_GROUPS):
        cols = slice(g * hg, (g + 1) * hg)
        xg = xc_ref[0, :, cols]
        bg = xc_ref[0, :, b_off + g * D_STATE:b_off + (g + 1) * D_STATE]
        cg = xc_ref[0, :, c_off + g * D_STATE:c_off + (g + 1) * D_STATE]
        cb = _dot_nt(cg, bg)
        cg_f = cg.astype(F32)
        sg = state_s[g]
        sg_b = sg.astype(BF16)
        tiles = []
        for pr in range(heads_g // 2):
            pcols = slice(pr * LANES, (pr + 1) * LANES)
            rhs = jnp.concatenate([xg[:, pcols], sg_b[:, pcols]], axis=0)
            outs = []
            for q in range(2):
                h = g * heads_g + pr * 2 + q
                col = jnp.broadcast_to(acs[:, h:h + 1], (CHUNK, CHUNK))
                decay = jnp.exp(jnp.where(tri_i, col - acs_t[h:h + 1, :], NEG))
                w = cb * decay * dt_t[h:h + 1, :]
                lhs = jnp.concatenate([w.astype(BF16), (cg_f * jnp.exp(col)).astype(BF16)], axis=1)
                outs.append(_dot(lhs, rhs))
            tiles.append(jnp.where(lane < SSM_HEAD_DIM, outs[0], outs[1]))
        xg_f = xg.astype(F32)
        y = jnp.concatenate(tiles, axis=1) + dskip_ref[:, cols] * xg_f
        xd = (xg_f * de_s[:, cols]).astype(BF16)
        state_s[g] = sg * el_x[:, cols] + _dot(bg.astype(F32).T.astype(BF16), xd)
        yz = y * zs_ref[0, :, cols].astype(F32)
        ms = jnp.mean(yz * yz, axis=-1, keepdims=True)
        y_ref[0, :, cols] = (yz * lax.rsqrt(ms + EPS) * ng_ref[:, cols]).astype(y_ref.dtype)


def _ssd(xc, zs, small3, dtb, aneg, dskip_x, ng, expand, dt_block):
    batch, seq, d_xbc = xc.shape
    d_inner = zs.shape[2]
    nc = seq // CHUNK
    const2 = lambda b, c: (0, 0)
    return pl.pallas_call(
        functools.partial(_ssd_kernel, d_inner=d_inner),
        grid=(batch, nc),
        in_specs=[pl.BlockSpec((1, CHUNK, d_xbc), lambda b, c: (b, c, 0)),
                  pl.BlockSpec((1, CHUNK, d_inner), lambda b, c: (b, c, 0)),
                  pl.BlockSpec((1, CHUNK, LANES), lambda b, c: (b, c, dt_block)),
                  pl.BlockSpec(dtb.shape, const2),
                  pl.BlockSpec(aneg.shape, const2),
                  pl.BlockSpec(dskip_x.shape, const2),
                  pl.BlockSpec(ng.shape, const2),
                  pl.BlockSpec(expand.shape, const2)],
        out_specs=pl.BlockSpec((1, CHUNK, d_inner), lambda b, c: (b, c, 0)),
        out_shape=jax.ShapeDtypeStruct((batch, seq, d_inner), BF16),
        scratch_shapes=[pltpu.VMEM((SSM_GROUPS, D_STATE, d_inner // SSM_GROUPS), F32),
                        pltpu.VMEM((CHUNK, d_inner), F32)],
        compiler_params=_cparams("parallel", "arbitrary"),
        name="ssd",
    )(xc, zs, small3, dtb, aneg, dskip_x, ng, expand)


def _pad_cols(a, n):
    return jnp.pad(a, ((0, 0), (0, n - a.shape[1])))


def _layer(x2, mod, batch, seq, norm1_g, w_in, cq_norm_g, ckv_norm_g, kidx_norm_g, kidx_norm_b, w_uq, w_iq,
           w_uk, w_uv, rel_bias, conv_w, conv_b, dt_bias, a_log, d_skip, ssm_norm_g, w_proj_a, w_proj_b,
           w_out, norm2_g, w_gate, w_up, w_down, final_g):
    t, d = x2.shape
    n_heads_b = dt_bias.shape[0]
    d_inner = n_heads_b * SSM_HEAD_DIM
    d_xbc = d_inner + 2 * SSM_GROUPS * D_STATE
    ha = N_HEADS_A * HEAD_DIM_A
    sh1, sc1, g1, sh2, sc2, g2 = [m.reshape(batch, 1, d) for m in jnp.split(mod, 6, axis=-1)]

    o_z = Q_RANK + KV_RANK + HEAD_DIM_IDX + N_HEADS_IDX
    o_xbc = o_z + d_inner
    o_dt = o_xbc + d_xbc
    o_g = o_dt + n_heads_b
    small_w = 1024
    dt_col = 896
    assert o_z <= dt_col and dt_col + n_heads_b <= small_w and dt_col % LANES == 0
    w_small = jnp.concatenate(
        [_pad_cols(w_in[:, :o_z], dt_col), _pad_cols(w_in[:, o_dt:o_g], small_w - dt_col)], axis=1).astype(BF16)
    w_z = w_in[:, o_z:o_xbc].astype(BF16)
    w_xbc = w_in[:, o_xbc:o_dt].astype(BF16)
    w_g = w_in[:, o_g:].astype(BF16)

    tm = 1024
    per_b = seq // tm
    h = _norm_mod(x2, norm1_g.reshape(1, d), sc1, sh1, seq)
    small = _matmul("in_small", h, w_small, F32, tm=tm, tn=small_w)
    zs = _matmul("in_z", h, w_z, BF16, tm=tm, tn=1024, epilogue=lambda acc: _silu(acc))
    xc = _xbc_conv(h, w_xbc, conv_w, conv_b.reshape(1, d_xbc), seq, tm=tm, tn=1024)
    sg = _matmul("in_gates", h, w_g, BF16, tm=tm, tn=1024, epilogue=lambda acc: jax.nn.sigmoid(acc))

    w_iq_p = jnp.pad(w_iq.reshape(Q_RANK, N_HEADS_IDX, HEAD_DIM_IDX),
                     ((0, 0), (0, 0), (0, LANES - HEAD_DIM_IDX))).reshape(Q_RANK, N_HEADS_IDX * LANES).astype(BF16)
    w_uk_t = jnp.transpose(w_uk.reshape(KV_RANK, N_HEADS_A, HEAD_DIM_A), (1, 2, 0)).astype(BF16)
    w_uv_h = jnp.transpose(w_uv.reshape(KV_RANK, N_HEADS_A, HEAD_DIM_A), (1, 0, 2)).astype(BF16)
    ql, qi, wt, ckv, kidx = _prep(
        small, cq_norm_g.reshape(1, Q_RANK), ckv_norm_g.reshape(1, KV_RANK),
        _pad_cols(kidx_norm_g.reshape(1, HEAD_DIM_IDX), LANES), _pad_cols(kidx_norm_b.reshape(1, HEAD_DIM_IDX), LANES),
        w_uq.astype(BF16), w_iq_p, w_uk_t)
    bt = _bias_tiles(rel_bias)
    y_a = _attention(qi, wt, kidx, ckv, ql, bt, w_uv_h, batch, seq).reshape(t, ha)

    expand = np.zeros((LANES, d_inner), np.float32)
    for hh in range(n_heads_b):
        expand[hh, hh * SSM_HEAD_DIM:(hh + 1) * SSM_HEAD_DIM] = 1.0
    aneg = _pad_cols(-jnp.exp(a_log.astype(F32)).reshape(1, n_heads_b), LANES)
    y_b = _ssd(xc.reshape(batch, seq, d_xbc), zs.reshape(batch, seq, d_inner), small.reshape(batch, seq, small_w),
               _pad_cols(dt_bias.reshape(1, n_heads_b), LANES), aneg,
               jnp.repeat(d_skip, SSM_HEAD_DIM).reshape(1, d_inner), ssm_norm_g.reshape(1, d_inner),
               jnp.asarray(expand, BF16), dt_col // LANES).reshape(t, d_inner)

    tn = 1024
    tp = 512
    merged = _matmul2("proj_ab", y_a, w_proj_a.astype(BF16), y_b, w_proj_b.astype(BF16), BF16, tm=tm, tn=tp,
                      epilogue=lambda ra, rb, sa_ref, sb_ref: (sa_ref[...].astype(F32) * ra
                                                               + sb_ref[...].astype(F32) * rb),
                      extras=(sg, sg),
                      extra_specs=(pl.BlockSpec((tm, tp), lambda i, j: (i, j)),
                                   pl.BlockSpec((tm, tp), lambda i, j: (i, j + d // tp))))
    x1 = _matmul("w_out", merged, w_out.astype(BF16), F32, tm=tm, tn=tn,
                 epilogue=lambda acc, x_ref, g_ref: x_ref[...] + g_ref[0] * acc,
                 extras=(x2, g1), extra_specs=(_tile_spec(tm, tn), _batch_row_spec(tn, per_b)))

    d_ff = w_gate.shape[1]
    act = _ffn_in(x1, norm2_g.reshape(1, d), sc2, sh2, w_gate, w_up, seq, tm=tm, tn=512)

    def down_epilogue(acc, x_ref, g_ref, f_ref):
        xo = x_ref[...] + g_ref[0] * acc
        ms = jnp.mean(xo * xo, axis=-1, keepdims=True)
        return xo * lax.rsqrt(ms + EPS) * f_ref[...]

    tmd = 1024
    assert d_ff % 4 == 0
    return _matmul("ffn_down", act, w_down.astype(BF16), F32, tm=tmd, tn=d, tk=d_ff // 4,
                   epilogue=down_epilogue, extras=(x1, g2, final_g.reshape(1, d)),
                   extra_specs=(_tile_spec(tmd, d), _batch_row_spec(d, seq // tmd),
                                pl.BlockSpec((1, d), lambda i, j, k: (0, 0))))


def kernel(x, c, w_ada, b_ada, norm1_g, w_in, cq_norm_g, ckv_norm_g, kidx_norm_g, kidx_norm_b, w_uq, w_iq, w_uk, w_uv, rel_bias, conv_w, conv_b, dt_bias, a_log, d_skip, ssm_norm_g, w_proj_a, w_proj_b, w_out, norm2_g, w_gate, w_up, w_down, final_g):
    batch, seq, d = x.shape
    assert w_ada.shape[0] == 1, "single-layer block"
    c8 = jnp.pad(c, ((0, SUBLANES - batch % SUBLANES if batch % SUBLANES else 0), (0, 0)))
    mod = _ada(c8, w_ada[0], b_ada[0].reshape(1, -1))[:batch]
    out = _layer(x.reshape(batch * seq, d), mod, batch, seq, norm1_g[0], w_in[0], cq_norm_g[0], ckv_norm_g[0],
                 kidx_norm_g[0], kidx_norm_b[0], w_uq[0], w_iq[0], w_uk[0], w_uv[0], rel_bias, conv_w[0],
                 conv_b[0], dt_bias[0], a_log[0], d_skip[0], ssm_norm_g[0], w_proj_a[0], w_proj_b[0], w_out[0],
                 norm2_g[0], w_gate[0], w_up[0], w_down[0], final_g)
    return out.reshape(batch, seq, d)
```

```python
import functools
import math

import numpy as np
import jax
import jax.numpy as jnp
from jax import lax
from jax.experimental import pallas as pl
from jax.experimental.pallas import tpu as pltpu

F32 = jnp.float32
BF16 = jnp.bfloat16
I32 = jnp.int32
I16 = jnp.int16

EPS = 1e-6
LANES = 128
SUBLANES = 8
VMEM_LIMIT = 56 * 2**20

N_HEADS_A = 16
HEAD_DIM_A = 128
Q_RANK = 512
KV_RANK = 256
N_HEADS_IDX = 16
HEAD_DIM_IDX = 64
TOP_K_MAX = 256
Q_BLOCK = 128
KEY_CHUNK = 512
HEAD_GROUP = 4
SM_ROWS = 64
LOG2E = 1.4426950408889634
N_BUCKETS = 32
MAX_DISTANCE = 128
SSM_HEAD_DIM = 64
SSM_GROUPS = 8
D_STATE = 128
NORM_ROWS = 32
CONV_WIDTH = 4
CONV_STRIP = 256
CONV_ROWS = 64
CHUNK = 128
NEG = -1e30
NEG_INF_KEY = -2139095041


def _cparams(*sem):
    return pltpu.CompilerParams(dimension_semantics=sem, vmem_limit_bytes=VMEM_LIMIT)


def _silu(x):
    return x * jax.nn.sigmoid(x)


def _split3(v):
    hi = v.astype(BF16)
    r1 = v - hi.astype(F32)
    mid = r1.astype(BF16)
    lo = (r1 - mid.astype(F32)).astype(BF16)
    return hi, mid, lo


def _dot(a, b):
    return jnp.dot(a, b, preferred_element_type=F32)


def _dot_nt(a, b):
    return lax.dot_general(a, b, (((1,), (1,)), ((), ())), preferred_element_type=F32)


def _ada_kernel(c_ref, w_ref, b_ref, o_ref):
    a = _silu(c_ref[...])
    o_ref[...] = _dot(a, w_ref[...]) + b_ref[...]


def _ada(c8, w, b):
    m, d = c8.shape
    n = w.shape[1]
    tn = 1024
    return pl.pallas_call(
        _ada_kernel,
        grid=(n // tn,),
        in_specs=[pl.BlockSpec((m, d), lambda j: (0, 0)),
                  pl.BlockSpec((d, tn), lambda j: (0, j)),
                  pl.BlockSpec((1, tn), lambda j: (0, j))],
        out_specs=pl.BlockSpec((m, tn), lambda j: (0, j)),
        out_shape=jax.ShapeDtypeStruct((m, n), F32),
        compiler_params=_cparams("parallel"),
        name="ada",
    )(c8, w, b)


def _norm_kernel(x_ref, g_ref, sc_ref, sh_ref, o_ref):
    x = x_ref[...]
    ms = jnp.mean(x * x, axis=-1, keepdims=True)
    y = x * lax.rsqrt(ms + EPS) * g_ref[...]
    o_ref[...] = (y * (1.0 + sc_ref[0]) + sh_ref[0]).astype(o_ref.dtype)


def _norm_mod(x2, g, sc, sh, seq):
    t, d = x2.shape
    tm = 512
    per_b = seq // tm
    return pl.pallas_call(
        _norm_kernel,
        grid=(t // tm,),
        in_specs=[pl.BlockSpec((tm, d), lambda i: (i, 0)),
                  pl.BlockSpec((1, d), lambda i: (0, 0)),
                  pl.BlockSpec((1, 1, d), lambda i: (i // per_b, 0, 0)),
                  pl.BlockSpec((1, 1, d), lambda i: (i // per_b, 0, 0))],
        out_specs=pl.BlockSpec((tm, d), lambda i: (i, 0)),
        out_shape=jax.ShapeDtypeStruct((t, d), BF16),
        compiler_params=_cparams("parallel"),
        name="norm_mod",
    )(x2, g, sc, sh)


def _mm_kernel(*refs, nk, n_extra, epilogue):
    a_ref, w_ref = refs[0], refs[1]
    extra = refs[2:2 + n_extra]
    o_ref = refs[2 + n_extra]
    if nk == 1:
        acc = _dot(a_ref[...], w_ref[...])
        o_ref[...] = epilogue(acc, *extra).astype(o_ref.dtype)
        return
    k = pl.program_id(2)

    @pl.when(k == 0)
    def _():
        o_ref[...] = _dot(a_ref[...], w_ref[...])

    @pl.when((k > 0) & (k < nk - 1))
    def _():
        o_ref[...] += _dot(a_ref[...], w_ref[...])

    @pl.when(k == nk - 1)
    def _():
        o_ref[...] = epilogue(o_ref[...] + _dot(a_ref[...], w_ref[...]), *extra)


def _matmul(name, a, w, out_dtype, *, tm, tn, tk=None, epilogue=None, extras=(), extra_specs=()):
    m, kdim = a.shape
    n = w.shape[1]
    tk = kdim if tk is None else tk
    nk = kdim // tk
    assert m % tm == 0 and n % tn == 0 and kdim % tk == 0
    if epilogue is None:
        epilogue = lambda acc: acc
    assert nk == 1 or (nk >= 2 and out_dtype == F32), "split K accumulates in the f32 output block"
    return pl.pallas_call(
        functools.partial(_mm_kernel, nk=nk, n_extra=len(extras), epilogue=epilogue),
        grid=(m // tm, n // tn, nk),
        in_specs=[pl.BlockSpec((tm, tk), lambda i, j, k: (i, k)),
                  pl.BlockSpec((tk, tn), lambda i, j, k: (k, j)),
                  *extra_specs],
        out_specs=pl.BlockSpec((tm, tn), lambda i, j, k: (i, j)),
        out_shape=jax.ShapeDtypeStruct((m, n), out_dtype),
        compiler_params=_cparams("parallel", "parallel", "arbitrary"),
        name=name,
    )(a, w, *extras)


def _tile_spec(tm, tn):
    return pl.BlockSpec((tm, tn), lambda i, j, k: (i, j))


def _batch_row_spec(tn, per_b):
    return pl.BlockSpec((1, 1, tn), lambda i, j, k: (i // per_b, 0, j))


def _mm2_kernel(a1_ref, w1_ref, a2_ref, w2_ref, *rest, epilogue):
    r1 = _dot(a1_ref[...], w1_ref[...])
    r2 = _dot(a2_ref[...], w2_ref[...])
    *extra, o_ref = rest
    o_ref[...] = epilogue(r1, r2, *extra).astype(o_ref.dtype)


def _matmul2(name, a1, w1, a2, w2, out_dtype, *, tm, tn, epilogue, extras=(), extra_specs=()):
    m, k1 = a1.shape
    k2 = a2.shape[1]
    n = w1.shape[1]
    assert m % tm == 0 and n % tn == 0 and w2.shape[1] == n
    return pl.pallas_call(
        functools.partial(_mm2_kernel, epilogue=epilogue),
        grid=(m // tm, n // tn),
        in_specs=[pl.BlockSpec((tm, k1), lambda i, j: (i, 0)),
                  pl.BlockSpec((k1, tn), lambda i, j: (0, j)),
                  pl.BlockSpec((tm, k2), lambda i, j: (i, 0)),
                  pl.BlockSpec((k2, tn), lambda i, j: (0, j)),
                  *extra_specs],
        out_specs=pl.BlockSpec((tm, tn), lambda i, j: (i, j)),
        out_shape=jax.ShapeDtypeStruct((m, n), out_dtype),
        compiler_params=_cparams("parallel", "parallel"),
        name=name,
    )(a1, w1, a2, w2, *extras)


def _ffn_in_kernel(x_ref, g_ref, sc_ref, sh_ref, wg_ref, wu_ref, o_ref, h_s):
    @pl.when(pl.program_id(1) == 0)
    def _():
        gain = g_ref[...] * (1.0 + sc_ref[0])

        def rows(r, carry):
            sl = pl.ds(pl.multiple_of(r * NORM_ROWS, NORM_ROWS), NORM_ROWS)
            x = x_ref[sl, :]
            ms = jnp.mean(x * x, axis=-1, keepdims=True)
            h_s[sl, :] = (x * lax.rsqrt(ms + EPS) * gain + sh_ref[0]).astype(h_s.dtype)
            return carry

        lax.fori_loop(0, x_ref.shape[0] // NORM_ROWS, rows, 0, unroll=4)

    h = h_s[...]
    rg = _dot(h, wg_ref[...].astype(BF16))
    ru = _dot(h, wu_ref[...].astype(BF16))
    o_ref[...] = (_silu(rg) * ru).astype(o_ref.dtype)


def _ffn_in(x, g, sc, sh, w_gate, w_up, seq, *, tm, tn):
    t, d = x.shape
    f = w_gate.shape[1]
    assert t % tm == 0 and f % tn == 0 and seq % tm == 0
    per_b = seq // tm
    return pl.pallas_call(
        _ffn_in_kernel,
        grid=(t // tm, f // tn),
        in_specs=[pl.BlockSpec((tm, d), lambda i, j: (i, 0)),
                  pl.BlockSpec((1, d), lambda i, j: (0, 0)),
                  pl.BlockSpec((1, 1, d), lambda i, j: (i // per_b, 0, 0)),
                  pl.BlockSpec((1, 1, d), lambda i, j: (i // per_b, 0, 0)),
                  pl.BlockSpec((d, tn), lambda i, j: (0, j)),
                  pl.BlockSpec((d, tn), lambda i, j: (0, j))],
        out_specs=pl.BlockSpec((tm, tn), lambda i, j: (i, j)),
        out_shape=jax.ShapeDtypeStruct((t, f), BF16),
        scratch_shapes=[pltpu.VMEM((tm, d), BF16)],
        compiler_params=_cparams("parallel", "arbitrary"),
        name="ffn_norm_gate_up",
    )(x, g, sc, sh, w_gate, w_up)


def _prep_kernel(small_ref, cqg_ref, ckvg_ref, kg_ref, kb_ref, wuq_ref, wiq_ref, wukt_ref,
                 ql_ref, qi_ref, wt_ref, ckv_ref, kidx_ref, *, tm):
    nqb = tm // Q_BLOCK
    cq = small_ref[:, 0:Q_RANK]
    ms = jnp.mean(cq * cq, axis=-1, keepdims=True)
    cqn = (cq * lax.rsqrt(ms + EPS) * cqg_ref[...]).astype(BF16)

    q = _dot(cqn, wuq_ref[...]).astype(BF16)
    scale = HEAD_DIM_A ** -0.5 * LOG2E
    for h in range(N_HEADS_A):
        qlh = (_dot(q[:, h * HEAD_DIM_A:(h + 1) * HEAD_DIM_A], wukt_ref[h]) * scale).astype(BF16)
        for b in range(nqb):
            ql_ref[b, h * Q_BLOCK:(h + 1) * Q_BLOCK, :] = qlh[b * Q_BLOCK:(b + 1) * Q_BLOCK, :]

    qix = _dot(cqn, wiq_ref[...]).astype(BF16)
    for h in range(N_HEADS_IDX):
        for b in range(nqb):
            qi_ref[b, h * Q_BLOCK:(h + 1) * Q_BLOCK, :] = (
                qix[b * Q_BLOCK:(b + 1) * Q_BLOCK, h * LANES:(h + 1) * LANES])

    ckv = small_ref[:, Q_RANK:Q_RANK + KV_RANK]
    ms = jnp.mean(ckv * ckv, axis=-1, keepdims=True)
    ckv_ref[...] = (ckv * lax.rsqrt(ms + EPS) * ckvg_ref[...]).astype(BF16)

    slab = small_ref[:, Q_RANK + KV_RANK:Q_RANK + KV_RANK + LANES]
    lane = lax.broadcasted_iota(I32, slab.shape, 1)
    kmask = lane < HEAD_DIM_IDX
    mu = jnp.sum(jnp.where(kmask, slab, 0.0), axis=-1, keepdims=True) * (1.0 / HEAD_DIM_IDX)
    xc = jnp.where(kmask, slab - mu, 0.0)
    var = jnp.sum(xc * xc, axis=-1, keepdims=True) * (1.0 / HEAD_DIM_IDX)
    y = xc * lax.rsqrt(var + EPS) * kg_ref[...] + kb_ref[...]
    kidx_ref[...] = jnp.where(kmask, y, 0.0).astype(BF16)

    wscale = N_HEADS_IDX ** -0.5 * HEAD_DIM_IDX ** -0.5
    slab_t = slab.T
    wt = slab_t[HEAD_DIM_IDX:HEAD_DIM_IDX + N_HEADS_IDX, :] * wscale
    for b in range(nqb):
        wt_ref[b] = wt[:, b * Q_BLOCK:(b + 1) * Q_BLOCK]


def _prep(small, cqg, ckvg, kg, kb, wuq, wiq, wukt):
    t = small.shape[0]
    tm = 512
    nqb = tm // Q_BLOCK
    nq = t // Q_BLOCK
    hq = N_HEADS_A * Q_BLOCK
    const2 = lambda i: (0, 0)
    return pl.pallas_call(
        functools.partial(_prep_kernel, tm=tm),
        grid=(t // tm,),
        in_specs=[pl.BlockSpec((tm, small.shape[1]), lambda i: (i, 0)),
                  pl.BlockSpec(cqg.shape, const2),
                  pl.BlockSpec(ckvg.shape, const2),
                  pl.BlockSpec(kg.shape, const2),
                  pl.BlockSpec(kb.shape, const2),
                  pl.BlockSpec(wuq.shape, const2),
                  pl.BlockSpec(wiq.shape, const2),
                  pl.BlockSpec(wukt.shape, lambda i: (0, 0, 0))],
        out_specs=[pl.BlockSpec((nqb, hq, KV_RANK), lambda i: (i, 0, 0)),
                   pl.BlockSpec((nqb, hq, LANES), lambda i: (i, 0, 0)),
                   pl.BlockSpec((nqb, N_HEADS_IDX, Q_BLOCK), lambda i: (i, 0, 0)),
                   pl.BlockSpec((tm, KV_RANK), lambda i: (i, 0)),
                   pl.BlockSpec((tm, LANES), lambda i: (i, 0))],
        out_shape=[jax.ShapeDtypeStruct((nq, hq, KV_RANK), BF16),
                   jax.ShapeDtypeStruct((nq, hq, LANES), BF16),
                   jax.ShapeDtypeStruct((nq, N_HEADS_IDX, Q_BLOCK), F32),
                   jax.ShapeDtypeStruct((t, KV_RANK), BF16),
                   jax.ShapeDtypeStruct((t, LANES), BF16)],
        compiler_params=_cparams("parallel"),
        name="dsa_prep",
    )(small, cqg, ckvg, kg, kb, wuq, wiq, wukt)


def _bucket_tiles():
    t = np.arange(Q_BLOCK)[:, None]
    s = np.arange(Q_BLOCK)[None, :]
    max_exact = N_BUCKETS // 2
    tiles = []
    for d in range(3):
        n = np.maximum(t - s + d * Q_BLOCK, 0)
        nf = np.maximum(n, 1).astype(np.float32)
        large = max_exact + (np.log(nf / max_exact) / math.log(MAX_DISTANCE / max_exact)
                             * (N_BUCKETS - max_exact)).astype(np.int32)
        large = np.minimum(large, N_BUCKETS - 1)
        tiles.append(np.where(n < max_exact, n, large))
    assert 2 * Q_BLOCK - (Q_BLOCK - 1) >= MAX_DISTANCE
    tiles[2] = np.full_like(tiles[2], N_BUCKETS - 1)
    return np.stack(tiles).astype(np.int32)


def _bias_kernel(rel_ref, bucket_ref, o_ref):
    h = pl.program_id(0)
    bucket = bucket_ref[...]
    out = jnp.zeros(bucket.shape, F32)
    for b in range(N_BUCKETS):
        out = jnp.where(bucket == b, rel_ref[b, h], out)
    o_ref[:, 0] = (out - rel_ref[N_BUCKETS - 1, h]) * LOG2E


def _bias_tiles(rel_bias):
    buckets = jnp.asarray(_bucket_tiles())
    return pl.pallas_call(
        _bias_kernel,
        grid=(N_HEADS_A,),
        in_specs=[pl.BlockSpec(memory_space=pltpu.SMEM),
                  pl.BlockSpec((3, Q_BLOCK, Q_BLOCK), lambda h: (0, 0, 0))],
        out_specs=pl.BlockSpec((3, 1, Q_BLOCK, Q_BLOCK), lambda h: (0, h, 0, 0)),
        out_shape=jax.ShapeDtypeStruct((3, N_HEADS_A, Q_BLOCK, Q_BLOCK), F32),
        compiler_params=_cparams("arbitrary"),
        name="rel_bias_tiles",
    )(rel_bias, buckets)


def _attn_kernel(qi_ref, wt_ref, kidx_ref, ckv_ref, ql_ref, bt_ref, wuv_ref, ya_ref,
                 key_s, hi16_s, lo16_s, c16_s, mb_s, st_s, x_buf, p_buf, al_buf, acc_s, m_s, lp_s, j_s,
                 *, seq, top_k):
    ck = KEY_CHUNK
    sub = ck // Q_BLOCK
    i = pl.program_id(1)
    nchunk = (i + sub) // sub
    t_row = i * Q_BLOCK + lax.broadcasted_iota(I32, (1, Q_BLOCK), 1)
    qi = qi_ref[0]

    def score_matmul(c, slot):
        k0 = pl.multiple_of(c * ck, ck)
        st_s[slot] = _dot_nt(kidx_ref[0, pl.ds(k0, ck), :], qi)

    def score_reduce(c, slot):
        k0 = pl.multiple_of(c * ck, ck)
        for u in range(sub):
            acc = jnp.zeros((Q_BLOCK, Q_BLOCK), F32)
            for h in range(N_HEADS_IDX):
                sh = st_s[slot, u * Q_BLOCK:(u + 1) * Q_BLOCK, h * Q_BLOCK:(h + 1) * Q_BLOCK]
                acc = acc + jnp.maximum(sh, 0.0) * wt_ref[0, h:h + 1, :]
            s_pos = k0 + u * Q_BLOCK + lax.broadcasted_iota(I32, (Q_BLOCK, Q_BLOCK), 0)
            acc = jnp.where(s_pos <= t_row, acc, -jnp.inf)
            bits = lax.bitcast_convert_type(acc, I32)
            key = jnp.where(bits < 0, bits ^ 0x7FFFFFFF, bits)
            urows = slice(u * Q_BLOCK, (u + 1) * Q_BLOCK)
            key_s[c, urows, :] = key
            hi16_s[c, urows, :] = (key >> 16).astype(I16)
            lo16_s[c, urows, :] = ((key & 0xFFFF) - 32768).astype(I16)

    odd = nchunk % 2

    @pl.when(odd == 1)
    def _():
        score_matmul(0, 0)
        score_reduce(0, 0)

    def score_pair(p, carry):
        c = odd + 2 * p
        score_matmul(c, 0)
        score_matmul(c + 1, 1)
        score_reduce(c, 0)
        score_reduce(c + 1, 1)
        return carry

    lax.fori_loop(0, nchunk // 2, score_pair, 0)

    def count(pred):
        def body(c, cnt):
            tile = key_s[c]
            s_pos = c * ck + lax.broadcasted_iota(I32, (ck, Q_BLOCK), 0)
            hit = jnp.where(pred(tile, s_pos), 1, 0).astype(I32)
            return cnt + jnp.sum(hit.reshape(ck // SUBLANES, SUBLANES, Q_BLOCK), axis=0)
        cnt = lax.fori_loop(0, nchunk, body, jnp.zeros((SUBLANES, Q_BLOCK), I32))
        return jnp.sum(cnt, axis=0, keepdims=True)

    pack_rows = 2 * SUBLANES
    one16 = jnp.ones((pack_rows, Q_BLOCK), I16)
    zero16 = jnp.zeros((pack_rows, Q_BLOCK), I16)

    def pack16(v):
        return jnp.broadcast_to(v, (pack_rows, Q_BLOCK)).astype(I16)

    def bisect16_static(arr_s, n, n_lo0):
        def step(_, state):
            lo, hi, n_lo = state
            mid = (lo + hi) >> 1
            thr16 = pack16(mid)
            hits = [jnp.where(arr_s[c, r * pack_rows:(r + 1) * pack_rows, :] >= thr16, one16, zero16)
                    for c in range(n) for r in range(ck // pack_rows)]
            while len(hits) > 1:
                hits = [hits[k] + hits[k + 1] for k in range(0, len(hits) - 1, 2)] + hits[len(hits) & ~1:]
            n_mid = jnp.sum(hits[0].astype(I32), axis=0, keepdims=True)
            ok = n_mid >= top_k
            return jnp.where(ok, mid, lo), jnp.where(ok, hi, mid), jnp.where(ok, n_mid, n_lo)

        lo, _, n_lo = lax.fori_loop(
            0, 16, step, (jnp.full((1, Q_BLOCK), -2**15, I32), jnp.full((1, Q_BLOCK), 2**15, I32), n_lo0))
        return lo, n_lo

    def bisect16(arr_s, n_lo0):
        branches = [functools.partial(bisect16_static, arr_s, n) for n in range(1, seq // ck + 1)]
        return lax.switch(nchunk - 1, branches, n_lo0)

    prefix, n_prefix = bisect16(hi16_s, jnp.broadcast_to(nchunk * ck, (1, Q_BLOCK)).astype(I32))
    p16 = pack16(prefix)[None]

    def low_chunk(c, carry):
        hi = hi16_s[c].reshape(ck // pack_rows, pack_rows, Q_BLOCK)
        lo = lo16_s[c].reshape(ck // pack_rows, pack_rows, Q_BLOCK)
        c16 = jnp.where(hi > p16, jnp.full((), 2**15 - 1, I16), jnp.where(hi == p16, lo, jnp.full((), -2**15, I16)))
        c16_s[c] = c16.reshape(ck, Q_BLOCK)
        return carry

    lax.fori_loop(0, nchunk, low_chunk, 0)
    low, n_ge = bisect16(c16_s, n_prefix)
    thr = prefix * 65536 + (low + 32768)
    tie = (n_ge > top_k) & (thr > NEG_INF_KEY)
    j_s[...] = jnp.full(j_s.shape, seq, I32)

    @pl.when(jnp.max(jnp.where(tie, 1, 0)) > 0)
    def _():
        need = top_k - count(lambda tile, s_pos: tile > thr)

        def jbisect(_, lohi):
            lo, hi = lohi
            mid = (lo + hi) >> 1
            ok = count(lambda tile, s_pos: (tile == thr) & (s_pos <= mid)) >= need
            return jnp.where(ok, lo, mid), jnp.where(ok, mid, hi)
        lo_j = jnp.full((1, Q_BLOCK), -1, I32)
        hi_j = jnp.full((1, Q_BLOCK), seq - 1, I32)
        _, cut = lax.fori_loop(0, max(1, (seq - 1).bit_length()) + 1, jbisect, (lo_j, hi_j))
        j_s[...] = jnp.broadcast_to(jnp.where(tie, cut, seq), j_s.shape)

    jcut = j_s[0:1, :]

    def mask_chunk(c):
        tile = key_s[c]
        s_pos = c * ck + lax.broadcasted_iota(I32, (ck, Q_BLOCK), 0)
        sel = (tile > thr) | ((tile == thr) & (s_pos <= jcut))
        sel = sel & (s_pos <= t_row)
        mbt = jnp.where(sel, 0.0, NEG)
        for u in range(sub):
            mb_s[c * sub + u] = mbt[u * Q_BLOCK:(u + 1) * Q_BLOCK, :].T

    @pl.when(odd == 1)
    def _():
        mask_chunk(0)

    def mask_pair(p, carry):
        mask_chunk(odd + 2 * p)
        mask_chunk(odd + 2 * p + 1)
        return carry

    lax.fori_loop(0, nchunk // 2, mask_pair, 0)
    neg_tile = seq // Q_BLOCK
    mb_s[neg_tile] = jnp.full((Q_BLOCK, Q_BLOCK), NEG, F32)

    m_s[...] = jnp.full(m_s.shape, -jnp.inf, F32)
    lp_s[...] = jnp.zeros(lp_s.shape, F32)
    acc_s[...] = jnp.zeros(acc_s.shape, F32)

    ngroup = N_HEADS_A // HEAD_GROUP
    grows_n = HEAD_GROUP * Q_BLOCK
    nparts = grows_n // SM_ROWS
    first_start = i + 1 - sub * nchunk

    def chunk_start(c):
        return jnp.maximum(first_start + c * sub, 0)

    def load_kv(c):
        k0 = pl.multiple_of(chunk_start(c) * Q_BLOCK, Q_BLOCK)
        return ckv_ref[0, pl.ds(k0, ck), :]

    def part_rows(g, part):
        h = g * HEAD_GROUP + part * SM_ROWS // Q_BLOCK
        q0 = (part * SM_ROWS) % Q_BLOCK
        rows = slice(h * Q_BLOCK + q0, h * Q_BLOCK + q0 + SM_ROWS)
        return h, q0, rows, slice(part * SM_ROWS, (part + 1) * SM_ROWS)

    def stage_a(c, g, near):
        start = chunk_start(c)
        shared = (c == 0) & (nchunk > 1)
        tiles = []
        for u in range(sub):
            blk = start + u
            tiles.append(jnp.where(shared & (blk >= first_start + sub), neg_tile, blk))
        s = _dot_nt(ql_ref[0, g * grows_n:(g + 1) * grows_n, :], load_kv(c))
        for part in range(nparts):
            h, q0, rows, lrows = part_rows(g, part)
            mx = None
            for u in range(sub):
                cols = slice(u * Q_BLOCK, (u + 1) * Q_BLOCK)
                x = s[lrows, cols] + mb_s[tiles[u], q0:q0 + SM_ROWS, :]
                if near:
                    x = x + bt_ref[jnp.clip(i - (start + u), 0, 2), h, q0:q0 + SM_ROWS, :]
                x_buf[g, lrows, cols] = x
                mx = x if mx is None else jnp.maximum(mx, x)
            m_prev = m_s[rows, :]
            m_cur = jnp.broadcast_to(jnp.max(mx, axis=1, keepdims=True), (SM_ROWS, LANES))
            m_new = jnp.maximum(m_prev, m_cur)
            al_buf[g, lrows, :] = jnp.exp2(m_prev - m_new)
            m_s[rows, :] = m_new

    def stage_b(g):
        for part in range(nparts):
            _, _, rows, lrows = part_rows(g, part)
            m_new = m_s[rows, :]
            ps = [jnp.exp2(x_buf[g, lrows, u * Q_BLOCK:(u + 1) * Q_BLOCK] - m_new) for u in range(sub)]
            lp_s[rows, :] = al_buf[g, lrows, :] * lp_s[rows, :] + ((ps[0] + ps[1]) + (ps[2] + ps[3]))
            p_buf[g, lrows, :] = jnp.concatenate(ps, axis=1).astype(BF16)

    def stage_c(c, g):
        grows = slice(g * grows_n, (g + 1) * grows_n)
        alpha = jnp.concatenate([al_buf[g]] * (KV_RANK // LANES), axis=1)
        acc_s[grows, :] = alpha * acc_s[grows, :] + _dot(p_buf[g], load_kv(c))

    assert sub == 4 and ngroup >= 2

    def chunk_steps(c, near, first):
        for k in range(ngroup):
            stage_a(c, k, near)
            if k >= 1 or not first:
                stage_b((k - 1) % ngroup)
            if k >= 2:
                stage_c(c, k - 2)
            elif not first:
                stage_c(c - 1, k - 2 + ngroup)

    def drain(c):
        stage_b(ngroup - 1)
        stage_c(c, ngroup - 2)
        stage_c(c, ngroup - 1)

    last = nchunk - 1

    @pl.when(nchunk == 1)
    def _():
        chunk_steps(last, True, True)
        drain(last)

    @pl.when(nchunk > 1)
    def _():
        chunk_steps(0, False, True)

        def far_body(c, carry):
            chunk_steps(c, False, False)
            return carry

        lax.fori_loop(1, last, far_body, 0)
        chunk_steps(last, True, False)
        drain(last)

    for h in range(N_HEADS_A):
        rows = slice(h * Q_BLOCK, (h + 1) * Q_BLOCK)
        l = jnp.sum(lp_s[rows, :], axis=1, keepdims=True)
        o = (acc_s[rows, :] / l).astype(BF16)
        ya_ref[0, :, h * HEAD_DIM_A:(h + 1) * HEAD_DIM_A] = _dot(o, wuv_ref[h]).astype(ya_ref.dtype)


def _attention(qi, wt, kidx, ckv, ql, bt, wuv, batch, seq):
    nq = seq // Q_BLOCK
    hq = N_HEADS_A * Q_BLOCK
    ck = KEY_CHUNK
    grp = HEAD_GROUP * Q_BLOCK
    ngrp = N_HEADS_A // HEAD_GROUP
    assert seq % ck == 0
    top_k = min(TOP_K_MAX, seq // 4)
    kidx3 = kidx.reshape(batch, seq, LANES)
    ckv3 = ckv.reshape(batch, seq, KV_RANK)
    return pl.pallas_call(
        functools.partial(_attn_kernel, seq=seq, top_k=top_k),
        grid=(batch, nq),
        in_specs=[pl.BlockSpec((1, hq, LANES), lambda b, i: (b * nq + i, 0, 0)),
                  pl.BlockSpec((1, N_HEADS_IDX, Q_BLOCK), lambda b, i: (b * nq + i, 0, 0)),
                  pl.BlockSpec((1, seq, LANES), lambda b, i: (b, 0, 0)),
                  pl.BlockSpec((1, seq, KV_RANK), lambda b, i: (b, 0, 0)),
                  pl.BlockSpec((1, hq, KV_RANK), lambda b, i: (b * nq + i, 0, 0)),
                  pl.BlockSpec(bt.shape, lambda b, i: (0, 0, 0, 0)),
                  pl.BlockSpec(wuv.shape, lambda b, i: (0, 0, 0))],
        out_specs=pl.BlockSpec((1, Q_BLOCK, N_HEADS_A * HEAD_DIM_A), lambda b, i: (b, i, 0)),
        out_shape=jax.ShapeDtypeStruct((batch, seq, N_HEADS_A * HEAD_DIM_A), BF16),
        scratch_shapes=[pltpu.VMEM((seq // ck, ck, Q_BLOCK), I32),
                        pltpu.VMEM((seq // ck, ck, Q_BLOCK), I16),
                        pltpu.VMEM((seq // ck, ck, Q_BLOCK), I16),
                        pltpu.VMEM((seq // ck, ck, Q_BLOCK), I16),
                        pltpu.VMEM((nq + 1, Q_BLOCK, Q_BLOCK), F32),
                        pltpu.VMEM((2, ck, hq), F32),
                        pltpu.VMEM((ngrp, grp, ck), F32),
                        pltpu.VMEM((ngrp, grp, ck), BF16),
                        pltpu.VMEM((ngrp, grp, LANES), F32),
                        pltpu.VMEM((hq, KV_RANK), F32),
                        pltpu.VMEM((hq, LANES), F32),
                        pltpu.VMEM((hq, LANES), F32),
                        pltpu.VMEM((SUBLANES, Q_BLOCK), I32)],
        compiler_params=_cparams("parallel", "arbitrary"),
        name="dsa_attention",
    )(qi, wt, kidx3, ckv3, ql, bt, wuv)


def _xbc_conv_kernel(a_ref, w_ref, cw_ref, cb_ref, o_ref, tail_s, ya_s, yb_s, *, tm, per_b):
    i, j = pl.program_id(0), pl.program_id(1)

    @pl.when(i % per_b == 0)
    def _():
        tail_s[j] = jnp.zeros(tail_s.shape[1:], F32)

    nstrip = o_ref.shape[1] // CONV_STRIP
    y_bufs = (ya_s, yb_s)
    row0 = pl.multiple_of(jnp.minimum(i, 0) * SUBLANES, SUBLANES)

    def matmul_strip(s):
        cols = slice(s * CONV_STRIP, (s + 1) * CONV_STRIP)
        buf = y_bufs[s % 2]
        buf[0:SUBLANES, :] = tail_s[j, :, cols]
        buf[SUBLANES:SUBLANES + tm, :] = _dot(a_ref[...], w_ref[:, cols])

    def conv_strip(s):
        cols = slice(s * CONV_STRIP, (s + 1) * CONV_STRIP)
        buf = y_bufs[s % 2]
        tail_s[j, :, cols] = buf[pl.ds(row0 + tm, SUBLANES), :]
        for r in range(0, tm, CONV_ROWS):
            ext = buf[pl.ds(row0 + r, CONV_ROWS + SUBLANES), :]
            acc = ext * cw_ref[0:1, cols]
            for k in range(1, CONV_WIDTH):
                acc = pltpu.roll(acc, 1, axis=0) + ext * cw_ref[k:k + 1, cols]
            o_ref[r:r + CONV_ROWS, cols] = _silu(acc[SUBLANES:, :] + cb_ref[:, cols]).astype(o_ref.dtype)

    for s in range(nstrip):
        matmul_strip(s)
        if s >= 1:
            conv_strip(s - 1)
    conv_strip(nstrip - 1)


def _xbc_conv(a, w, conv_w, conv_b, seq, *, tm, tn):
    m, kdim = a.shape
    n = w.shape[1]
    assert m % tm == 0 and n % tn == 0 and seq % tm == 0
    return pl.pallas_call(
        functools.partial(_xbc_conv_kernel, tm=tm, per_b=seq // tm),
        grid=(m // tm, n // tn),
        in_specs=[pl.BlockSpec((tm, kdim), lambda i, j: (i, 0)),
                  pl.BlockSpec((kdim, tn), lambda i, j: (0, j)),
                  pl.BlockSpec((CONV_WIDTH, tn), lambda i, j: (0, j)),
                  pl.BlockSpec((1, tn), lambda i, j: (0, j))],
        out_specs=pl.BlockSpec((tm, tn), lambda i, j: (i, j)),
        out_shape=jax.ShapeDtypeStruct((m, n), BF16),
        scratch_shapes=[pltpu.VMEM((n // tn, SUBLANES, tn), F32),
                        pltpu.VMEM((SUBLANES + tm, CONV_STRIP), F32),
                        pltpu.VMEM((SUBLANES + tm, CONV_STRIP), F32)],
        compiler_params=_cparams("arbitrary", "arbitrary"),
        name="in_xbc_conv",
    )(a, w, conv_w, conv_b)


def _ssd_kernel(xc_ref, zs_ref, dt_ref, dtb_ref, aneg_ref, dskip_ref, ng_ref, e_ref, y_ref,
                state_s, de_s, *, d_inner):
    c = pl.program_id(1)
    hg = d_inner // SSM_GROUPS
    heads_g = hg // SSM_HEAD_DIM
    b_off = d_inner
    c_off = d_inner + SSM_GROUPS * D_STATE

    @pl.when(c == 0)
    def _():
        state_s[...] = jnp.zeros_like(state_s)

    x = dt_ref[0] + dtb_ref[...]
    dt = jnp.maximum(x, 0.0) + jnp.log1p(jnp.exp(-jnp.abs(x)))
    da = dt * aneg_ref[...]
    tri_i = lax.broadcasted_iota(I32, (CHUNK, CHUNK), 0) >= lax.broadcasted_iota(I32, (CHUNK, CHUNK), 1)
    tri = jnp.where(tri_i, 1.0, 0.0).astype(BF16)
    acs = sum(_dot(tri, part) for part in _split3(da))
    acs_t = acs.T
    row_t = acs_t - jnp.log(dt.T)
    acs_last = acs[CHUNK - 1:CHUNK, :]
    e = e_ref[...]
    de_s[...] = sum(_dot(part, e) for part in _split3(jnp.exp(acs_last - acs) * dt))
    el = jnp.broadcast_to(jnp.exp(acs_last), (SUBLANES, LANES))
    el_x = sum(_dot(part, e) for part in _split3(el))[0:1, :]
    lane = lax.broadcasted_iota(I32, (CHUNK, LANES), 1)

    for g in range(SSM_GROUPS):
        cols = slice(g * hg, (g + 1) * hg)
        xg = xc_ref[0, :, cols]
        bg = xc_ref[0, :, b_off + g * D_STATE:b_off + (g + 1) * D_STATE]
        cg = xc_ref[0, :, c_off + g * D_STATE:c_off + (g + 1) * D_STATE]
        cb = _dot_nt(cg, bg)
        cg_f = cg.astype(F32)
        sg = state_s[g]
        sg_b = sg.astype(BF16)
        tiles = []
        for pr in range(heads_g // 2):
            pcols = slice(pr * LANES, (pr + 1) * LANES)
            rhs = jnp.concatenate([xg[:, pcols], sg_b[:, pcols]], axis=0)
            outs = []
            for q in range(2):
                h = g * heads_g + pr * 2 + q
                col = jnp.broadcast_to(acs[:, h:h + 1], (CHUNK, CHUNK))
                w = cb * jnp.exp(jnp.where(tri_i, col - row_t[h:h + 1, :], NEG))
                lhs = jnp.concatenate([w.astype(BF16), (cg_f * jnp.exp(col)).astype(BF16)], axis=1)
                outs.append(_dot(lhs, rhs))
            tiles.append(jnp.where(lane < SSM_HEAD_DIM, outs[0], outs[1]))
        xg_f = xg.astype(F32)
        y = jnp.concatenate(tiles, axis=1) + dskip_ref[:, cols] * xg_f
        xd = (xg_f * de_s[:, cols]).astype(BF16)
        state_s[g] = sg * el_x[:, cols] + _dot(bg.astype(F32).T.astype(BF16), xd)
        yz = y * zs_ref[0, :, cols].astype(F32)
        ms = jnp.mean(yz * yz, axis=-1, keepdims=True)
        y_ref[0, :, cols] = (yz * lax.rsqrt(ms + EPS) * ng_ref[:, cols]).astype(y_ref.dtype)


def _ssd(xc, zs, small3, dtb, aneg, dskip_x, ng, expand, dt_block):
    batch, seq, d_xbc = xc.shape
    d_inner = zs.shape[2]
    nc = seq // CHUNK
    const2 = lambda b, c: (0, 0)
    return pl.pallas_call(
        functools.partial(_ssd_kernel, d_inner=d_inner),
        grid=(batch, nc),
        in_specs=[pl.BlockSpec((1, CHUNK, d_xbc), lambda b, c: (b, c, 0)),
                  pl.BlockSpec((1, CHUNK, d_inner), lambda b, c: (b, c, 0)),
                  pl.BlockSpec((1, CHUNK, LANES), lambda b, c: (b, c, dt_block)),
                  pl.BlockSpec(dtb.shape, const2),
                  pl.BlockSpec(aneg.shape, const2),
                  pl.BlockSpec(dskip_x.shape, const2),
                  pl.BlockSpec(ng.shape, const2),
                  pl.BlockSpec(expand.shape, const2)],
        out_specs=pl.BlockSpec((1, CHUNK, d_inner), lambda b, c: (b, c, 0)),
        out_shape=jax.ShapeDtypeStruct((batch, seq, d_inner), BF16),
        scratch_shapes=[pltpu.VMEM((SSM_GROUPS, D_STATE, d_inner // SSM_GROUPS), F32),
                        pltpu.VMEM((CHUNK, d_inner), F32)],
        compiler_params=_cparams("parallel", "arbitrary"),
        name="ssd",
    )(xc, zs, small3, dtb, aneg, dskip_x, ng, expand)


def _pad_cols(a, n):
    return jnp.pad(a, ((0, 0), (0, n - a.shape[1])))


def _layer(x2, mod, batch, seq, norm1_g, w_in, cq_norm_g, ckv_norm_g, kidx_norm_g, kidx_norm_b, w_uq, w_iq,
           w_uk, w_uv, rel_bias, conv_w, conv_b, dt_bias, a_log, d_skip, ssm_norm_g, w_proj_a, w_proj_b,
           w_out, norm2_g, w_gate, w_up, w_down, final_g):
    t, d = x2.shape
    n_heads_b = dt_bias.shape[0]
    d_inner = n_heads_b * SSM_HEAD_DIM
    d_xbc = d_inner + 2 * SSM_GROUPS * D_STATE
    ha = N_HEADS_A * HEAD_DIM_A
    sh1, sc1, g1, sh2, sc2, g2 = [m.reshape(batch, 1, d) for m in jnp.split(mod, 6, axis=-1)]

    o_z = Q_RANK + KV_RANK + HEAD_DIM_IDX + N_HEADS_IDX
    o_xbc = o_z + d_inner
    o_dt = o_xbc + d_xbc
    o_g = o_dt + n_heads_b
    small_w = 1024
    dt_col = 896
    assert o_z <= dt_col and dt_col + n_heads_b <= small_w and dt_col % LANES == 0
    w_small = jnp.concatenate(
        [_pad_cols(w_in[:, :o_z], dt_col), _pad_cols(w_in[:, o_dt:o_g], small_w - dt_col)], axis=1).astype(BF16)
    w_z = w_in[:, o_z:o_xbc].astype(BF16)
    w_xbc = w_in[:, o_xbc:o_dt].astype(BF16)
    w_g = w_in[:, o_g:].astype(BF16)

    tm = 1024
    per_b = seq // tm
    h = _norm_mod(x2, norm1_g.reshape(1, d), sc1, sh1, seq)
    small = _matmul("in_small", h, w_small, F32, tm=tm, tn=small_w)
    zs = _matmul("in_z", h, w_z, BF16, tm=tm, tn=2048, epilogue=lambda acc: _silu(acc))
    xc = _xbc_conv(h, w_xbc, conv_w, conv_b.reshape(1, d_xbc), seq, tm=tm, tn=1024)
    sg = _matmul("in_gates", h, w_g, BF16, tm=tm, tn=2048, epilogue=lambda acc: jax.nn.sigmoid(acc))

    w_iq_p = jnp.pad(w_iq.reshape(Q_RANK, N_HEADS_IDX, HEAD_DIM_IDX),
                     ((0, 0), (0, 0), (0, LANES - HEAD_DIM_IDX))).reshape(Q_RANK, N_HEADS_IDX * LANES).astype(BF16)
    w_uk_t = jnp.transpose(w_uk.reshape(KV_RANK, N_HEADS_A, HEAD_DIM_A), (1, 2, 0)).astype(BF16)
    w_uv_h = jnp.transpose(w_uv.reshape(KV_RANK, N_HEADS_A, HEAD_DIM_A), (1, 0, 2)).astype(BF16)
    ql, qi, wt, ckv, kidx = _prep(
        small, cq_norm_g.reshape(1, Q_RANK), ckv_norm_g.reshape(1, KV_RANK),
        _pad_cols(kidx_norm_g.reshape(1, HEAD_DIM_IDX), LANES), _pad_cols(kidx_norm_b.reshape(1, HEAD_DIM_IDX), LANES),
        w_uq.astype(BF16), w_iq_p, w_uk_t)
    bt = _bias_tiles(rel_bias)
    y_a = _attention(qi, wt, kidx, ckv, ql, bt, w_uv_h, batch, seq).reshape(t, ha)

    expand = np.zeros((LANES, d_inner), np.float32)
    for hh in range(n_heads_b):
        expand[hh, hh * SSM_HEAD_DIM:(hh + 1) * SSM_HEAD_DIM] = 1.0
    aneg = _pad_cols(-jnp.exp(a_log.astype(F32)).reshape(1, n_heads_b), LANES)
    y_b = _ssd(xc.reshape(batch, seq, d_xbc), zs.reshape(batch, seq, d_inner), small.reshape(batch, seq, small_w),
               _pad_cols(dt_bias.reshape(1, n_heads_b), LANES), aneg,
               jnp.repeat(d_skip, SSM_HEAD_DIM).reshape(1, d_inner), ssm_norm_g.reshape(1, d_inner),
               jnp.asarray(expand, BF16), dt_col // LANES).reshape(t, d_inner)

    tn = 1024
    tp = 512
    merged = _matmul2("proj_ab", y_a, w_proj_a.astype(BF16), y_b, w_proj_b.astype(BF16), BF16, tm=tm, tn=tp,
                      epilogue=lambda ra, rb, sa_ref, sb_ref: (sa_ref[...].astype(F32) * ra
                                                               + sb_ref[...].astype(F32) * rb),
                      extras=(sg, sg),
                      extra_specs=(pl.BlockSpec((tm, tp), lambda i, j: (i, j)),
                                   pl.BlockSpec((tm, tp), lambda i, j: (i, j + d // tp))))
    x1 = _matmul("w_out", merged, w_out.astype(BF16), F32, tm=tm, tn=tn,
                 epilogue=lambda acc, x_ref, g_ref: x_ref[...] + g_ref[0] * acc,
                 extras=(x2, g1), extra_specs=(_tile_spec(tm, tn), _batch_row_spec(tn, per_b)))

    d_ff = w_gate.shape[1]
    act = _ffn_in(x1, norm2_g.reshape(1, d), sc2, sh2, w_gate, w_up, seq, tm=tm, tn=512)

    def down_epilogue(acc, x_ref, g_ref, f_ref):
        xo = x_ref[...] + g_ref[0] * acc
        ms = jnp.mean(xo * xo, axis=-1, keepdims=True)
        return xo * lax.rsqrt(ms + EPS) * f_ref[...]

    tmd = 1024
    assert d_ff % 4 == 0
    return _matmul("ffn_down", act, w_down.astype(BF16), F32, tm=tmd, tn=d, tk=d_ff // 4,
                   epilogue=down_epilogue, extras=(x1, g2, final_g.reshape(1, d)),
                   extra_specs=(_tile_spec(tmd, d), _batch_row_spec(d, seq // tmd),
                                pl.BlockSpec((1, d), lambda i, j, k: (0, 0))))


def kernel(x, c, w_ada, b_ada, norm1_g, w_in, cq_norm_g, ckv_norm_g, kidx_norm_g, kidx_norm_b, w_uq, w_iq, w_uk, w_uv, rel_bias, conv_w, conv_b, dt_bias, a_log, d_skip, ssm_norm_g, w_proj_a, w_proj_b, w_out, norm2_g, w_gate, w_up, w_down, final_g):
    batch, seq, d = x.shape
    assert w_ada.shape[0] == 1, "single-layer block"
    c8 = jnp.pad(c, ((0, SUBLANES - batch % SUBLANES if batch % SUBLANES else 0), (0, 0)))
    mod = _ada(c8, w_ada[0], b_ada[0].reshape(1, -1))[:batch]
    out = _layer(x.reshape(batch * seq, d), mod, batch, seq, norm1_g[0], w_in[0], cq_norm_g[0], ckv_norm_g[0],
                 kidx_norm_g[0], kidx_norm_b[0], w_uq[0], w_iq[0], w_uk[0], w_uv[0], rel_bias, conv_w[0],
                 conv_b[0], dt_bias[0], a_log[0], d_skip[0], ssm_norm_g[0], w_proj_a[0], w_proj_b[0], w_out[0],
                 norm2_g[0], w_gate[0], w_up[0], w_down[0], final_g)
    return out.reshape(batch, seq, d)
```

```python
import functools
import math

import numpy as np
import jax
import jax.numpy as jnp
from jax import lax
from jax.experimental import pallas as pl
from jax.experimental.pallas import tpu as pltpu

F32 = jnp.float32
BF16 = jnp.bfloat16
I32 = jnp.int32
I16 = jnp.int16

EPS = 1e-6
LANES = 128
SUBLANES = 8
VMEM_LIMIT = 56 * 2**20

N_HEADS_A = 16
HEAD_DIM_A = 128
Q_RANK = 512
KV_RANK = 256
N_HEADS_IDX = 16
HEAD_DIM_IDX = 64
TOP_K_MAX = 256
Q_BLOCK = 128
KEY_CHUNK = 512
HEAD_GROUP = 4
SM_ROWS = 64
LOG2E = 1.4426950408889634
N_BUCKETS = 32
MAX_DISTANCE = 128
SSM_HEAD_DIM = 64
SSM_GROUPS = 8
D_STATE = 128
NORM_ROWS = 32
CONV_WIDTH = 4
CONV_STRIP = 256
CONV_ROWS = 64
CHUNK = 128
NEG = -1e30
NEG_INF_KEY = -2139095041


def _cparams(*sem):
    return pltpu.CompilerParams(dimension_semantics=sem, vmem_limit_bytes=VMEM_LIMIT)


def _silu(x):
    return x * jax.nn.sigmoid(x)


def _split3(v):
    hi = v.astype(BF16)
    r1 = v - hi.astype(F32)
    mid = r1.astype(BF16)
    lo = (r1 - mid.astype(F32)).astype(BF16)
    return hi, mid, lo


def _dot(a, b):
    return jnp.dot(a, b, preferred_element_type=F32)


def _dot_nt(a, b):
    return lax.dot_general(a, b, (((1,), (1,)), ((), ())), preferred_element_type=F32)


def _ada_kernel(c_ref, w_ref, b_ref, o_ref):
    a = _silu(c_ref[...])
    o_ref[...] = _dot(a, w_ref[...]) + b_ref[...]


def _ada(c8, w, b):
    m, d = c8.shape
    n = w.shape[1]
    tn = 1024
    return pl.pallas_call(
        _ada_kernel,
        grid=(n // tn,),
        in_specs=[pl.BlockSpec((m, d), lambda j: (0, 0)),
                  pl.BlockSpec((d, tn), lambda j: (0, j)),
                  pl.BlockSpec((1, tn), lambda j: (0, j))],
        out_specs=pl.BlockSpec((m, tn), lambda j: (0, j)),
        out_shape=jax.ShapeDtypeStruct((m, n), F32),
        compiler_params=_cparams("parallel"),
        name="ada",
    )(c8, w, b)


def _norm_rows(x_ref, g_ref, sc_ref, sh_ref, h_ref):
    gain = g_ref[...] * (1.0 + sc_ref[0])

    def rows(r, carry):
        sl = pl.ds(pl.multiple_of(r * NORM_ROWS, NORM_ROWS), NORM_ROWS)
        x = x_ref[sl, :]
        ms = jnp.mean(x * x, axis=-1, keepdims=True)
        h_ref[sl, :] = (x * lax.rsqrt(ms + EPS) * gain + sh_ref[0]).astype(h_ref.dtype)
        return carry

    lax.fori_loop(0, x_ref.shape[0] // NORM_ROWS, rows, 0, unroll=4)


def _norm_proj_kernel(x_ref, g_ref, sc_ref, sh_ref, w_ref, o_ref, h_ref):
    _norm_rows(x_ref, g_ref, sc_ref, sh_ref, h_ref)
    o_ref[...] = _dot(h_ref[...], w_ref[...])


def _norm_proj(x2, g, sc, sh, w, seq, *, tm):
    t, d = x2.shape
    n = w.shape[1]
    per_b = seq // tm
    assert t % tm == 0 and seq % tm == 0
    return pl.pallas_call(
        _norm_proj_kernel,
        grid=(t // tm,),
        in_specs=[pl.BlockSpec((tm, d), lambda i: (i, 0)),
                  pl.BlockSpec((1, d), lambda i: (0, 0)),
                  pl.BlockSpec((1, 1, d), lambda i: (i // per_b, 0, 0)),
                  pl.BlockSpec((1, 1, d), lambda i: (i // per_b, 0, 0)),
                  pl.BlockSpec((d, n), lambda i: (0, 0))],
        out_specs=[pl.BlockSpec((tm, n), lambda i: (i, 0)),
                   pl.BlockSpec((tm, d), lambda i: (i, 0))],
        out_shape=[jax.ShapeDtypeStruct((t, n), F32), jax.ShapeDtypeStruct((t, d), BF16)],
        compiler_params=_cparams("parallel"),
        name="norm_in_small",
    )(x2, g, sc, sh, w)


def _mm_kernel(*refs, nk, n_extra, epilogue):
    a_ref, w_ref = refs[0], refs[1]
    extra = refs[2:2 + n_extra]
    o_ref = refs[2 + n_extra]
    if nk == 1:
        acc = _dot(a_ref[...], w_ref[...])
        o_ref[...] = epilogue(acc, *extra).astype(o_ref.dtype)
        return
    k = pl.program_id(2)

    @pl.when(k == 0)
    def _():
        o_ref[...] = _dot(a_ref[...], w_ref[...])

    @pl.when((k > 0) & (k < nk - 1))
    def _():
        o_ref[...] += _dot(a_ref[...], w_ref[...])

    @pl.when(k == nk - 1)
    def _():
        o_ref[...] = epilogue(o_ref[...] + _dot(a_ref[...], w_ref[...]), *extra)


def _matmul(name, a, w, out_dtype, *, tm, tn, tk=None, epilogue=None, extras=(), extra_specs=()):
    m, kdim = a.shape
    n = w.shape[1]
    tk = kdim if tk is None else tk
    nk = kdim // tk
    assert m % tm == 0 and n % tn == 0 and kdim % tk == 0
    if epilogue is None:
        epilogue = lambda acc: acc
    assert nk == 1 or (nk >= 2 and out_dtype == F32), "split K accumulates in the f32 output block"
    return pl.pallas_call(
        functools.partial(_mm_kernel, nk=nk, n_extra=len(extras), epilogue=epilogue),
        grid=(m // tm, n // tn, nk),
        in_specs=[pl.BlockSpec((tm, tk), lambda i, j, k: (i, k)),
                  pl.BlockSpec((tk, tn), lambda i, j, k: (k, j)),
                  *extra_specs],
        out_specs=pl.BlockSpec((tm, tn), lambda i, j, k: (i, j)),
        out_shape=jax.ShapeDtypeStruct((m, n), out_dtype),
        compiler_params=_cparams("parallel", "parallel", "arbitrary"),
        name=name,
    )(a, w, *extras)


def _tile_spec(tm, tn):
    return pl.BlockSpec((tm, tn), lambda i, j, k: (i, j))


def _batch_row_spec(tn, per_b):
    return pl.BlockSpec((1, 1, tn), lambda i, j, k: (i // per_b, 0, j))


def _mm2_kernel(a1_ref, w1_ref, a2_ref, w2_ref, *rest, epilogue):
    r1 = _dot(a1_ref[...], w1_ref[...])
    r2 = _dot(a2_ref[...], w2_ref[...])
    *extra, o_ref = rest
    o_ref[...] = epilogue(r1, r2, *extra).astype(o_ref.dtype)


def _matmul2(name, a1, w1, a2, w2, out_dtype, *, tm, tn, epilogue, extras=(), extra_specs=()):
    m, k1 = a1.shape
    k2 = a2.shape[1]
    n = w1.shape[1]
    assert m % tm == 0 and n % tn == 0 and w2.shape[1] == n
    return pl.pallas_call(
        functools.partial(_mm2_kernel, epilogue=epilogue),
        grid=(m // tm, n // tn),
        in_specs=[pl.BlockSpec((tm, k1), lambda i, j: (i, 0)),
                  pl.BlockSpec((k1, tn), lambda i, j: (0, j)),
                  pl.BlockSpec((tm, k2), lambda i, j: (i, 0)),
                  pl.BlockSpec((k2, tn), lambda i, j: (0, j)),
                  *extra_specs],
        out_specs=pl.BlockSpec((tm, tn), lambda i, j: (i, j)),
        out_shape=jax.ShapeDtypeStruct((m, n), out_dtype),
        compiler_params=_cparams("parallel", "parallel"),
        name=name,
    )(a1, w1, a2, w2, *extras)


def _ffn_in_kernel(x_ref, g_ref, sc_ref, sh_ref, wg_ref, wu_ref, o_ref, h_s):
    @pl.when(pl.program_id(1) == 0)
    def _():
        _norm_rows(x_ref, g_ref, sc_ref, sh_ref, h_s)

    h = h_s[...]
    rg = _dot(h, wg_ref[...].astype(BF16))
    ru = _dot(h, wu_ref[...].astype(BF16))
    o_ref[...] = (_silu(rg) * ru).astype(o_ref.dtype)


def _ffn_in(x, g, sc, sh, w_gate, w_up, seq, *, tm, tn):
    t, d = x.shape
    f = w_gate.shape[1]
    assert t % tm == 0 and f % tn == 0 and seq % tm == 0
    per_b = seq // tm
    return pl.pallas_call(
        _ffn_in_kernel,
        grid=(t // tm, f // tn),
        in_specs=[pl.BlockSpec((tm, d), lambda i, j: (i, 0)),
                  pl.BlockSpec((1, d), lambda i, j: (0, 0)),
                  pl.BlockSpec((1, 1, d), lambda i, j: (i // per_b, 0, 0)),
                  pl.BlockSpec((1, 1, d), lambda i, j: (i // per_b, 0, 0)),
                  pl.BlockSpec((d, tn), lambda i, j: (0, j)),
                  pl.BlockSpec((d, tn), lambda i, j: (0, j))],
        out_specs=pl.BlockSpec((tm, tn), lambda i, j: (i, j)),
        out_shape=jax.ShapeDtypeStruct((t, f), BF16),
        scratch_shapes=[pltpu.VMEM((tm, d), BF16)],
        compiler_params=_cparams("parallel", "arbitrary"),
        name="ffn_norm_gate_up",
    )(x, g, sc, sh, w_gate, w_up)


def _prep_kernel(small_ref, cqg_ref, ckvg_ref, kg_ref, kb_ref, wuq_ref, wiq_ref, wukt_ref,
                 ql_ref, qi_ref, wt_ref, ckv_ref, kidx_ref, *, tm):
    nqb = tm // Q_BLOCK
    cq = small_ref[:, 0:Q_RANK]
    ms = jnp.mean(cq * cq, axis=-1, keepdims=True)
    cqn = (cq * lax.rsqrt(ms + EPS) * cqg_ref[...]).astype(BF16)

    q = _dot(cqn, wuq_ref[...]).astype(BF16)
    scale = HEAD_DIM_A ** -0.5 * LOG2E
    for h in range(N_HEADS_A):
        qlh = (_dot(q[:, h * HEAD_DIM_A:(h + 1) * HEAD_DIM_A], wukt_ref[h]) * scale).astype(BF16)
        for b in range(nqb):
            ql_ref[b, h * Q_BLOCK:(h + 1) * Q_BLOCK, :] = qlh[b * Q_BLOCK:(b + 1) * Q_BLOCK, :]

    qix = _dot(cqn, wiq_ref[...]).astype(BF16)
    for h in range(N_HEADS_IDX):
        for b in range(nqb):
            qi_ref[b, h * Q_BLOCK:(h + 1) * Q_BLOCK, :] = (
                qix[b * Q_BLOCK:(b + 1) * Q_BLOCK, h * LANES:(h + 1) * LANES])

    ckv = small_ref[:, Q_RANK:Q_RANK + KV_RANK]
    ms = jnp.mean(ckv * ckv, axis=-1, keepdims=True)
    ckv_ref[...] = (ckv * lax.rsqrt(ms + EPS) * ckvg_ref[...]).astype(BF16)

    slab = small_ref[:, Q_RANK + KV_RANK:Q_RANK + KV_RANK + LANES]
    lane = lax.broadcasted_iota(I32, slab.shape, 1)
    kmask = lane < HEAD_DIM_IDX
    mu = jnp.sum(jnp.where(kmask, slab, 0.0), axis=-1, keepdims=True) * (1.0 / HEAD_DIM_IDX)
    xc = jnp.where(kmask, slab - mu, 0.0)
    var = jnp.sum(xc * xc, axis=-1, keepdims=True) * (1.0 / HEAD_DIM_IDX)
    y = xc * lax.rsqrt(var + EPS) * kg_ref[...] + kb_ref[...]
    kidx_ref[...] = jnp.where(kmask, y, 0.0).astype(BF16)

    wscale = N_HEADS_IDX ** -0.5 * HEAD_DIM_IDX ** -0.5
    slab_t = slab.T
    wt = slab_t[HEAD_DIM_IDX:HEAD_DIM_IDX + N_HEADS_IDX, :] * wscale
    for b in range(nqb):
        wt_ref[b] = wt[:, b * Q_BLOCK:(b + 1) * Q_BLOCK]


def _prep(small, cqg, ckvg, kg, kb, wuq, wiq, wukt):
    t = small.shape[0]
    tm = 512
    nqb = tm // Q_BLOCK
    nq = t // Q_BLOCK
    hq = N_HEADS_A * Q_BLOCK
    const2 = lambda i: (0, 0)
    return pl.pallas_call(
        functools.partial(_prep_kernel, tm=tm),
        grid=(t // tm,),
        in_specs=[pl.BlockSpec((tm, small.shape[1]), lambda i: (i, 0)),
                  pl.BlockSpec(cqg.shape, const2),
                  pl.BlockSpec(ckvg.shape, const2),
                  pl.BlockSpec(kg.shape, const2),
                  pl.BlockSpec(kb.shape, const2),
                  pl.BlockSpec(wuq.shape, const2),
                  pl.BlockSpec(wiq.shape, const2),
                  pl.BlockSpec(wukt.shape, lambda i: (0, 0, 0))],
        out_specs=[pl.BlockSpec((nqb, hq, KV_RANK), lambda i: (i, 0, 0)),
                   pl.BlockSpec((nqb, hq, LANES), lambda i: (i, 0, 0)),
                   pl.BlockSpec((nqb, N_HEADS_IDX, Q_BLOCK), lambda i: (i, 0, 0)),
                   pl.BlockSpec((tm, KV_RANK), lambda i: (i, 0)),
                   pl.BlockSpec((tm, LANES), lambda i: (i, 0))],
        out_shape=[jax.ShapeDtypeStruct((nq, hq, KV_RANK), BF16),
                   jax.ShapeDtypeStruct((nq, hq, LANES), BF16),
                   jax.ShapeDtypeStruct((nq, N_HEADS_IDX, Q_BLOCK), F32),
                   jax.ShapeDtypeStruct((t, KV_RANK), BF16),
                   jax.ShapeDtypeStruct((t, LANES), BF16)],
        compiler_params=_cparams("parallel"),
        name="dsa_prep",
    )(small, cqg, ckvg, kg, kb, wuq, wiq, wukt)


def _bucket_tiles():
    t = np.arange(Q_BLOCK)[:, None]
    s = np.arange(Q_BLOCK)[None, :]
    max_exact = N_BUCKETS // 2
    tiles = []
    for d in range(3):
        n = np.maximum(t - s + d * Q_BLOCK, 0)
        nf = np.maximum(n, 1).astype(np.float32)
        large = max_exact + (np.log(nf / max_exact) / math.log(MAX_DISTANCE / max_exact)
                             * (N_BUCKETS - max_exact)).astype(np.int32)
        large = np.minimum(large, N_BUCKETS - 1)
        tiles.append(np.where(n < max_exact, n, large))
    assert 2 * Q_BLOCK - (Q_BLOCK - 1) >= MAX_DISTANCE
    tiles[2] = np.full_like(tiles[2], N_BUCKETS - 1)
    return np.stack(tiles).astype(np.int32)


def _bias_kernel(rel_ref, bucket_ref, o_ref):
    h = pl.program_id(0)
    bucket = bucket_ref[...]
    out = jnp.zeros(bucket.shape, F32)
    for b in range(N_BUCKETS):
        out = jnp.where(bucket == b, rel_ref[b, h], out)
    o_ref[:, 0] = (out - rel_ref[N_BUCKETS - 1, h]) * LOG2E


def _bias_tiles(rel_bias):
    buckets = jnp.asarray(_bucket_tiles())
    return pl.pallas_call(
        _bias_kernel,
        grid=(N_HEADS_A,),
        in_specs=[pl.BlockSpec(memory_space=pltpu.SMEM),
                  pl.BlockSpec((3, Q_BLOCK, Q_BLOCK), lambda h: (0, 0, 0))],
        out_specs=pl.BlockSpec((3, 1, Q_BLOCK, Q_BLOCK), lambda h: (0, h, 0, 0)),
        out_shape=jax.ShapeDtypeStruct((3, N_HEADS_A, Q_BLOCK, Q_BLOCK), F32),
        compiler_params=_cparams("arbitrary"),
        name="rel_bias_tiles",
    )(rel_bias, buckets)


def _attn_kernel(qi_ref, wt_ref, kidx_ref, ckv_ref, ql_ref, bt_ref, wuv_ref, ya_ref,
                 key_s, hi16_s, lo16_s, c16_s, mb_s, st_s, x_buf, p_buf, al_buf, acc_s, m_s, lp_s, j_s,
                 *, seq, top_k):
    ck = KEY_CHUNK
    sub = ck // Q_BLOCK
    i = pl.program_id(1)
    nchunk = (i + sub) // sub
    t_row = i * Q_BLOCK + lax.broadcasted_iota(I32, (1, Q_BLOCK), 1)
    qi = qi_ref[0]

    def score_matmul(c, slot):
        k0 = pl.multiple_of(c * ck, ck)
        st_s[slot] = _dot_nt(kidx_ref[0, pl.ds(k0, ck), :], qi)

    def score_reduce(c, slot):
        k0 = pl.multiple_of(c * ck, ck)
        for u in range(sub):
            acc = jnp.zeros((Q_BLOCK, Q_BLOCK), F32)
            for h in range(N_HEADS_IDX):
                sh = st_s[slot, u * Q_BLOCK:(u + 1) * Q_BLOCK, h * Q_BLOCK:(h + 1) * Q_BLOCK]
                acc = acc + jnp.maximum(sh, 0.0) * wt_ref[0, h:h + 1, :]
            s_pos = k0 + u * Q_BLOCK + lax.broadcasted_iota(I32, (Q_BLOCK, Q_BLOCK), 0)
            acc = jnp.where(s_pos <= t_row, acc, -jnp.inf)
            bits = lax.bitcast_convert_type(acc, I32)
            key = jnp.where(bits < 0, bits ^ 0x7FFFFFFF, bits)
            urows = slice(u * Q_BLOCK, (u + 1) * Q_BLOCK)
            key_s[c, urows, :] = key
            hi16_s[c, urows, :] = (key >> 16).astype(I16)
            lo16_s[c, urows, :] = ((key & 0xFFFF) - 32768).astype(I16)

    odd = nchunk % 2

    @pl.when(odd == 1)
    def _():
        score_matmul(0, 0)
        score_reduce(0, 0)

    def score_pair(p, carry):
        c = odd + 2 * p
        score_matmul(c, 0)
        score_matmul(c + 1, 1)
        score_reduce(c, 0)
        score_reduce(c + 1, 1)
        return carry

    lax.fori_loop(0, nchunk // 2, score_pair, 0)

    def count(pred):
        def body(c, cnt):
            tile = key_s[c]
            s_pos = c * ck + lax.broadcasted_iota(I32, (ck, Q_BLOCK), 0)
            hit = jnp.where(pred(tile, s_pos), 1, 0).astype(I32)
            return cnt + jnp.sum(hit.reshape(ck // SUBLANES, SUBLANES, Q_BLOCK), axis=0)
        cnt = lax.fori_loop(0, nchunk, body, jnp.zeros((SUBLANES, Q_BLOCK), I32))
        return jnp.sum(cnt, axis=0, keepdims=True)

    pack_rows = 2 * SUBLANES
    one16 = jnp.ones((pack_rows, Q_BLOCK), I16)
    zero16 = jnp.zeros((pack_rows, Q_BLOCK), I16)

    def pack16(v):
        return jnp.broadcast_to(v, (pack_rows, Q_BLOCK)).astype(I16)

    def bisect16_static(arr_s, n, n_lo0):
        def step(_, state):
            lo, hi, n_lo = state
            mid = (lo + hi) >> 1
            thr16 = pack16(mid)
            hits = [jnp.where(arr_s[c, r * pack_rows:(r + 1) * pack_rows, :] >= thr16, one16, zero16)
                    for c in range(n) for r in range(ck // pack_rows)]
            while len(hits) > 1:
                hits = [hits[k] + hits[k + 1] for k in range(0, len(hits) - 1, 2)] + hits[len(hits) & ~1:]
            n_mid = jnp.sum(hits[0].astype(I32), axis=0, keepdims=True)
            ok = n_mid >= top_k
            return jnp.where(ok, mid, lo), jnp.where(ok, hi, mid), jnp.where(ok, n_mid, n_lo)

        lo, _, n_lo = lax.fori_loop(
            0, 16, step, (jnp.full((1, Q_BLOCK), -2**15, I32), jnp.full((1, Q_BLOCK), 2**15, I32), n_lo0))
        return lo, n_lo

    def bisect16(arr_s, n_lo0):
        branches = [functools.partial(bisect16_static, arr_s, n) for n in range(1, seq // ck + 1)]
        return lax.switch(nchunk - 1, branches, n_lo0)

    prefix, n_prefix = bisect16(hi16_s, jnp.broadcast_to(nchunk * ck, (1, Q_BLOCK)).astype(I32))
    p16 = pack16(prefix)[None]

    def low_chunk(c, carry):
        hi = hi16_s[c].reshape(ck // pack_rows, pack_rows, Q_BLOCK)
        lo = lo16_s[c].reshape(ck // pack_rows, pack_rows, Q_BLOCK)
        c16 = jnp.where(hi > p16, jnp.full((), 2**15 - 1, I16), jnp.where(hi == p16, lo, jnp.full((), -2**15, I16)))
        c16_s[c] = c16.reshape(ck, Q_BLOCK)
        return carry

    lax.fori_loop(0, nchunk, low_chunk, 0)
    low, n_ge = bisect16(c16_s, n_prefix)
    thr = prefix * 65536 + (low + 32768)
    tie = (n_ge > top_k) & (thr > NEG_INF_KEY)
    j_s[...] = jnp.full(j_s.shape, seq, I32)

    @pl.when(jnp.max(jnp.where(tie, 1, 0)) > 0)
    def _():
        need = top_k - count(lambda tile, s_pos: tile > thr)

        def jbisect(_, lohi):
            lo, hi = lohi
            mid = (lo + hi) >> 1
            ok = count(lambda tile, s_pos: (tile == thr) & (s_pos <= mid)) >= need
            return jnp.where(ok, lo, mid), jnp.where(ok, mid, hi)
        lo_j = jnp.full((1, Q_BLOCK), -1, I32)
        hi_j = jnp.full((1, Q_BLOCK), seq - 1, I32)
        _, cut = lax.fori_loop(0, max(1, (seq - 1).bit_length()) + 1, jbisect, (lo_j, hi_j))
        j_s[...] = jnp.broadcast_to(jnp.where(tie, cut, seq), j_s.shape)

    jcut = j_s[0:1, :]

    def mask_chunk(c):
        tile = key_s[c]
        s_pos = c * ck + lax.broadcasted_iota(I32, (ck, Q_BLOCK), 0)
        sel = (tile > thr) | ((tile == thr) & (s_pos <= jcut))
        sel = sel & (s_pos <= t_row)
        mbt = jnp.where(sel, 0.0, NEG)
        for u in range(sub):
            mb_s[c * sub + u] = mbt[u * Q_BLOCK:(u + 1) * Q_BLOCK, :].T

    @pl.when(odd == 1)
    def _():
        mask_chunk(0)

    def mask_pair(p, carry):
        mask_chunk(odd + 2 * p)
        mask_chunk(odd + 2 * p + 1)
        return carry

    lax.fori_loop(0, nchunk // 2, mask_pair, 0)
    neg_tile = seq // Q_BLOCK
    mb_s[neg_tile] = jnp.full((Q_BLOCK, Q_BLOCK), NEG, F32)

    m_s[...] = jnp.full(m_s.shape, -jnp.inf, F32)
    lp_s[...] = jnp.zeros(lp_s.shape, F32)
    acc_s[...] = jnp.zeros(acc_s.shape, F32)

    ngroup = N_HEADS_A // HEAD_GROUP
    grows_n = HEAD_GROUP * Q_BLOCK
    nparts = grows_n // SM_ROWS
    first_start = i + 1 - sub * nchunk

    def chunk_start(c):
        return jnp.maximum(first_start + c * sub, 0)

    def load_kv(c):
        k0 = pl.multiple_of(chunk_start(c) * Q_BLOCK, Q_BLOCK)
        return ckv_ref[0, pl.ds(k0, ck), :]

    def part_rows(g, part):
        h = g * HEAD_GROUP + part * SM_ROWS // Q_BLOCK
        q0 = (part * SM_ROWS) % Q_BLOCK
        rows = slice(h * Q_BLOCK + q0, h * Q_BLOCK + q0 + SM_ROWS)
        return h, q0, rows, slice(part * SM_ROWS, (part + 1) * SM_ROWS)

    def stage_a(c, g, near):
        start = chunk_start(c)
        shared = (c == 0) & (nchunk > 1)
        tiles = []
        for u in range(sub):
            blk = start + u
            tiles.append(jnp.where(shared & (blk >= first_start + sub), neg_tile, blk))
        s = _dot_nt(ql_ref[0, g * grows_n:(g + 1) * grows_n, :], load_kv(c))
        for part in range(nparts):
            h, q0, rows, lrows = part_rows(g, part)
            mx = None
            for u in range(sub):
                cols = slice(u * Q_BLOCK, (u + 1) * Q_BLOCK)
                x = s[lrows, cols] + mb_s[tiles[u], q0:q0 + SM_ROWS, :]
                if near:
                    x = x + bt_ref[jnp.clip(i - (start + u), 0, 2), h, q0:q0 + SM_ROWS, :]
                x_buf[g, lrows, cols] = x
                mx = x if mx is None else jnp.maximum(mx, x)
            m_prev = m_s[rows, :]
            m_cur = jnp.broadcast_to(jnp.max(mx, axis=1, keepdims=True), (SM_ROWS, LANES))
            m_new = jnp.maximum(m_prev, m_cur)
            al_buf[g, lrows, :] = jnp.exp2(m_prev - m_new)
            m_s[rows, :] = m_new

    def stage_b(g):
        for part in range(nparts):
            _, _, rows, lrows = part_rows(g, part)
            m_new = m_s[rows, :]
            ps = [jnp.exp2(x_buf[g, lrows, u * Q_BLOCK:(u + 1) * Q_BLOCK] - m_new) for u in range(sub)]
            lp_s[rows, :] = al_buf[g, lrows, :] * lp_s[rows, :] + ((ps[0] + ps[1]) + (ps[2] + ps[3]))
            p_buf[g, lrows, :] = jnp.concatenate(ps, axis=1).astype(BF16)

    def stage_c(c, g):
        grows = slice(g * grows_n, (g + 1) * grows_n)
        alpha = jnp.concatenate([al_buf[g]] * (KV_RANK // LANES), axis=1)
        acc_s[grows, :] = alpha * acc_s[grows, :] + _dot(p_buf[g], load_kv(c))

    assert sub == 4 and ngroup >= 2

    def chunk_steps(c, near, first):
        for k in range(ngroup):
            stage_a(c, k, near)
            if k >= 1 or not first:
                stage_b((k - 1) % ngroup)
            if k >= 2:
                stage_c(c, k - 2)
            elif not first:
                stage_c(c - 1, k - 2 + ngroup)

    def drain(c):
        stage_b(ngroup - 1)
        stage_c(c, ngroup - 2)
        stage_c(c, ngroup - 1)

    last = nchunk - 1

    @pl.when(nchunk == 1)
    def _():
        chunk_steps(last, True, True)
        drain(last)

    @pl.when(nchunk > 1)
    def _():
        chunk_steps(0, False, True)

        def far_body(c, carry):
            chunk_steps(c, False, False)
            return carry

        lax.fori_loop(1, last, far_body, 0)
        chunk_steps(last, True, False)
        drain(last)

    for h in range(N_HEADS_A):
        rows = slice(h * Q_BLOCK, (h + 1) * Q_BLOCK)
        l = jnp.sum(lp_s[rows, :], axis=1, keepdims=True)
        o = (acc_s[rows, :] / l).astype(BF16)
        ya_ref[0, :, h * HEAD_DIM_A:(h + 1) * HEAD_DIM_A] = _dot(o, wuv_ref[h]).astype(ya_ref.dtype)


def _attention(qi, wt, kidx, ckv, ql, bt, wuv, batch, seq):
    nq = seq // Q_BLOCK
    hq = N_HEADS_A * Q_BLOCK
    ck = KEY_CHUNK
    grp = HEAD_GROUP * Q_BLOCK
    ngrp = N_HEADS_A // HEAD_GROUP
    assert seq % ck == 0
    top_k = min(TOP_K_MAX, seq // 4)
    kidx3 = kidx.reshape(batch, seq, LANES)
    ckv3 = ckv.reshape(batch, seq, KV_RANK)
    return pl.pallas_call(
        functools.partial(_attn_kernel, seq=seq, top_k=top_k),
        grid=(batch, nq),
        in_specs=[pl.BlockSpec((1, hq, LANES), lambda b, i: (b * nq + i, 0, 0)),
                  pl.BlockSpec((1, N_HEADS_IDX, Q_BLOCK), lambda b, i: (b * nq + i, 0, 0)),
                  pl.BlockSpec((1, seq, LANES), lambda b, i: (b, 0, 0)),
                  pl.BlockSpec((1, seq, KV_RANK), lambda b, i: (b, 0, 0)),
                  pl.BlockSpec((1, hq, KV_RANK), lambda b, i: (b * nq + i, 0, 0)),
                  pl.BlockSpec(bt.shape, lambda b, i: (0, 0, 0, 0)),
                  pl.BlockSpec(wuv.shape, lambda b, i: (0, 0, 0))],
        out_specs=pl.BlockSpec((1, Q_BLOCK, N_HEADS_A * HEAD_DIM_A), lambda b, i: (b, i, 0)),
        out_shape=jax.ShapeDtypeStruct((batch, seq, N_HEADS_A * HEAD_DIM_A), BF16),
        scratch_shapes=[pltpu.VMEM((seq // ck, ck, Q_BLOCK), I32),
                        pltpu.VMEM((seq // ck, ck, Q_BLOCK), I16),
                        pltpu.VMEM((seq // ck, ck, Q_BLOCK), I16),
                        pltpu.VMEM((seq // ck, ck, Q_BLOCK), I16),
                        pltpu.VMEM((nq + 1, Q_BLOCK, Q_BLOCK), F32),
                        pltpu.VMEM((2, ck, hq), F32),
                        pltpu.VMEM((ngrp, grp, ck), F32),
                        pltpu.VMEM((ngrp, grp, ck), BF16),
                        pltpu.VMEM((ngrp, grp, LANES), F32),
                        pltpu.VMEM((hq, KV_RANK), F32),
                        pltpu.VMEM((hq, LANES), F32),
                        pltpu.VMEM((hq, LANES), F32),
                        pltpu.VMEM((SUBLANES, Q_BLOCK), I32)],
        compiler_params=_cparams("parallel", "arbitrary"),
        name="dsa_attention",
    )(qi, wt, kidx3, ckv3, ql, bt, wuv)


def _xbc_conv_kernel(a_ref, w_ref, cw_ref, cb_ref, o_ref, tail_s, ya_s, yb_s, *, tm, per_b):
    i, j = pl.program_id(0), pl.program_id(1)

    @pl.when(i % per_b == 0)
    def _():
        tail_s[j] = jnp.zeros(tail_s.shape[1:], F32)

    nstrip = o_ref.shape[1] // CONV_STRIP
    y_bufs = (ya_s, yb_s)
    row0 = pl.multiple_of(jnp.minimum(i, 0) * SUBLANES, SUBLANES)

    def matmul_strip(s):
        cols = slice(s * CONV_STRIP, (s + 1) * CONV_STRIP)
        buf = y_bufs[s % 2]
        buf[0:SUBLANES, :] = tail_s[j, :, cols]
        buf[SUBLANES:SUBLANES + tm, :] = _dot(a_ref[...], w_ref[:, cols])

    def conv_strip(s):
        cols = slice(s * CONV_STRIP, (s + 1) * CONV_STRIP)
        buf = y_bufs[s % 2]
        tail_s[j, :, cols] = buf[pl.ds(row0 + tm, SUBLANES), :]
        for r in range(0, tm, CONV_ROWS):
            ext = buf[pl.ds(row0 + r, CONV_ROWS + SUBLANES), :]
            acc = ext * cw_ref[0:1, cols]
            for k in range(1, CONV_WIDTH):
                acc = pltpu.roll(acc, 1, axis=0) + ext * cw_ref[k:k + 1, cols]
            o_ref[r:r + CONV_ROWS, cols] = _silu(acc[SUBLANES:, :] + cb_ref[:, cols]).astype(o_ref.dtype)

    for s in range(nstrip):
        matmul_strip(s)
        if s >= 1:
            conv_strip(s - 1)
    conv_strip(nstrip - 1)


def _xbc_conv(a, w, conv_w, conv_b, seq, *, tm, tn):
    m, kdim = a.shape
    n = w.shape[1]
    assert m % tm == 0 and n % tn == 0 and seq % tm == 0
    return pl.pallas_call(
        functools.partial(_xbc_conv_kernel, tm=tm, per_b=seq // tm),
        grid=(m // tm, n // tn),
        in_specs=[pl.BlockSpec((tm, kdim), lambda i, j: (i, 0)),
                  pl.BlockSpec((kdim, tn), lambda i, j: (0, j)),
                  pl.BlockSpec((CONV_WIDTH, tn), lambda i, j: (0, j)),
                  pl.BlockSpec((1, tn), lambda i, j: (0, j))],
        out_specs=pl.BlockSpec((tm, tn), lambda i, j: (i, j)),
        out_shape=jax.ShapeDtypeStruct((m, n), BF16),
        scratch_shapes=[pltpu.VMEM((n // tn, SUBLANES, tn), F32),
                        pltpu.VMEM((SUBLANES + tm, CONV_STRIP), F32),
                        pltpu.VMEM((SUBLANES + tm, CONV_STRIP), F32)],
        compiler_params=_cparams("arbitrary", "arbitrary"),
        name="in_xbc_conv",
    )(a, w, conv_w, conv_b)


def _ssd_kernel(xc_ref, zs_ref, dt_ref, dtb_ref, aneg_ref, dskip_ref, ng_ref, e_ref, y_ref,
                state_s, de_s, *, d_inner):
    c = pl.program_id(1)
    hg = d_inner // SSM_GROUPS
    heads_g = hg // SSM_HEAD_DIM
    b_off = d_inner
    c_off = d_inner + SSM_GROUPS * D_STATE

    @pl.when(c == 0)
    def _():
        state_s[...] = jnp.zeros_like(state_s)

    x = dt_ref[0] + dtb_ref[...]
    dt = jnp.maximum(x, 0.0) + jnp.log1p(jnp.exp(-jnp.abs(x)))
    da = dt * aneg_ref[...]
    tri_i = lax.broadcasted_iota(I32, (CHUNK, CHUNK), 0) >= lax.broadcasted_iota(I32, (CHUNK, CHUNK), 1)
    tri = jnp.where(tri_i, 1.0, 0.0).astype(BF16)
    acs = sum(_dot(tri, part) for part in _split3(da))
    acs_t = acs.T
    row_t = acs_t - jnp.log(dt.T)
    acs_last = acs[CHUNK - 1:CHUNK, :]
    e = e_ref[...]
    de_s[...] = sum(_dot(part, e) for part in _split3(jnp.exp(acs_last - acs) * dt))
    el = jnp.broadcast_to(jnp.exp(acs_last), (SUBLANES, LANES))
    el_x = sum(_dot(part, e) for part in _split3(el))[0:1, :]
    lane = lax.broadcasted_iota(I32, (CHUNK, LANES), 1)

    for g in range(SSM_GROUPS):
        cols = slice(g * hg, (g + 1) * hg)
        xg = xc_ref[0, :, cols]
        bg = xc_ref[0, :, b_off + g * D_STATE:b_off + (g + 1) * D_STATE]
        cg = xc_ref[0, :, c_off + g * D_STATE:c_off + (g + 1) * D_STATE]
        cb = _dot_nt(cg, bg)
        cg_f = cg.astype(F32)
        sg = state_s[g]
        sg_b = sg.astype(BF16)
        tiles = []
        for pr in range(heads_g // 2):
            pcols = slice(pr * LANES, (pr + 1) * LANES)
            rhs = jnp.concatenate([xg[:, pcols], sg_b[:, pcols]], axis=0)
            outs = []
            for q in range(2):
                h = g * heads_g + pr * 2 + q
                col = jnp.broadcast_to(acs[:, h:h + 1], (CHUNK, CHUNK))
                w = cb * jnp.exp(jnp.where(tri_i, col - row_t[h:h + 1, :], NEG))
                lhs = jnp.concatenate([w.astype(BF16), (cg_f * jnp.exp(col)).astype(BF16)], axis=1)
                outs.append(_dot(lhs, rhs))
            tiles.append(jnp.where(lane < SSM_HEAD_DIM, outs[0], outs[1]))
        xg_f = xg.astype(F32)
        y = jnp.concatenate(tiles, axis=1) + dskip_ref[:, cols] * xg_f
        xd = (xg_f * de_s[:, cols]).astype(BF16)
        state_s[g] = sg * el_x[:, cols] + _dot(bg.astype(F32).T.astype(BF16), xd)
        yz = y * zs_ref[0, :, cols].astype(F32)
        ms = jnp.mean(yz * yz, axis=-1, keepdims=True)
        y_ref[0, :, cols] = (yz * lax.rsqrt(ms + EPS) * ng_ref[:, cols]).astype(y_ref.dtype)


def _ssd(xc, zs, small3, dtb, aneg, dskip_x, ng, expand, dt_block):
    batch, seq, d_xbc = xc.shape
    d_inner = zs.shape[2]
    nc = seq // CHUNK
    const2 = lambda b, c: (0, 0)
    return pl.pallas_call(
        functools.partial(_ssd_kernel, d_inner=d_inner),
        grid=(batch, nc),
        in_specs=[pl.BlockSpec((1, CHUNK, d_xbc), lambda b, c: (b, c, 0)),
                  pl.BlockSpec((1, CHUNK, d_inner), lambda b, c: (b, c, 0)),
                  pl.BlockSpec((1, CHUNK, LANES), lambda b, c: (b, c, dt_block)),
                  pl.BlockSpec(dtb.shape, const2),
                  pl.BlockSpec(aneg.shape, const2),
                  pl.BlockSpec(dskip_x.shape, const2),
                  pl.BlockSpec(ng.shape, const2),
                  pl.BlockSpec(expand.shape, const2)],
        out_specs=pl.BlockSpec((1, CHUNK, d_inner), lambda b, c: (b, c, 0)),
        out_shape=jax.ShapeDtypeStruct((batch, seq, d_inner), BF16),
        scratch_shapes=[pltpu.VMEM((SSM_GROUPS, D_STATE, d_inner // SSM_GROUPS), F32),
                        pltpu.VMEM((CHUNK, d_inner), F32)],
        compiler_params=_cparams("parallel", "arbitrary"),
        name="ssd",
    )(xc, zs, small3, dtb, aneg, dskip_x, ng, expand)


def _pad_cols(a, n):
    return jnp.pad(a, ((0, 0), (0, n - a.shape[1])))


def _layer(x2, mod, batch, seq, norm1_g, w_in, cq_norm_g, ckv_norm_g, kidx_norm_g, kidx_norm_b, w_uq, w_iq,
           w_uk, w_uv, rel_bias, conv_w, conv_b, dt_bias, a_log, d_skip, ssm_norm_g, w_proj_a, w_proj_b,
           w_out, norm2_g, w_gate, w_up, w_down, final_g):
    t, d = x2.shape
    n_heads_b = dt_bias.shape[0]
    d_inner = n_heads_b * SSM_HEAD_DIM
    d_xbc = d_inner + 2 * SSM_GROUPS * D_STATE
    ha = N_HEADS_A * HEAD_DIM_A
    sh1, sc1, g1, sh2, sc2, g2 = [m.reshape(batch, 1, d) for m in jnp.split(mod, 6, axis=-1)]

    o_z = Q_RANK + KV_RANK + HEAD_DIM_IDX + N_HEADS_IDX
    o_xbc = o_z + d_inner
    o_dt = o_xbc + d_xbc
    o_g = o_dt + n_heads_b
    small_w = 1024
    dt_col = 896
    assert o_z <= dt_col and dt_col + n_heads_b <= small_w and dt_col % LANES == 0
    w_small = jnp.concatenate(
        [_pad_cols(w_in[:, :o_z], dt_col), _pad_cols(w_in[:, o_dt:o_g], small_w - dt_col)], axis=1).astype(BF16)
    w_z = w_in[:, o_z:o_xbc].astype(BF16)
    w_xbc = w_in[:, o_xbc:o_dt].astype(BF16)
    w_g = w_in[:, o_g:].astype(BF16)

    tm = 1024
    per_b = seq // tm
    small, h = _norm_proj(x2, norm1_g.reshape(1, d), sc1, sh1, w_small, seq, tm=tm)
    zs = _matmul("in_z", h, w_z, BF16, tm=tm, tn=2048, epilogue=lambda acc: _silu(acc))
    xc = _xbc_conv(h, w_xbc, conv_w, conv_b.reshape(1, d_xbc), seq, tm=tm, tn=1024)
    sg = _matmul("in_gates", h, w_g, BF16, tm=tm, tn=2048, epilogue=lambda acc: jax.nn.sigmoid(acc))

    w_iq_p = jnp.pad(w_iq.reshape(Q_RANK, N_HEADS_IDX, HEAD_DIM_IDX),
                     ((0, 0), (0, 0), (0, LANES - HEAD_DIM_IDX))).reshape(Q_RANK, N_HEADS_IDX * LANES).astype(BF16)
    w_uk_t = jnp.transpose(w_uk.reshape(KV_RANK, N_HEADS_A, HEAD_DIM_A), (1, 2, 0)).astype(BF16)
    w_uv_h = jnp.transpose(w_uv.reshape(KV_RANK, N_HEADS_A, HEAD_DIM_A), (1, 0, 2)).astype(BF16)
    ql, qi, wt, ckv, kidx = _prep(
        small, cq_norm_g.reshape(1, Q_RANK), ckv_norm_g.reshape(1, KV_RANK),
        _pad_cols(kidx_norm_g.reshape(1, HEAD_DIM_IDX), LANES), _pad_cols(kidx_norm_b.reshape(1, HEAD_DIM_IDX), LANES),
        w_uq.astype(BF16), w_iq_p, w_uk_t)
    bt = _bias_tiles(rel_bias)
    y_a = _attention(qi, wt, kidx, ckv, ql, bt, w_uv_h, batch, seq).reshape(t, ha)

    expand = np.zeros((LANES, d_inner), np.float32)
    for hh in range(n_heads_b):
        expand[hh, hh * SSM_HEAD_DIM:(hh + 1) * SSM_HEAD_DIM] = 1.0
    aneg = _pad_cols(-jnp.exp(a_log.astype(F32)).reshape(1, n_heads_b), LANES)
    y_b = _ssd(xc.reshape(batch, seq, d_xbc), zs.reshape(batch, seq, d_inner), small.reshape(batch, seq, small_w),
               _pad_cols(dt_bias.reshape(1, n_heads_b), LANES), aneg,
               jnp.repeat(d_skip, SSM_HEAD_DIM).reshape(1, d_inner), ssm_norm_g.reshape(1, d_inner),
               jnp.asarray(expand, BF16), dt_col // LANES).reshape(t, d_inner)

    tn = 1024
    tp = 512
    merged = _matmul2("proj_ab", y_a, w_proj_a.astype(BF16), y_b, w_proj_b.astype(BF16), BF16, tm=tm, tn=tp,
                      epilogue=lambda ra, rb, sa_ref, sb_ref: (sa_ref[...].astype(F32) * ra
                                                               + sb_ref[...].astype(F32) * rb),
                      extras=(sg, sg),
                      extra_specs=(pl.BlockSpec((tm, tp), lambda i, j: (i, j)),
                                   pl.BlockSpec((tm, tp), lambda i, j: (i, j + d // tp))))
    x1 = _matmul("w_out", merged, w_out.astype(BF16), F32, tm=tm, tn=tn,
                 epilogue=lambda acc, x_ref, g_ref: x_ref[...] + g_ref[0] * acc,
                 extras=(x2, g1), extra_specs=(_tile_spec(tm, tn), _batch_row_spec(tn, per_b)))

    d_ff = w_gate.shape[1]
    act = _ffn_in(x1, norm2_g.reshape(1, d), sc2, sh2, w_gate, w_up, seq, tm=tm, tn=512)

    def down_epilogue(acc, x_ref, g_ref, f_ref):
        xo = x_ref[...] + g_ref[0] * acc
        ms = jnp.mean(xo * xo, axis=-1, keepdims=True)
        return xo * lax.rsqrt(ms + EPS) * f_ref[...]

    tmd = 1024
    assert d_ff % 4 == 0
    return _matmul("ffn_down", act, w_down.astype(BF16), F32, tm=tmd, tn=d, tk=d_ff // 4,
                   epilogue=down_epilogue, extras=(x1, g2, final_g.reshape(1, d)),
                   extra_specs=(_tile_spec(tmd, d), _batch_row_spec(d, seq // tmd),
                                pl.BlockSpec((1, d), lambda i, j, k: (0, 0))))


def kernel(x, c, w_ada, b_ada, norm1_g, w_in, cq_norm_g, ckv_norm_g, kidx_norm_g, kidx_norm_b, w_uq, w_iq, w_uk, w_uv, rel_bias, conv_w, conv_b, dt_bias, a_log, d_skip, ssm_norm_g, w_proj_a, w_proj_b, w_out, norm2_g, w_gate, w_up, w_down, final_g):
    batch, seq, d = x.shape
    assert w_ada.shape[0] == 1, "single-layer block"
    c8 = jnp.pad(c, ((0, SUBLANES - batch % SUBLANES if batch % SUBLANES else 0), (0, 0)))
    mod = _ada(c8, w_ada[0], b_ada[0].reshape(1, -1))[:batch]
    out = _layer(x.reshape(batch * seq, d), mod, batch, seq, norm1_g[0], w_in[0], cq_norm_g[0], ckv_norm_g[0],
                 kidx_norm_g[0], kidx_norm_b[0], w_uq[0], w_iq[0], w_uk[0], w_uv[0], rel_bias, conv_w[0],
                 conv_b[0], dt_bias[0], a_log[0], d_skip[0], ssm_norm_g[0], w_proj_a[0], w_proj_b[0], w_out[0],
                 norm2_g[0], w_gate[0], w_up[0], w_down[0], final_g)
    return out.reshape(batch, seq, d)
```

```python
import functools
import math

import numpy as np
import jax
import jax.numpy as jnp
from jax import lax
from jax.experimental import pallas as pl
from jax.experimental.pallas import tpu as pltpu

F32 = jnp.float32
BF16 = jnp.bfloat16
I32 = jnp.int32
I16 = jnp.int16

EPS = 1e-6
LANES = 128
SUBLANES = 8
VMEM_LIMIT = 56 * 2**20

N_HEADS_A = 16
HEAD_DIM_A = 128
Q_RANK = 512
KV_RANK = 256
N_HEADS_IDX = 16
HEAD_DIM_IDX = 64
TOP_K_MAX = 256
Q_BLOCK = 128
KEY_CHUNK = 512
HEAD_GROUP = 4
SM_ROWS = 64
LOG2E = 1.4426950408889634
N_BUCKETS = 32
MAX_DISTANCE = 128
SSM_HEAD_DIM = 64
SSM_GROUPS = 8
D_STATE = 128
NORM_ROWS = 32
CONV_WIDTH = 4
CONV_STRIP = 256
CONV_ROWS = 64
CHUNK = 128
NEG = -1e30
NEG_INF_KEY = -2139095041


def _cparams(*sem):
    return pltpu.CompilerParams(dimension_semantics=sem, vmem_limit_bytes=VMEM_LIMIT)


def _silu(x):
    return x * jax.nn.sigmoid(x)


def _split3(v):
    hi = v.astype(BF16)
    r1 = v - hi.astype(F32)
    mid = r1.astype(BF16)
    lo = (r1 - mid.astype(F32)).astype(BF16)
    return hi, mid, lo


def _dot(a, b):
    return jnp.dot(a, b, preferred_element_type=F32)


def _dot_nt(a, b):
    return lax.dot_general(a, b, (((1,), (1,)), ((), ())), preferred_element_type=F32)


def _ada_kernel(c_ref, w_ref, b_ref, o_ref):
    a = _silu(c_ref[...])
    o_ref[...] = _dot(a, w_ref[...]) + b_ref[...]


def _ada(c8, w, b):
    m, d = c8.shape
    n = w.shape[1]
    tn = 1024
    return pl.pallas_call(
        _ada_kernel,
        grid=(n // tn,),
        in_specs=[pl.BlockSpec((m, d), lambda j: (0, 0)),
                  pl.BlockSpec((d, tn), lambda j: (0, j)),
                  pl.BlockSpec((1, tn), lambda j: (0, j))],
        out_specs=pl.BlockSpec((m, tn), lambda j: (0, j)),
        out_shape=jax.ShapeDtypeStruct((m, n), F32),
        compiler_params=_cparams("parallel"),
        name="ada",
    )(c8, w, b)


def _norm_rows(x_ref, g_ref, sc_ref, sh_ref, h_ref):
    gain = g_ref[...] * (1.0 + sc_ref[0])

    def rows(r, carry):
        sl = pl.ds(pl.multiple_of(r * NORM_ROWS, NORM_ROWS), NORM_ROWS)
        x = x_ref[sl, :]
        ms = jnp.mean(x * x, axis=-1, keepdims=True)
        h_ref[sl, :] = (x * lax.rsqrt(ms + EPS) * gain + sh_ref[0]).astype(h_ref.dtype)
        return carry

    lax.fori_loop(0, x_ref.shape[0] // NORM_ROWS, rows, 0, unroll=4)


def _norm_proj_kernel(x_ref, g_ref, sc_ref, sh_ref, w_ref, o_ref, h_ref):
    _norm_rows(x_ref, g_ref, sc_ref, sh_ref, h_ref)
    o_ref[...] = _dot(h_ref[...], w_ref[...])


def _norm_proj(x2, g, sc, sh, w, seq, *, tm):
    t, d = x2.shape
    n = w.shape[1]
    per_b = seq // tm
    assert t % tm == 0 and seq % tm == 0
    return pl.pallas_call(
        _norm_proj_kernel,
        grid=(t // tm,),
        in_specs=[pl.BlockSpec((tm, d), lambda i: (i, 0)),
                  pl.BlockSpec((1, d), lambda i: (0, 0)),
                  pl.BlockSpec((1, 1, d), lambda i: (i // per_b, 0, 0)),
                  pl.BlockSpec((1, 1, d), lambda i: (i // per_b, 0, 0)),
                  pl.BlockSpec((d, n), lambda i: (0, 0))],
        out_specs=[pl.BlockSpec((tm, n), lambda i: (i, 0)),
                   pl.BlockSpec((tm, d), lambda i: (i, 0))],
        out_shape=[jax.ShapeDtypeStruct((t, n), F32), jax.ShapeDtypeStruct((t, d), BF16)],
        compiler_params=_cparams("parallel"),
        name="norm_in_small",
    )(x2, g, sc, sh, w)


def _mm_kernel(*refs, nk, n_extra, epilogue):
    a_ref, w_ref = refs[0], refs[1]
    extra = refs[2:2 + n_extra]
    o_ref = refs[2 + n_extra]
    if nk == 1:
        acc = _dot(a_ref[...], w_ref[...])
        o_ref[...] = epilogue(acc, *extra).astype(o_ref.dtype)
        return
    k = pl.program_id(2)

    @pl.when(k == 0)
    def _():
        o_ref[...] = _dot(a_ref[...], w_ref[...])

    @pl.when((k > 0) & (k < nk - 1))
    def _():
        o_ref[...] += _dot(a_ref[...], w_ref[...])

    @pl.when(k == nk - 1)
    def _():
        o_ref[...] = epilogue(o_ref[...] + _dot(a_ref[...], w_ref[...]), *extra)


def _matmul(name, a, w, out_dtype, *, tm, tn, tk=None, epilogue=None, extras=(), extra_specs=()):
    m, kdim = a.shape
    n = w.shape[1]
    tk = kdim if tk is None else tk
    nk = kdim // tk
    assert m % tm == 0 and n % tn == 0 and kdim % tk == 0
    if epilogue is None:
        epilogue = lambda acc: acc
    assert nk == 1 or (nk >= 2 and out_dtype == F32), "split K accumulates in the f32 output block"
    return pl.pallas_call(
        functools.partial(_mm_kernel, nk=nk, n_extra=len(extras), epilogue=epilogue),
        grid=(m // tm, n // tn, nk),
        in_specs=[pl.BlockSpec((tm, tk), lambda i, j, k: (i, k)),
                  pl.BlockSpec((tk, tn), lambda i, j, k: (k, j)),
                  *extra_specs],
        out_specs=pl.BlockSpec((tm, tn), lambda i, j, k: (i, j)),
        out_shape=jax.ShapeDtypeStruct((m, n), out_dtype),
        compiler_params=_cparams("parallel", "parallel", "arbitrary"),
        name=name,
    )(a, w, *extras)


def _tile_spec(tm, tn):
    return pl.BlockSpec((tm, tn), lambda i, j, k: (i, j))


def _batch_row_spec(tn, per_b):
    return pl.BlockSpec((1, 1, tn), lambda i, j, k: (i // per_b, 0, j))


def _mm2_kernel(a1_ref, w1_ref, a2_ref, w2_ref, *rest, epilogue):
    r1 = _dot(a1_ref[...], w1_ref[...])
    r2 = _dot(a2_ref[...], w2_ref[...])
    *extra, o_ref = rest
    o_ref[...] = epilogue(r1, r2, *extra).astype(o_ref.dtype)


def _matmul2(name, a1, w1, a2, w2, out_dtype, *, tm, tn, epilogue, extras=(), extra_specs=()):
    m, k1 = a1.shape
    k2 = a2.shape[1]
    n = w1.shape[1]
    assert m % tm == 0 and n % tn == 0 and w2.shape[1] == n
    return pl.pallas_call(
        functools.partial(_mm2_kernel, epilogue=epilogue),
        grid=(m // tm, n // tn),
        in_specs=[pl.BlockSpec((tm, k1), lambda i, j: (i, 0)),
                  pl.BlockSpec((k1, tn), lambda i, j: (0, j)),
                  pl.BlockSpec((tm, k2), lambda i, j: (i, 0)),
                  pl.BlockSpec((k2, tn), lambda i, j: (0, j)),
                  *extra_specs],
        out_specs=pl.BlockSpec((tm, tn), lambda i, j: (i, j)),
        out_shape=jax.ShapeDtypeStruct((m, n), out_dtype),
        compiler_params=_cparams("parallel", "parallel"),
        name=name,
    )(a1, w1, a2, w2, *extras)


def _ffn_in_kernel(x_ref, g_ref, sc_ref, sh_ref, wg_ref, wu_ref, o_ref, h_s):
    @pl.when(pl.program_id(1) == 0)
    def _():
        _norm_rows(x_ref, g_ref, sc_ref, sh_ref, h_s)

    h = h_s[...]
    rg = _dot(h, wg_ref[...].astype(BF16))
    ru = _dot(h, wu_ref[...].astype(BF16))
    o_ref[...] = (_silu(rg) * ru).astype(o_ref.dtype)


def _ffn_in(x, g, sc, sh, w_gate, w_up, seq, *, tm, tn):
    t, d = x.shape
    f = w_gate.shape[1]
    assert t % tm == 0 and f % tn == 0 and seq % tm == 0
    per_b = seq // tm
    return pl.pallas_call(
        _ffn_in_kernel,
        grid=(t // tm, f // tn),
        in_specs=[pl.BlockSpec((tm, d), lambda i, j: (i, 0)),
                  pl.BlockSpec((1, d), lambda i, j: (0, 0)),
                  pl.BlockSpec((1, 1, d), lambda i, j: (i // per_b, 0, 0)),
                  pl.BlockSpec((1, 1, d), lambda i, j: (i // per_b, 0, 0)),
                  pl.BlockSpec((d, tn), lambda i, j: (0, j)),
                  pl.BlockSpec((d, tn), lambda i, j: (0, j))],
        out_specs=pl.BlockSpec((tm, tn), lambda i, j: (i, j)),
        out_shape=jax.ShapeDtypeStruct((t, f), BF16),
        scratch_shapes=[pltpu.VMEM((tm, d), BF16)],
        compiler_params=_cparams("parallel", "arbitrary"),
        name="ffn_norm_gate_up",
    )(x, g, sc, sh, w_gate, w_up)


def _prep_kernel(small_ref, cqg_ref, ckvg_ref, kg_ref, kb_ref, wuq_ref, wiq_ref, wukt_ref,
                 ql_ref, qi_ref, wt_ref, ckv_ref, kidx_ref, *, tm):
    nqb = tm // Q_BLOCK
    cq = small_ref[:, 0:Q_RANK]
    ms = jnp.mean(cq * cq, axis=-1, keepdims=True)
    cqn = (cq * lax.rsqrt(ms + EPS) * cqg_ref[...]).astype(BF16)

    q = _dot(cqn, wuq_ref[...]).astype(BF16)
    scale = HEAD_DIM_A ** -0.5 * LOG2E
    for h in range(N_HEADS_A):
        qlh = (_dot(q[:, h * HEAD_DIM_A:(h + 1) * HEAD_DIM_A], wukt_ref[h]) * scale).astype(BF16)
        for b in range(nqb):
            ql_ref[b, h * Q_BLOCK:(h + 1) * Q_BLOCK, :] = qlh[b * Q_BLOCK:(b + 1) * Q_BLOCK, :]

    qix = _dot(cqn, wiq_ref[...]).astype(BF16)
    for h in range(N_HEADS_IDX):
        for b in range(nqb):
            qi_ref[b, h * Q_BLOCK:(h + 1) * Q_BLOCK, :] = (
                qix[b * Q_BLOCK:(b + 1) * Q_BLOCK, h * LANES:(h + 1) * LANES])

    ckv = small_ref[:, Q_RANK:Q_RANK + KV_RANK]
    ms = jnp.mean(ckv * ckv, axis=-1, keepdims=True)
    ckv_ref[...] = (ckv * lax.rsqrt(ms + EPS) * ckvg_ref[...]).astype(BF16)

    slab = small_ref[:, Q_RANK + KV_RANK:Q_RANK + KV_RANK + LANES]
    lane = lax.broadcasted_iota(I32, slab.shape, 1)
    kmask = lane < HEAD_DIM_IDX
    mu = jnp.sum(jnp.where(kmask, slab, 0.0), axis=-1, keepdims=True) * (1.0 / HEAD_DIM_IDX)
    xc = jnp.where(kmask, slab - mu, 0.0)
    var = jnp.sum(xc * xc, axis=-1, keepdims=True) * (1.0 / HEAD_DIM_IDX)
    y = xc * lax.rsqrt(var + EPS) * kg_ref[...] + kb_ref[...]
    kidx_ref[...] = jnp.where(kmask, y, 0.0).astype(BF16)

    wscale = N_HEADS_IDX ** -0.5 * HEAD_DIM_IDX ** -0.5
    slab_t = slab.T
    wt = slab_t[HEAD_DIM_IDX:HEAD_DIM_IDX + N_HEADS_IDX, :] * wscale
    for b in range(nqb):
        wt_ref[b] = wt[:, b * Q_BLOCK:(b + 1) * Q_BLOCK]


def _prep(small, cqg, ckvg, kg, kb, wuq, wiq, wukt):
    t = small.shape[0]
    tm = 512
    nqb = tm // Q_BLOCK
    nq = t // Q_BLOCK
    hq = N_HEADS_A * Q_BLOCK
    const2 = lambda i: (0, 0)
    return pl.pallas_call(
        functools.partial(_prep_kernel, tm=tm),
        grid=(t // tm,),
        in_specs=[pl.BlockSpec((tm, small.shape[1]), lambda i: (i, 0)),
                  pl.BlockSpec(cqg.shape, const2),
                  pl.BlockSpec(ckvg.shape, const2),
                  pl.BlockSpec(kg.shape, const2),
                  pl.BlockSpec(kb.shape, const2),
                  pl.BlockSpec(wuq.shape, const2),
                  pl.BlockSpec(wiq.shape, const2),
                  pl.BlockSpec(wukt.shape, lambda i: (0, 0, 0))],
        out_specs=[pl.BlockSpec((nqb, hq, KV_RANK), lambda i: (i, 0, 0)),
                   pl.BlockSpec((nqb, hq, LANES), lambda i: (i, 0, 0)),
                   pl.BlockSpec((nqb, N_HEADS_IDX, Q_BLOCK), lambda i: (i, 0, 0)),
                   pl.BlockSpec((tm, KV_RANK), lambda i: (i, 0)),
                   pl.BlockSpec((tm, LANES), lambda i: (i, 0))],
        out_shape=[jax.ShapeDtypeStruct((nq, hq, KV_RANK), BF16),
                   jax.ShapeDtypeStruct((nq, hq, LANES), BF16),
                   jax.ShapeDtypeStruct((nq, N_HEADS_IDX, Q_BLOCK), F32),
                   jax.ShapeDtypeStruct((t, KV_RANK), BF16),
                   jax.ShapeDtypeStruct((t, LANES), BF16)],
        compiler_params=_cparams("parallel"),
        name="dsa_prep",
    )(small, cqg, ckvg, kg, kb, wuq, wiq, wukt)


def _bucket_tiles():
    t = np.arange(Q_BLOCK)[:, None]
    s = np.arange(Q_BLOCK)[None, :]
    max_exact = N_BUCKETS // 2
    tiles = []
    for d in range(3):
        n = np.maximum(t - s + d * Q_BLOCK, 0)
        nf = np.maximum(n, 1).astype(np.float32)
        large = max_exact + (np.log(nf / max_exact) / math.log(MAX_DISTANCE / max_exact)
                             * (N_BUCKETS - max_exact)).astype(np.int32)
        large = np.minimum(large, N_BUCKETS - 1)
        tiles.append(np.where(n < max_exact, n, large))
    assert 2 * Q_BLOCK - (Q_BLOCK - 1) >= MAX_DISTANCE
    tiles[2] = np.full_like(tiles[2], N_BUCKETS - 1)
    return np.stack(tiles).astype(np.int32)


def _bias_kernel(rel_ref, bucket_ref, o_ref):
    h = pl.program_id(0)
    bucket = bucket_ref[...]
    out = jnp.zeros(bucket.shape, F32)
    for b in range(N_BUCKETS):
        out = jnp.where(bucket == b, rel_ref[b, h], out)
    o_ref[:, 0] = (out - rel_ref[N_BUCKETS - 1, h]) * LOG2E


def _bias_tiles(rel_bias):
    buckets = jnp.asarray(_bucket_tiles())
    return pl.pallas_call(
        _bias_kernel,
        grid=(N_HEADS_A,),
        in_specs=[pl.BlockSpec(memory_space=pltpu.SMEM),
                  pl.BlockSpec((3, Q_BLOCK, Q_BLOCK), lambda h: (0, 0, 0))],
        out_specs=pl.BlockSpec((3, 1, Q_BLOCK, Q_BLOCK), lambda h: (0, h, 0, 0)),
        out_shape=jax.ShapeDtypeStruct((3, N_HEADS_A, Q_BLOCK, Q_BLOCK), F32),
        compiler_params=_cparams("arbitrary"),
        name="rel_bias_tiles",
    )(rel_bias, buckets)


def _attn_kernel(qi_ref, wt_ref, kidx_ref, ckv_ref, ql_ref, bt_ref, wuv_ref, ya_ref,
                 key_s, hi16_s, lo16_s, c16_s, mb_s, st_s, x_buf, p_buf, al_buf, acc_s, m_s, lp_s, j_s,
                 *, seq, top_k):
    ck = KEY_CHUNK
    sub = ck // Q_BLOCK
    i = pl.program_id(1)
    nchunk = (i + sub) // sub
    t_row = i * Q_BLOCK + lax.broadcasted_iota(I32, (1, Q_BLOCK), 1)
    qi = qi_ref[0]

    def score_matmul(c, slot):
        k0 = pl.multiple_of(c * ck, ck)
        st_s[slot] = _dot_nt(kidx_ref[0, pl.ds(k0, ck), :], qi)

    def score_reduce(c, slot):
        k0 = pl.multiple_of(c * ck, ck)
        for u in range(sub):
            acc = jnp.zeros((Q_BLOCK, Q_BLOCK), F32)
            for h in range(N_HEADS_IDX):
                sh = st_s[slot, u * Q_BLOCK:(u + 1) * Q_BLOCK, h * Q_BLOCK:(h + 1) * Q_BLOCK]
                acc = acc + jnp.maximum(sh, 0.0) * wt_ref[0, h:h + 1, :]
            s_pos = k0 + u * Q_BLOCK + lax.broadcasted_iota(I32, (Q_BLOCK, Q_BLOCK), 0)
            acc = jnp.where(s_pos <= t_row, acc, -jnp.inf)
            bits = lax.bitcast_convert_type(acc, I32)
            key = jnp.where(bits < 0, bits ^ 0x7FFFFFFF, bits)
            urows = slice(u * Q_BLOCK, (u + 1) * Q_BLOCK)
            key_s[c, urows, :] = key
            hi16_s[c, urows, :] = (key >> 16).astype(I16)
            lo16_s[c, urows, :] = ((key & 0xFFFF) - 32768).astype(I16)

    odd = nchunk % 2

    @pl.when(odd == 1)
    def _():
        score_matmul(0, 0)
        score_reduce(0, 0)

    def score_pair(p, carry):
        c = odd + 2 * p
        score_matmul(c, 0)
        score_matmul(c + 1, 1)
        score_reduce(c, 0)
        score_reduce(c + 1, 1)
        return carry

    lax.fori_loop(0, nchunk // 2, score_pair, 0)

    def count(pred):
        def body(c, cnt):
            tile = key_s[c]
            s_pos = c * ck + lax.broadcasted_iota(I32, (ck, Q_BLOCK), 0)
            hit = jnp.where(pred(tile, s_pos), 1, 0).astype(I32)
            return cnt + jnp.sum(hit.reshape(ck // SUBLANES, SUBLANES, Q_BLOCK), axis=0)
        cnt = lax.fori_loop(0, nchunk, body, jnp.zeros((SUBLANES, Q_BLOCK), I32))
        return jnp.sum(cnt, axis=0, keepdims=True)

    pack_rows = 2 * SUBLANES
    one16 = jnp.ones((pack_rows, Q_BLOCK), I16)
    zero16 = jnp.zeros((pack_rows, Q_BLOCK), I16)

    def pack16(v):
        return jnp.broadcast_to(v, (pack_rows, Q_BLOCK)).astype(I16)

    def bisect16_static(arr_s, n, n_lo0):
        def step(_, state):
            lo, hi, n_lo = state
            mid = (lo + hi) >> 1
            thr16 = pack16(mid)
            hits = [jnp.where(arr_s[c, r * pack_rows:(r + 1) * pack_rows, :] >= thr16, one16, zero16)
                    for c in range(n) for r in range(ck // pack_rows)]
            while len(hits) > 1:
                hits = [hits[k] + hits[k + 1] for k in range(0, len(hits) - 1, 2)] + hits[len(hits) & ~1:]
            n_mid = jnp.sum(hits[0].astype(I32), axis=0, keepdims=True)
            ok = n_mid >= top_k
            return jnp.where(ok, mid, lo), jnp.where(ok, hi, mid), jnp.where(ok, n_mid, n_lo)

        lo, _, n_lo = lax.fori_loop(
            0, 16, step, (jnp.full((1, Q_BLOCK), -2**15, I32), jnp.full((1, Q_BLOCK), 2**15, I32), n_lo0))
        return lo, n_lo

    def bisect16(arr_s, n_lo0):
        branches = [functools.partial(bisect16_static, arr_s, n) for n in range(1, seq // ck + 1)]
        return lax.switch(nchunk - 1, branches, n_lo0)

    prefix, n_prefix = bisect16(hi16_s, jnp.broadcast_to(nchunk * ck, (1, Q_BLOCK)).astype(I32))
    p16 = pack16(prefix)[None]

    def low_chunk(c, carry):
        hi = hi16_s[c].reshape(ck // pack_rows, pack_rows, Q_BLOCK)
        lo = lo16_s[c].reshape(ck // pack_rows, pack_rows, Q_BLOCK)
        c16 = jnp.where(hi > p16, jnp.full((), 2**15 - 1, I16), jnp.where(hi == p16, lo, jnp.full((), -2**15, I16)))
        c16_s[c] = c16.reshape(ck, Q_BLOCK)
        return carry

    lax.fori_loop(0, nchunk, low_chunk, 0)
    low, n_ge = bisect16(c16_s, n_prefix)
    thr = prefix * 65536 + (low + 32768)
    tie = (n_ge > top_k) & (thr > NEG_INF_KEY)
    j_s[...] = jnp.full(j_s.shape, seq, I32)

    @pl.when(jnp.max(jnp.where(tie, 1, 0)) > 0)
    def _():
        need = top_k - count(lambda tile, s_pos: tile > thr)

        def jbisect(_, lohi):
            lo, hi = lohi
            mid = (lo + hi) >> 1
            ok = count(lambda tile, s_pos: (tile == thr) & (s_pos <= mid)) >= need
            return jnp.where(ok, lo, mid), jnp.where(ok, mid, hi)
        lo_j = jnp.full((1, Q_BLOCK), -1, I32)
        hi_j = jnp.full((1, Q_BLOCK), seq - 1, I32)
        _, cut = lax.fori_loop(0, max(1, (seq - 1).bit_length()) + 1, jbisect, (lo_j, hi_j))
        j_s[...] = jnp.broadcast_to(jnp.where(tie, cut, seq), j_s.shape)

    jcut = j_s[0:1, :]

    def mask_chunk(c):
        tile = key_s[c]
        s_pos = c * ck + lax.broadcasted_iota(I32, (ck, Q_BLOCK), 0)
        sel = (tile > thr) | ((tile == thr) & (s_pos <= jcut))
        sel = sel & (s_pos <= t_row)
        mbt = jnp.where(sel, 0.0, NEG)
        for u in range(sub):
            mb_s[c * sub + u] = mbt[u * Q_BLOCK:(u + 1) * Q_BLOCK, :].T

    @pl.when(odd == 1)
    def _():
        mask_chunk(0)

    def mask_pair(p, carry):
        mask_chunk(odd + 2 * p)
        mask_chunk(odd + 2 * p + 1)
        return carry

    lax.fori_loop(0, nchunk // 2, mask_pair, 0)
    neg_tile = seq // Q_BLOCK
    mb_s[neg_tile] = jnp.full((Q_BLOCK, Q_BLOCK), NEG, F32)

    m_s[...] = jnp.full(m_s.shape, -jnp.inf, F32)
    lp_s[...] = jnp.zeros(lp_s.shape, F32)
    acc_s[...] = jnp.zeros(acc_s.shape, F32)

    ngroup = N_HEADS_A // HEAD_GROUP
    grows_n = HEAD_GROUP * Q_BLOCK
    nparts = grows_n // SM_ROWS
    first_start = i + 1 - sub * nchunk

    def chunk_start(c):
        return jnp.maximum(first_start + c * sub, 0)

    def load_kv(c):
        k0 = pl.multiple_of(chunk_start(c) * Q_BLOCK, Q_BLOCK)
        return ckv_ref[0, pl.ds(k0, ck), :]

    def part_rows(g, part):
        h = g * HEAD_GROUP + part * SM_ROWS // Q_BLOCK
        q0 = (part * SM_ROWS) % Q_BLOCK
        rows = slice(h * Q_BLOCK + q0, h * Q_BLOCK + q0 + SM_ROWS)
        return h, q0, rows, slice(part * SM_ROWS, (part + 1) * SM_ROWS)

    def stage_a(c, g, near):
        start = chunk_start(c)
        shared = (c == 0) & (nchunk > 1)
        tiles = []
        for u in range(sub):
            blk = start + u
            tiles.append(jnp.where(shared & (blk >= first_start + sub), neg_tile, blk))
        s = _dot_nt(ql_ref[0, g * grows_n:(g + 1) * grows_n, :], load_kv(c))
        for part in range(nparts):
            h, q0, rows, lrows = part_rows(g, part)
            mx = None
            for u in range(sub):
                cols = slice(u * Q_BLOCK, (u + 1) * Q_BLOCK)
                x = s[lrows, cols] + mb_s[tiles[u], q0:q0 + SM_ROWS, :]
                if near:
                    x = x + bt_ref[jnp.clip(i - (start + u), 0, 2), h, q0:q0 + SM_ROWS, :]
                x_buf[g, lrows, cols] = x
                mx = x if mx is None else jnp.maximum(mx, x)
            m_prev = m_s[rows, :]
            m_cur = jnp.broadcast_to(jnp.max(mx, axis=1, keepdims=True), (SM_ROWS, LANES))
            m_new = jnp.maximum(m_prev, m_cur)
            al_buf[g, lrows, :] = jnp.exp2(m_prev - m_new)
            m_s[rows, :] = m_new

    def stage_b(g):
        for part in range(nparts):
            _, _, rows, lrows = part_rows(g, part)
            m_new = m_s[rows, :]
            ps = [jnp.exp2(x_buf[g, lrows, u * Q_BLOCK:(u + 1) * Q_BLOCK] - m_new) for u in range(sub)]
            lp_s[rows, :] = al_buf[g, lrows, :] * lp_s[rows, :] + ((ps[0] + ps[1]) + (ps[2] + ps[3]))
            p_buf[g, lrows, :] = jnp.concatenate(ps, axis=1).astype(BF16)

    def stage_c(c, g):
        grows = slice(g * grows_n, (g + 1) * grows_n)
        alpha = jnp.concatenate([al_buf[g]] * (KV_RANK // LANES), axis=1)
        acc_s[grows, :] = alpha * acc_s[grows, :] + _dot(p_buf[g], load_kv(c))

    assert sub == 4 and ngroup >= 2

    def chunk_steps(c, near, first):
        for k in range(ngroup):
            stage_a(c, k, near)
            if k >= 1 or not first:
                stage_b((k - 1) % ngroup)
            if k >= 2:
                stage_c(c, k - 2)
            elif not first:
                stage_c(c - 1, k - 2 + ngroup)

    def drain(c):
        stage_b(ngroup - 1)
        stage_c(c, ngroup - 2)
        stage_c(c, ngroup - 1)

    last = nchunk - 1

    @pl.when(nchunk == 1)
    def _():
        chunk_steps(last, True, True)
        drain(last)

    @pl.when(nchunk > 1)
    def _():
        chunk_steps(0, False, True)

        def far_body(c, carry):
            chunk_steps(c, False, False)
            return carry

        lax.fori_loop(1, last, far_body, 0)
        chunk_steps(last, True, False)
        drain(last)

    for h in range(N_HEADS_A):
        rows = slice(h * Q_BLOCK, (h + 1) * Q_BLOCK)
        l = jnp.sum(lp_s[rows, :], axis=1, keepdims=True)
        o = (acc_s[rows, :] / l).astype(BF16)
        ya_ref[0, :, h * HEAD_DIM_A:(h + 1) * HEAD_DIM_A] = _dot(o, wuv_ref[h]).astype(ya_ref.dtype)


def _attention(qi, wt, kidx, ckv, ql, bt, wuv, batch, seq):
    nq = seq // Q_BLOCK
    hq = N_HEADS_A * Q_BLOCK
    ck = KEY_CHUNK
    grp = HEAD_GROUP * Q_BLOCK
    ngrp = N_HEADS_A // HEAD_GROUP
    assert seq % ck == 0
    top_k = min(TOP_K_MAX, seq // 4)
    kidx3 = kidx.reshape(batch, seq, LANES)
    ckv3 = ckv.reshape(batch, seq, KV_RANK)
    return pl.pallas_call(
        functools.partial(_attn_kernel, seq=seq, top_k=top_k),
        grid=(batch, nq),
        in_specs=[pl.BlockSpec((1, hq, LANES), lambda b, i: (b * nq + i, 0, 0)),
                  pl.BlockSpec((1, N_HEADS_IDX, Q_BLOCK), lambda b, i: (b * nq + i, 0, 0)),
                  pl.BlockSpec((1, seq, LANES), lambda b, i: (b, 0, 0)),
                  pl.BlockSpec((1, seq, KV_RANK), lambda b, i: (b, 0, 0)),
                  pl.BlockSpec((1, hq, KV_RANK), lambda b, i: (b * nq + i, 0, 0)),
                  pl.BlockSpec(bt.shape, lambda b, i: (0, 0, 0, 0)),
                  pl.BlockSpec(wuv.shape, lambda b, i: (0, 0, 0))],
        out_specs=pl.BlockSpec((1, Q_BLOCK, N_HEADS_A * HEAD_DIM_A), lambda b, i: (b, i, 0)),
        out_shape=jax.ShapeDtypeStruct((batch, seq, N_HEADS_A * HEAD_DIM_A), BF16),
        scratch_shapes=[pltpu.VMEM((seq // ck, ck, Q_BLOCK), I32),
                        pltpu.VMEM((seq // ck, ck, Q_BLOCK), I16),
                        pltpu.VMEM((seq // ck, ck, Q_BLOCK), I16),
                        pltpu.VMEM((seq // ck, ck, Q_BLOCK), I16),
                        pltpu.VMEM((nq + 1, Q_BLOCK, Q_BLOCK), F32),
                        pltpu.VMEM((2, ck, hq), F32),
                        pltpu.VMEM((ngrp, grp, ck), F32),
                        pltpu.VMEM((ngrp, grp, ck), BF16),
                        pltpu.VMEM((ngrp, grp, LANES), F32),
                        pltpu.VMEM((hq, KV_RANK), F32),
                        pltpu.VMEM((hq, LANES), F32),
                        pltpu.VMEM((hq, LANES), F32),
                        pltpu.VMEM((SUBLANES, Q_BLOCK), I32)],
        compiler_params=_cparams("parallel", "arbitrary"),
        name="dsa_attention",
    )(qi, wt, kidx3, ckv3, ql, bt, wuv)


def _xbc_conv_kernel(a_ref, w_ref, cw_ref, cb_ref, o_ref, tail_s, ya_s, yb_s, *, tm, per_b):
    i, j = pl.program_id(0), pl.program_id(1)

    @pl.when(i % per_b == 0)
    def _():
        tail_s[j] = jnp.zeros(tail_s.shape[1:], F32)

    nstrip = o_ref.shape[1] // CONV_STRIP
    y_bufs = (ya_s, yb_s)
    row0 = pl.multiple_of(jnp.minimum(i, 0) * SUBLANES, SUBLANES)

    def matmul_strip(s):
        cols = slice(s * CONV_STRIP, (s + 1) * CONV_STRIP)
        buf = y_bufs[s % 2]
        buf[0:SUBLANES, :] = tail_s[j, :, cols]
        buf[SUBLANES:SUBLANES + tm, :] = _dot(a_ref[...], w_ref[:, cols])

    def conv_strip(s):
        cols = slice(s * CONV_STRIP, (s + 1) * CONV_STRIP)
        buf = y_bufs[s % 2]
        tail_s[j, :, cols] = buf[pl.ds(row0 + tm, SUBLANES), :]
        for r in range(0, tm, CONV_ROWS):
            ext = buf[pl.ds(row0 + r, CONV_ROWS + SUBLANES), :]
            acc = ext * cw_ref[0:1, cols]
            for k in range(1, CONV_WIDTH):
                acc = pltpu.roll(acc, 1, axis=0) + ext * cw_ref[k:k + 1, cols]
            o_ref[r:r + CONV_ROWS, cols] = _silu(acc[SUBLANES:, :] + cb_ref[:, cols]).astype(o_ref.dtype)

    for s in range(nstrip):
        matmul_strip(s)
        if s >= 1:
            conv_strip(s - 1)
    conv_strip(nstrip - 1)


def _xbc_conv(a, w, conv_w, conv_b, seq, *, tm, tn):
    m, kdim = a.shape
    n = w.shape[1]
    assert m % tm == 0 and n % tn == 0 and seq % tm == 0
    return pl.pallas_call(
        functools.partial(_xbc_conv_kernel, tm=tm, per_b=seq // tm),
        grid=(m // tm, n // tn),
        in_specs=[pl.BlockSpec((tm, kdim), lambda i, j: (i, 0)),
                  pl.BlockSpec((kdim, tn), lambda i, j: (0, j)),
                  pl.BlockSpec((CONV_WIDTH, tn), lambda i, j: (0, j)),
                  pl.BlockSpec((1, tn), lambda i, j: (0, j))],
        out_specs=pl.BlockSpec((tm, tn), lambda i, j: (i, j)),
        out_shape=jax.ShapeDtypeStruct((m, n), BF16),
        scratch_shapes=[pltpu.VMEM((n // tn, SUBLANES, tn), F32),
                        pltpu.VMEM((SUBLANES + tm, CONV_STRIP), F32),
                        pltpu.VMEM((SUBLANES + tm, CONV_STRIP), F32)],
        compiler_params=_cparams("arbitrary", "arbitrary"),
        name="in_xbc_conv",
    )(a, w, conv_w, conv_b)


def _ssd_kernel(xc_ref, zs_ref, dt_ref, dtb_ref, aneg_ref, dskip_ref, ng_ref, e_ref, y_ref,
                state_s, de_s, *, d_inner):
    c = pl.program_id(1)
    hg = d_inner // SSM_GROUPS
    heads_g = hg // SSM_HEAD_DIM
    b_off = d_inner
    c_off = d_inner + SSM_GROUPS * D_STATE

    @pl.when(c == 0)
    def _():
        state_s[...] = jnp.zeros_like(state_s)

    x = dt_ref[0] + dtb_ref[...]
    dt = jnp.maximum(x, 0.0) + jnp.log1p(jnp.exp(-jnp.abs(x)))
    da = dt * aneg_ref[...]
    tri_i = lax.broadcasted_iota(I32, (CHUNK, CHUNK), 0) >= lax.broadcasted_iota(I32, (CHUNK, CHUNK), 1)
    tri = jnp.where(tri_i, 1.0, 0.0).astype(BF16)
    acs = sum(_dot(tri, part) for part in _split3(da))
    acs_t = acs.T
    row_t = acs_t - jnp.log(dt.T)
    acs_last = acs[CHUNK - 1:CHUNK, :]
    e = e_ref[...]
    de_hi, de_mid, _ = _split3(jnp.exp(acs_last - acs) * dt)
    el_terms = [jnp.broadcast_to(part, (2 * SUBLANES, LANES)) for part in _split3(jnp.exp(acs_last))]
    first = _dot(jnp.concatenate([de_hi] + el_terms, axis=0), e)
    de_s[...] = first[:CHUNK] + _dot(de_mid, e)
    el_x = (first[CHUNK:CHUNK + 1] + first[CHUNK + 2 * SUBLANES:CHUNK + 2 * SUBLANES + 1]
            + first[CHUNK + 4 * SUBLANES:CHUNK + 4 * SUBLANES + 1])
    lane = lax.broadcasted_iota(I32, (CHUNK, LANES), 1)

    for g in range(SSM_GROUPS):
        cols = slice(g * hg, (g + 1) * hg)
        xg = xc_ref[0, :, cols]
        bg = xc_ref[0, :, b_off + g * D_STATE:b_off + (g + 1) * D_STATE]
        cg = xc_ref[0, :, c_off + g * D_STATE:c_off + (g + 1) * D_STATE]
        cb = _dot_nt(cg, bg)
        cg_f = cg.astype(F32)
        sg = state_s[g]
        sg_b = sg.astype(BF16)
        tiles = []
        for pr in range(heads_g // 2):
            pcols = slice(pr * LANES, (pr + 1) * LANES)
            rhs = jnp.concatenate([xg[:, pcols], sg_b[:, pcols]], axis=0)
            outs = []
            for q in range(2):
                h = g * heads_g + pr * 2 + q
                col = jnp.broadcast_to(acs[:, h:h + 1], (CHUNK, CHUNK))
                w = cb * jnp.exp(jnp.where(tri_i, col - row_t[h:h + 1, :], NEG))
                lhs = jnp.concatenate([w.astype(BF16), (cg_f * jnp.exp(col)).astype(BF16)], axis=1)
                outs.append(_dot(lhs, rhs))
            tiles.append(jnp.where(lane < SSM_HEAD_DIM, outs[0], outs[1]))
        xg_f = xg.astype(F32)
        y = jnp.concatenate(tiles, axis=1) + dskip_ref[:, cols] * xg_f
        xd = (xg_f * de_s[:, cols]).astype(BF16)
        state_s[g] = sg * el_x[:, cols] + _dot(bg.astype(F32).T.astype(BF16), xd)
        yz = y * zs_ref[0, :, cols].astype(F32)
        ms = jnp.mean(yz * yz, axis=-1, keepdims=True)
        y_ref[0, :, cols] = (yz * lax.rsqrt(ms + EPS) * ng_ref[:, cols]).astype(y_ref.dtype)


def _ssd(xc, zs, small3, dtb, aneg, dskip_x, ng, expand, dt_block):
    batch, seq, d_xbc = xc.shape
    d_inner = zs.shape[2]
    nc = seq // CHUNK
    const2 = lambda b, c: (0, 0)
    return pl.pallas_call(
        functools.partial(_ssd_kernel, d_inner=d_inner),
        grid=(batch, nc),
        in_specs=[pl.BlockSpec((1, CHUNK, d_xbc), lambda b, c: (b, c, 0)),
                  pl.BlockSpec((1, CHUNK, d_inner), lambda b, c: (b, c, 0)),
                  pl.BlockSpec((1, CHUNK, LANES), lambda b, c: (b, c, dt_block)),
                  pl.BlockSpec(dtb.shape, const2),
                  pl.BlockSpec(aneg.shape, const2),
                  pl.BlockSpec(dskip_x.shape, const2),
                  pl.BlockSpec(ng.shape, const2),
                  pl.BlockSpec(expand.shape, const2)],
        out_specs=pl.BlockSpec((1, CHUNK, d_inner), lambda b, c: (b, c, 0)),
        out_shape=jax.ShapeDtypeStruct((batch, seq, d_inner), BF16),
        scratch_shapes=[pltpu.VMEM((SSM_GROUPS, D_STATE, d_inner // SSM_GROUPS), F32),
                        pltpu.VMEM((CHUNK, d_inner), F32)],
        compiler_params=_cparams("parallel", "arbitrary"),
        name="ssd",
    )(xc, zs, small3, dtb, aneg, dskip_x, ng, expand)


def _pad_cols(a, n):
    return jnp.pad(a, ((0, 0), (0, n - a.shape[1])))


def _layer(x2, mod, batch, seq, norm1_g, w_in, cq_norm_g, ckv_norm_g, kidx_norm_g, kidx_norm_b, w_uq, w_iq,
           w_uk, w_uv, rel_bias, conv_w, conv_b, dt_bias, a_log, d_skip, ssm_norm_g, w_proj_a, w_proj_b,
           w_out, norm2_g, w_gate, w_up, w_down, final_g):
    t, d = x2.shape
    n_heads_b = dt_bias.shape[0]
    d_inner = n_heads_b * SSM_HEAD_DIM
    d_xbc = d_inner + 2 * SSM_GROUPS * D_STATE
    ha = N_HEADS_A * HEAD_DIM_A
    sh1, sc1, g1, sh2, sc2, g2 = [m.reshape(batch, 1, d) for m in jnp.split(mod, 6, axis=-1)]

    o_z = Q_RANK + KV_RANK + HEAD_DIM_IDX + N_HEADS_IDX
    o_xbc = o_z + d_inner
    o_dt = o_xbc + d_xbc
    o_g = o_dt + n_heads_b
    small_w = 1024
    dt_col = 896
    assert o_z <= dt_col and dt_col + n_heads_b <= small_w and dt_col % LANES == 0
    w_small = jnp.concatenate(
        [_pad_cols(w_in[:, :o_z], dt_col), _pad_cols(w_in[:, o_dt:o_g], small_w - dt_col)], axis=1).astype(BF16)
    w_z = w_in[:, o_z:o_xbc].astype(BF16)
    w_xbc = w_in[:, o_xbc:o_dt].astype(BF16)
    w_g = w_in[:, o_g:].astype(BF16)

    tm = 1024
    per_b = seq // tm
    small, h = _norm_proj(x2, norm1_g.reshape(1, d), sc1, sh1, w_small, seq, tm=tm)
    zs = _matmul("in_z", h, w_z, BF16, tm=tm, tn=2048, epilogue=lambda acc: _silu(acc))
    xc = _xbc_conv(h, w_xbc, conv_w, conv_b.reshape(1, d_xbc), seq, tm=tm, tn=1024)
    sg = _matmul("in_gates", h, w_g, BF16, tm=tm, tn=2048, epilogue=lambda acc: jax.nn.sigmoid(acc))

    w_iq_p = jnp.pad(w_iq.reshape(Q_RANK, N_HEADS_IDX, HEAD_DIM_IDX),
                     ((0, 0), (0, 0), (0, LANES - HEAD_DIM_IDX))).reshape(Q_RANK, N_HEADS_IDX * LANES).astype(BF16)
    w_uk_t = jnp.transpose(w_uk.reshape(KV_RANK, N_HEADS_A, HEAD_DIM_A), (1, 2, 0)).astype(BF16)
    w_uv_h = jnp.transpose(w_uv.reshape(KV_RANK, N_HEADS_A, HEAD_DIM_A), (1, 0, 2)).astype(BF16)
    ql, qi, wt, ckv, kidx = _prep(
        small, cq_norm_g.reshape(1, Q_RANK), ckv_norm_g.reshape(1, KV_RANK),
        _pad_cols(kidx_norm_g.reshape(1, HEAD_DIM_IDX), LANES), _pad_cols(kidx_norm_b.reshape(1, HEAD_DIM_IDX), LANES),
        w_uq.astype(BF16), w_iq_p, w_uk_t)
    bt = _bias_tiles(rel_bias)
    y_a = _attention(qi, wt, kidx, ckv, ql, bt, w_uv_h, batch, seq).reshape(t, ha)

    expand = np.zeros((LANES, d_inner), np.float32)
    for hh in range(n_heads_b):
        expand[hh, hh * SSM_HEAD_DIM:(hh + 1) * SSM_HEAD_DIM] = 1.0
    aneg = _pad_cols(-jnp.exp(a_log.astype(F32)).reshape(1, n_heads_b), LANES)
    y_b = _ssd(xc.reshape(batch, seq, d_xbc), zs.reshape(batch, seq, d_inner), small.reshape(batch, seq, small_w),
               _pad_cols(dt_bias.reshape(1, n_heads_b), LANES), aneg,
               jnp.repeat(d_skip, SSM_HEAD_DIM).reshape(1, d_inner), ssm_norm_g.reshape(1, d_inner),
               jnp.asarray(expand, BF16), dt_col // LANES).reshape(t, d_inner)

    tn = 1024
    tp = 512
    merged = _matmul2("proj_ab", y_a, w_proj_a.astype(BF16), y_b, w_proj_b.astype(BF16), BF16, tm=tm, tn=tp,
                      epilogue=lambda ra, rb, sa_ref, sb_ref: (sa_ref[...].astype(F32) * ra
                                                               + sb_ref[...].astype(F32) * rb),
                      extras=(sg, sg),
                      extra_specs=(pl.BlockSpec((tm, tp), lambda i, j: (i, j)),
                                   pl.BlockSpec((tm, tp), lambda i, j: (i, j + d // tp))))
    x1 = _matmul("w_out", merged, w_out.astype(BF16), F32, tm=tm, tn=tn,
                 epilogue=lambda acc, x_ref, g_ref: x_ref[...] + g_ref[0] * acc,
                 extras=(x2, g1), extra_specs=(_tile_spec(tm, tn), _batch_row_spec(tn, per_b)))

    d_ff = w_gate.shape[1]
    act = _ffn_in(x1, norm2_g.reshape(1, d), sc2, sh2, w_gate, w_up, seq, tm=tm, tn=512)

    def down_epilogue(acc, x_ref, g_ref, f_ref):
        xo = x_ref[...] + g_ref[0] * acc
        ms = jnp.mean(xo * xo, axis=-1, keepdims=True)
        return xo * lax.rsqrt(ms + EPS) * f_ref[...]

    tmd = 1024
    assert d_ff % 4 == 0
    return _matmul("ffn_down", act, w_down.astype(BF16), F32, tm=tmd, tn=d, tk=d_ff // 4,
                   epilogue=down_epilogue, extras=(x1, g2, final_g.reshape(1, d)),
                   extra_specs=(_tile_spec(tmd, d), _batch_row_spec(d, seq // tmd),
                                pl.BlockSpec((1, d), lambda i, j, k: (0, 0))))


def kernel(x, c, w_ada, b_ada, norm1_g, w_in, cq_norm_g, ckv_norm_g, kidx_norm_g, kidx_norm_b, w_uq, w_iq, w_uk, w_uv, rel_bias, conv_w, conv_b, dt_bias, a_log, d_skip, ssm_norm_g, w_proj_a, w_proj_b, w_out, norm2_g, w_gate, w_up, w_down, final_g):
    batch, seq, d = x.shape
    assert w_ada.shape[0] == 1, "single-layer block"
    c8 = jnp.pad(c, ((0, SUBLANES - batch % SUBLANES if batch % SUBLANES else 0), (0, 0)))
    mod = _ada(c8, w_ada[0], b_ada[0].reshape(1, -1))[:batch]
    out = _layer(x.reshape(batch * seq, d), mod, batch, seq, norm1_g[0], w_in[0], cq_norm_g[0], ckv_norm_g[0],
                 kidx_norm_g[0], kidx_norm_b[0], w_uq[0], w_iq[0], w_uk[0], w_uv[0], rel_bias, conv_w[0],
                 conv_b[0], dt_bias[0], a_log[0], d_skip[0], ssm_norm_g[0], w_proj_a[0], w_proj_b[0], w_out[0],
                 norm2_g[0], w_gate[0], w_up[0], w_down[0], final_g)
    return out.reshape(batch, seq, d)
```

```python
import functools
import math

import numpy as np
import jax
import jax.numpy as jnp
from jax import lax
from jax.experimental import pallas as pl
from jax.experimental.pallas import tpu as pltpu

F32 = jnp.float32
BF16 = jnp.bfloat16
I32 = jnp.int32
I16 = jnp.int16

EPS = 1e-6
LANES = 128
SUBLANES = 8
VMEM_LIMIT = 56 * 2**20

N_HEADS_A = 16
HEAD_DIM_A = 128
Q_RANK = 512
KV_RANK = 256
N_HEADS_IDX = 16
HEAD_DIM_IDX = 64
TOP_K_MAX = 256
Q_BLOCK = 128
KEY_CHUNK = 512
HEAD_GROUP = 4
SM_ROWS = 64
LOG2E = 1.4426950408889634
N_BUCKETS = 32
MAX_DISTANCE = 128
SSM_HEAD_DIM = 64
SSM_GROUPS = 8
D_STATE = 128
NORM_ROWS = 32
CONV_WIDTH = 4
CONV_STRIP = 256
CONV_ROWS = 64
CHUNK = 128
NEG = -1e30
NEG_INF_KEY = -2139095041


def _cparams(*sem):
    return pltpu.CompilerParams(dimension_semantics=sem, vmem_limit_bytes=VMEM_LIMIT)


def _silu(x):
    return x * jax.nn.sigmoid(x)


def _split3(v):
    hi = v.astype(BF16)
    r1 = v - hi.astype(F32)
    mid = r1.astype(BF16)
    lo = (r1 - mid.astype(F32)).astype(BF16)
    return hi, mid, lo


def _dot(a, b):
    return jnp.dot(a, b, preferred_element_type=F32)


def _dot_nt(a, b):
    return lax.dot_general(a, b, (((1,), (1,)), ((), ())), preferred_element_type=F32)


def _ada_kernel(c_ref, w_ref, b_ref, o_ref):
    a = _silu(c_ref[...])
    o_ref[...] = _dot(a, w_ref[...]) + b_ref[...]


def _ada(c8, w, b):
    m, d = c8.shape
    n = w.shape[1]
    tn = 1024
    return pl.pallas_call(
        _ada_kernel,
        grid=(n // tn,),
        in_specs=[pl.BlockSpec((m, d), lambda j: (0, 0)),
                  pl.BlockSpec((d, tn), lambda j: (0, j)),
                  pl.BlockSpec((1, tn), lambda j: (0, j))],
        out_specs=pl.BlockSpec((m, tn), lambda j: (0, j)),
        out_shape=jax.ShapeDtypeStruct((m, n), F32),
        compiler_params=_cparams("parallel"),
        name="ada",
    )(c8, w, b)


def _norm_rows(x_ref, g_ref, sc_ref, sh_ref, h_ref):
    gain = g_ref[...] * (1.0 + sc_ref[0])

    def rows(r, carry):
        sl = pl.ds(pl.multiple_of(r * NORM_ROWS, NORM_ROWS), NORM_ROWS)
        x = x_ref[sl, :]
        ms = jnp.mean(x * x, axis=-1, keepdims=True)
        h_ref[sl, :] = (x * lax.rsqrt(ms + EPS) * gain + sh_ref[0]).astype(h_ref.dtype)
        return carry

    lax.fori_loop(0, x_ref.shape[0] // NORM_ROWS, rows, 0, unroll=4)


def _norm_proj_kernel(x_ref, g_ref, sc_ref, sh_ref, w_ref, o_ref, h_ref):
    _norm_rows(x_ref, g_ref, sc_ref, sh_ref, h_ref)
    o_ref[...] = _dot(h_ref[...], w_ref[...])


def _norm_proj(x2, g, sc, sh, w, seq, *, tm):
    t, d = x2.shape
    n = w.shape[1]
    per_b = seq // tm
    assert t % tm == 0 and seq % tm == 0
    return pl.pallas_call(
        _norm_proj_kernel,
        grid=(t // tm,),
        in_specs=[pl.BlockSpec((tm, d), lambda i: (i, 0)),
                  pl.BlockSpec((1, d), lambda i: (0, 0)),
                  pl.BlockSpec((1, 1, d), lambda i: (i // per_b, 0, 0)),
                  pl.BlockSpec((1, 1, d), lambda i: (i // per_b, 0, 0)),
                  pl.BlockSpec((d, n), lambda i: (0, 0))],
        out_specs=[pl.BlockSpec((tm, n), lambda i: (i, 0)),
                   pl.BlockSpec((tm, d), lambda i: (i, 0))],
        out_shape=[jax.ShapeDtypeStruct((t, n), F32), jax.ShapeDtypeStruct((t, d), BF16)],
        compiler_params=_cparams("parallel"),
        name="norm_in_small",
    )(x2, g, sc, sh, w)


def _mm_kernel(*refs, nk, n_extra, epilogue):
    a_ref, w_ref = refs[0], refs[1]
    extra = refs[2:2 + n_extra]
    o_ref = refs[2 + n_extra]
    if nk == 1:
        acc = _dot(a_ref[...], w_ref[...])
        o_ref[...] = epilogue(acc, *extra).astype(o_ref.dtype)
        return
    k = pl.program_id(2)

    @pl.when(k == 0)
    def _():
        o_ref[...] = _dot(a_ref[...], w_ref[...])

    @pl.when((k > 0) & (k < nk - 1))
    def _():
        o_ref[...] += _dot(a_ref[...], w_ref[...])

    @pl.when(k == nk - 1)
    def _():
        o_ref[...] = epilogue(o_ref[...] + _dot(a_ref[...], w_ref[...]), *extra)


def _matmul(name, a, w, out_dtype, *, tm, tn, tk=None, epilogue=None, extras=(), extra_specs=()):
    m, kdim = a.shape
    n = w.shape[1]
    tk = kdim if tk is None else tk
    nk = kdim // tk
    assert m % tm == 0 and n % tn == 0 and kdim % tk == 0
    if epilogue is None:
        epilogue = lambda acc: acc
    assert nk == 1 or (nk >= 2 and out_dtype == F32), "split K accumulates in the f32 output block"
    w_spec_kwargs = dict(pipeline_mode=pl.Buffered(1)) if (nk == 1 and n == tn) else {}
    return pl.pallas_call(
        functools.partial(_mm_kernel, nk=nk, n_extra=len(extras), epilogue=epilogue),
        grid=(m // tm, n // tn, nk),
        in_specs=[pl.BlockSpec((tm, tk), lambda i, j, k: (i, k)),
                  pl.BlockSpec((tk, tn), lambda i, j, k: (k, j), **w_spec_kwargs),
                  *extra_specs],
        out_specs=pl.BlockSpec((tm, tn), lambda i, j, k: (i, j)),
        out_shape=jax.ShapeDtypeStruct((m, n), out_dtype),
        compiler_params=_cparams("parallel", "parallel", "arbitrary"),
        name=name,
    )(a, w, *extras)


def _tile_spec(tm, tn):
    return pl.BlockSpec((tm, tn), lambda i, j, k: (i, j))


def _batch_row_spec(tn, per_b):
    return pl.BlockSpec((1, 1, tn), lambda i, j, k: (i // per_b, 0, j))


def _mm2_kernel(a1_ref, w1_ref, a2_ref, w2_ref, *rest, epilogue):
    r1 = _dot(a1_ref[...], w1_ref[...])
    r2 = _dot(a2_ref[...], w2_ref[...])
    *extra, o_ref = rest
    o_ref[...] = epilogue(r1, r2, *extra).astype(o_ref.dtype)


def _matmul2(name, a1, w1, a2, w2, out_dtype, *, tm, tn, epilogue, extras=(), extra_specs=()):
    m, k1 = a1.shape
    k2 = a2.shape[1]
    n = w1.shape[1]
    assert m % tm == 0 and n % tn == 0 and w2.shape[1] == n
    return pl.pallas_call(
        functools.partial(_mm2_kernel, epilogue=epilogue),
        grid=(m // tm, n // tn),
        in_specs=[pl.BlockSpec((tm, k1), lambda i, j: (i, 0)),
                  pl.BlockSpec((k1, tn), lambda i, j: (0, j)),
                  pl.BlockSpec((tm, k2), lambda i, j: (i, 0)),
                  pl.BlockSpec((k2, tn), lambda i, j: (0, j)),
                  *extra_specs],
        out_specs=pl.BlockSpec((tm, tn), lambda i, j: (i, j)),
        out_shape=jax.ShapeDtypeStruct((m, n), out_dtype),
        compiler_params=_cparams("parallel", "parallel"),
        name=name,
    )(a1, w1, a2, w2, *extras)


def _ffn_in_kernel(x_ref, g_ref, sc_ref, sh_ref, wg_ref, wu_ref, o_ref, h_s):
    @pl.when(pl.program_id(1) == 0)
    def _():
        _norm_rows(x_ref, g_ref, sc_ref, sh_ref, h_s)

    h = h_s[...]
    rg = _dot(h, wg_ref[...].astype(BF16))
    ru = _dot(h, wu_ref[...].astype(BF16))
    o_ref[...] = (_silu(rg) * ru).astype(o_ref.dtype)


def _ffn_in(x, g, sc, sh, w_gate, w_up, seq, *, tm, tn):
    t, d = x.shape
    f = w_gate.shape[1]
    assert t % tm == 0 and f % tn == 0 and seq % tm == 0
    per_b = seq // tm
    return pl.pallas_call(
        _ffn_in_kernel,
        grid=(t // tm, f // tn),
        in_specs=[pl.BlockSpec((tm, d), lambda i, j: (i, 0)),
                  pl.BlockSpec((1, d), lambda i, j: (0, 0)),
                  pl.BlockSpec((1, 1, d), lambda i, j: (i // per_b, 0, 0)),
                  pl.BlockSpec((1, 1, d), lambda i, j: (i // per_b, 0, 0)),
                  pl.BlockSpec((d, tn), lambda i, j: (0, j)),
                  pl.BlockSpec((d, tn), lambda i, j: (0, j))],
        out_specs=pl.BlockSpec((tm, tn), lambda i, j: (i, j)),
        out_shape=jax.ShapeDtypeStruct((t, f), BF16),
        scratch_shapes=[pltpu.VMEM((tm, d), BF16)],
        compiler_params=_cparams("parallel", "arbitrary"),
        name="ffn_norm_gate_up",
    )(x, g, sc, sh, w_gate, w_up)


def _prep_kernel(small_ref, cqg_ref, ckvg_ref, kg_ref, kb_ref, wuq_ref, wiq_ref, wukt_ref,
                 ql_ref, qi_ref, wt_ref, ckv_ref, kidx_ref, *, tm):
    nqb = tm // Q_BLOCK
    cq = small_ref[:, 0:Q_RANK]
    ms = jnp.mean(cq * cq, axis=-1, keepdims=True)
    cqn = (cq * lax.rsqrt(ms + EPS) * cqg_ref[...]).astype(BF16)

    q = _dot(cqn, wuq_ref[...]).astype(BF16)
    scale = HEAD_DIM_A ** -0.5 * LOG2E
    for h in range(N_HEADS_A):
        qlh = (_dot(q[:, h * HEAD_DIM_A:(h + 1) * HEAD_DIM_A], wukt_ref[h]) * scale).astype(BF16)
        for b in range(nqb):
            ql_ref[b, h * Q_BLOCK:(h + 1) * Q_BLOCK, :] = qlh[b * Q_BLOCK:(b + 1) * Q_BLOCK, :]

    qix = _dot(cqn, wiq_ref[...]).astype(BF16)
    for h in range(N_HEADS_IDX):
        for b in range(nqb):
            qi_ref[b, h * Q_BLOCK:(h + 1) * Q_BLOCK, :] = (
                qix[b * Q_BLOCK:(b + 1) * Q_BLOCK, h * LANES:(h + 1) * LANES])

    ckv = small_ref[:, Q_RANK:Q_RANK + KV_RANK]
    ms = jnp.mean(ckv * ckv, axis=-1, keepdims=True)
    ckv_ref[...] = (ckv * lax.rsqrt(ms + EPS) * ckvg_ref[...]).astype(BF16)

    slab = small_ref[:, Q_RANK + KV_RANK:Q_RANK + KV_RANK + LANES]
    lane = lax.broadcasted_iota(I32, slab.shape, 1)
    kmask = lane < HEAD_DIM_IDX
    mu = jnp.sum(jnp.where(kmask, slab, 0.0), axis=-1, keepdims=True) * (1.0 / HEAD_DIM_IDX)
    xc = jnp.where(kmask, slab - mu, 0.0)
    var = jnp.sum(xc * xc, axis=-1, keepdims=True) * (1.0 / HEAD_DIM_IDX)
    y = xc * lax.rsqrt(var + EPS) * kg_ref[...] + kb_ref[...]
    kidx_ref[...] = jnp.where(kmask, y, 0.0).astype(BF16)

    wscale = N_HEADS_IDX ** -0.5 * HEAD_DIM_IDX ** -0.5
    slab_t = slab.T
    wt = slab_t[HEAD_DIM_IDX:HEAD_DIM_IDX + N_HEADS_IDX, :] * wscale
    for b in range(nqb):
        wt_ref[b] = wt[:, b * Q_BLOCK:(b + 1) * Q_BLOCK]


def _prep(small, cqg, ckvg, kg, kb, wuq, wiq, wukt):
    t = small.shape[0]
    tm = 512
    nqb = tm // Q_BLOCK
    nq = t // Q_BLOCK
    hq = N_HEADS_A * Q_BLOCK
    const2 = lambda i: (0, 0)
    return pl.pallas_call(
        functools.partial(_prep_kernel, tm=tm),
        grid=(t // tm,),
        in_specs=[pl.BlockSpec((tm, small.shape[1]), lambda i: (i, 0)),
                  pl.BlockSpec(cqg.shape, const2),
                  pl.BlockSpec(ckvg.shape, const2),
                  pl.BlockSpec(kg.shape, const2),
                  pl.BlockSpec(kb.shape, const2),
                  pl.BlockSpec(wuq.shape, const2),
                  pl.BlockSpec(wiq.shape, const2),
                  pl.BlockSpec(wukt.shape, lambda i: (0, 0, 0))],
        out_specs=[pl.BlockSpec((nqb, hq, KV_RANK), lambda i: (i, 0, 0)),
                   pl.BlockSpec((nqb, hq, LANES), lambda i: (i, 0, 0)),
                   pl.BlockSpec((nqb, N_HEADS_IDX, Q_BLOCK), lambda i: (i, 0, 0)),
                   pl.BlockSpec((tm, KV_RANK), lambda i: (i, 0)),
                   pl.BlockSpec((tm, LANES), lambda i: (i, 0))],
        out_shape=[jax.ShapeDtypeStruct((nq, hq, KV_RANK), BF16),
                   jax.ShapeDtypeStruct((nq, hq, LANES), BF16),
                   jax.ShapeDtypeStruct((nq, N_HEADS_IDX, Q_BLOCK), F32),
                   jax.ShapeDtypeStruct((t, KV_RANK), BF16),
                   jax.ShapeDtypeStruct((t, LANES), BF16)],
        compiler_params=_cparams("parallel"),
        name="dsa_prep",
    )(small, cqg, ckvg, kg, kb, wuq, wiq, wukt)


def _bucket_tiles():
    t = np.arange(Q_BLOCK)[:, None]
    s = np.arange(Q_BLOCK)[None, :]
    max_exact = N_BUCKETS // 2
    tiles = []
    for d in range(3):
        n = np.maximum(t - s + d * Q_BLOCK, 0)
        nf = np.maximum(n, 1).astype(np.float32)
        large = max_exact + (np.log(nf / max_exact) / math.log(MAX_DISTANCE / max_exact)
                             * (N_BUCKETS - max_exact)).astype(np.int32)
        large = np.minimum(large, N_BUCKETS - 1)
        tiles.append(np.where(n < max_exact, n, large))
    assert 2 * Q_BLOCK - (Q_BLOCK - 1) >= MAX_DISTANCE
    tiles[2] = np.full_like(tiles[2], N_BUCKETS - 1)
    return np.stack(tiles).astype(np.int32)


def _bias_kernel(rel_ref, bucket_ref, o_ref):
    h = pl.program_id(0)
    bucket = bucket_ref[...]
    out = jnp.zeros(bucket.shape, F32)
    for b in range(N_BUCKETS):
        out = jnp.where(bucket == b, rel_ref[b, h], out)
    o_ref[:, 0] = (out - rel_ref[N_BUCKETS - 1, h]) * LOG2E


def _bias_tiles(rel_bias):
    buckets = jnp.asarray(_bucket_tiles())
    return pl.pallas_call(
        _bias_kernel,
        grid=(N_HEADS_A,),
        in_specs=[pl.BlockSpec(memory_space=pltpu.SMEM),
                  pl.BlockSpec((3, Q_BLOCK, Q_BLOCK), lambda h: (0, 0, 0))],
        out_specs=pl.BlockSpec((3, 1, Q_BLOCK, Q_BLOCK), lambda h: (0, h, 0, 0)),
        out_shape=jax.ShapeDtypeStruct((3, N_HEADS_A, Q_BLOCK, Q_BLOCK), F32),
        compiler_params=_cparams("arbitrary"),
        name="rel_bias_tiles",
    )(rel_bias, buckets)


def _attn_kernel(qi_ref, wt_ref, kidx_ref, ckv_ref, ql_ref, bt_ref, wuv_ref, ya_ref,
                 key_s, hi16_s, lo16_s, c16_s, mb_s, st_s, x_buf, p_buf, al_buf, acc_s, m_s, lp_s, j_s,
                 *, seq, top_k):
    ck = KEY_CHUNK
    sub = ck // Q_BLOCK
    i = pl.program_id(1)
    nchunk = (i + sub) // sub
    t_row = i * Q_BLOCK + lax.broadcasted_iota(I32, (1, Q_BLOCK), 1)
    qi = qi_ref[0]

    def score_matmul(c, slot):
        k0 = pl.multiple_of(c * ck, ck)
        st_s[slot] = _dot_nt(kidx_ref[0, pl.ds(k0, ck), :], qi)

    def score_reduce(c, slot):
        k0 = pl.multiple_of(c * ck, ck)
        for u in range(sub):
            acc = jnp.zeros((Q_BLOCK, Q_BLOCK), F32)
            for h in range(N_HEADS_IDX):
                sh = st_s[slot, u * Q_BLOCK:(u + 1) * Q_BLOCK, h * Q_BLOCK:(h + 1) * Q_BLOCK]
                acc = acc + jnp.maximum(sh, 0.0) * wt_ref[0, h:h + 1, :]
            s_pos = k0 + u * Q_BLOCK + lax.broadcasted_iota(I32, (Q_BLOCK, Q_BLOCK), 0)
            acc = jnp.where(s_pos <= t_row, acc, -jnp.inf)
            bits = lax.bitcast_convert_type(acc, I32)
            key = jnp.where(bits < 0, bits ^ 0x7FFFFFFF, bits)
            urows = slice(u * Q_BLOCK, (u + 1) * Q_BLOCK)
            key_s[c, urows, :] = key
            hi16_s[c, urows, :] = (key >> 16).astype(I16)
            lo16_s[c, urows, :] = ((key & 0xFFFF) - 32768).astype(I16)

    odd = nchunk % 2

    @pl.when(odd == 1)
    def _():
        score_matmul(0, 0)
        score_reduce(0, 0)

    def score_pair(p, carry):
        c = odd + 2 * p
        score_matmul(c, 0)
        score_matmul(c + 1, 1)
        score_reduce(c, 0)
        score_reduce(c + 1, 1)
        return carry

    lax.fori_loop(0, nchunk // 2, score_pair, 0)

    def count(pred):
        def body(c, cnt):
            tile = key_s[c]
            s_pos = c * ck + lax.broadcasted_iota(I32, (ck, Q_BLOCK), 0)
            hit = jnp.where(pred(tile, s_pos), 1, 0).astype(I32)
            return cnt + jnp.sum(hit.reshape(ck // SUBLANES, SUBLANES, Q_BLOCK), axis=0)
        cnt = lax.fori_loop(0, nchunk, body, jnp.zeros((SUBLANES, Q_BLOCK), I32))
        return jnp.sum(cnt, axis=0, keepdims=True)

    pack_rows = 2 * SUBLANES
    one16 = jnp.ones((pack_rows, Q_BLOCK), I16)
    zero16 = jnp.zeros((pack_rows, Q_BLOCK), I16)

    def pack16(v):
        return jnp.broadcast_to(v, (pack_rows, Q_BLOCK)).astype(I16)

    def bisect16_static(arr_s, n, n_lo0):
        def step(_, state):
            lo, hi, n_lo = state
            mid = (lo + hi) >> 1
            thr16 = pack16(mid)
            hits = [jnp.where(arr_s[c, r * pack_rows:(r + 1) * pack_rows, :] >= thr16, one16, zero16)
                    for c in range(n) for r in range(ck // pack_rows)]
            while len(hits) > 1:
                hits = [hits[k] + hits[k + 1] for k in range(0, len(hits) - 1, 2)] + hits[len(hits) & ~1:]
            n_mid = jnp.sum(hits[0].astype(I32), axis=0, keepdims=True)
            ok = n_mid >= top_k
            return jnp.where(ok, mid, lo), jnp.where(ok, hi, mid), jnp.where(ok, n_mid, n_lo)

        lo, _, n_lo = lax.fori_loop(
            0, 16, step, (jnp.full((1, Q_BLOCK), -2**15, I32), jnp.full((1, Q_BLOCK), 2**15, I32), n_lo0))
        return lo, n_lo

    def bisect16(arr_s, n_lo0):
        branches = [functools.partial(bisect16_static, arr_s, n) for n in range(1, seq // ck + 1)]
        return lax.switch(nchunk - 1, branches, n_lo0)

    prefix, n_prefix = bisect16(hi16_s, jnp.broadcast_to(nchunk * ck, (1, Q_BLOCK)).astype(I32))
    p16 = pack16(prefix)[None]

    def low_chunk(c, carry):
        hi = hi16_s[c].reshape(ck // pack_rows, pack_rows, Q_BLOCK)
        lo = lo16_s[c].reshape(ck // pack_rows, pack_rows, Q_BLOCK)
        c16 = jnp.where(hi > p16, jnp.full((), 2**15 - 1, I16), jnp.where(hi == p16, lo, jnp.full((), -2**15, I16)))
        c16_s[c] = c16.reshape(ck, Q_BLOCK)
        return carry

    lax.fori_loop(0, nchunk, low_chunk, 0)
    low, n_ge = bisect16(c16_s, n_prefix)
    thr = prefix * 65536 + (low + 32768)
    tie = (n_ge > top_k) & (thr > NEG_INF_KEY)
    j_s[...] = jnp.full(j_s.shape, seq, I32)

    @pl.when(jnp.max(jnp.where(tie, 1, 0)) > 0)
    def _():
        need = top_k - count(lambda tile, s_pos: tile > thr)

        def jbisect(_, lohi):
            lo, hi = lohi
            mid = (lo + hi) >> 1
            ok = count(lambda tile, s_pos: (tile == thr) & (s_pos <= mid)) >= need
            return jnp.where(ok, lo, mid), jnp.where(ok, mid, hi)
        lo_j = jnp.full((1, Q_BLOCK), -1, I32)
        hi_j = jnp.full((1, Q_BLOCK), seq - 1, I32)
        _, cut = lax.fori_loop(0, max(1, (seq - 1).bit_length()) + 1, jbisect, (lo_j, hi_j))
        j_s[...] = jnp.broadcast_to(jnp.where(tie, cut, seq), j_s.shape)

    jcut = j_s[0:1, :]

    def mask_chunk(c):
        tile = key_s[c]
        s_pos = c * ck + lax.broadcasted_iota(I32, (ck, Q_BLOCK), 0)
        sel = (tile > thr) | ((tile == thr) & (s_pos <= jcut))
        sel = sel & (s_pos <= t_row)
        mbt = jnp.where(sel, 0.0, NEG)
        for u in range(sub):
            mb_s[c * sub + u] = mbt[u * Q_BLOCK:(u + 1) * Q_BLOCK, :].T

    @pl.when(odd == 1)
    def _():
        mask_chunk(0)

    def mask_pair(p, carry):
        mask_chunk(odd + 2 * p)
        mask_chunk(odd + 2 * p + 1)
        return carry

    lax.fori_loop(0, nchunk // 2, mask_pair, 0)
    neg_tile = seq // Q_BLOCK
    mb_s[neg_tile] = jnp.full((Q_BLOCK, Q_BLOCK), NEG, F32)

    m_s[...] = jnp.full(m_s.shape, -jnp.inf, F32)
    lp_s[...] = jnp.zeros(lp_s.shape, F32)
    acc_s[...] = jnp.zeros(acc_s.shape, F32)

    ngroup = N_HEADS_A // HEAD_GROUP
    grows_n = HEAD_GROUP * Q_BLOCK
    nparts = grows_n // SM_ROWS
    first_start = i + 1 - sub * nchunk

    def chunk_start(c):
        return jnp.maximum(first_start + c * sub, 0)

    def load_kv(c):
        k0 = pl.multiple_of(chunk_start(c) * Q_BLOCK, Q_BLOCK)
        return ckv_ref[0, pl.ds(k0, ck), :]

    def part_rows(g, part):
        h = g * HEAD_GROUP + part * SM_ROWS // Q_BLOCK
        q0 = (part * SM_ROWS) % Q_BLOCK
        rows = slice(h * Q_BLOCK + q0, h * Q_BLOCK + q0 + SM_ROWS)
        return h, q0, rows, slice(part * SM_ROWS, (part + 1) * SM_ROWS)

    def stage_a(c, g, near):
        start = chunk_start(c)
        shared = (c == 0) & (nchunk > 1)
        tiles = []
        for u in range(sub):
            blk = start + u
            tiles.append(jnp.where(shared & (blk >= first_start + sub), neg_tile, blk))
        s = _dot_nt(ql_ref[0, g * grows_n:(g + 1) * grows_n, :], load_kv(c))
        for part in range(nparts):
            h, q0, rows, lrows = part_rows(g, part)
            mx = None
            for u in range(sub):
                cols = slice(u * Q_BLOCK, (u + 1) * Q_BLOCK)
                x = s[lrows, cols] + mb_s[tiles[u], q0:q0 + SM_ROWS, :]
                if near:
                    x = x + bt_ref[jnp.clip(i - (start + u), 0, 2), h, q0:q0 + SM_ROWS, :]
                x_buf[g, lrows, cols] = x
                mx = x if mx is None else jnp.maximum(mx, x)
            m_prev = m_s[rows, :]
            m_cur = jnp.broadcast_to(jnp.max(mx, axis=1, keepdims=True), (SM_ROWS, LANES))
            m_new = jnp.maximum(m_prev, m_cur)
            al_buf[g, lrows, :] = jnp.exp2(m_prev - m_new)
            m_s[rows, :] = m_new

    def stage_b(g):
        for part in range(nparts):
            _, _, rows, lrows = part_rows(g, part)
            m_new = m_s[rows, :]
            ps = [jnp.exp2(x_buf[g, lrows, u * Q_BLOCK:(u + 1) * Q_BLOCK] - m_new) for u in range(sub)]
            lp_s[rows, :] = al_buf[g, lrows, :] * lp_s[rows, :] + ((ps[0] + ps[1]) + (ps[2] + ps[3]))
            p_buf[g, lrows, :] = jnp.concatenate(ps, axis=1).astype(BF16)

    def stage_c(c, g):
        grows = slice(g * grows_n, (g + 1) * grows_n)
        alpha = jnp.concatenate([al_buf[g]] * (KV_RANK // LANES), axis=1)
        acc_s[grows, :] = alpha * acc_s[grows, :] + _dot(p_buf[g], load_kv(c))

    assert sub == 4 and ngroup >= 2

    def chunk_steps(c, near, first):
        for k in range(ngroup):
            stage_a(c, k, near)
            if k >= 1 or not first:
                stage_b((k - 1) % ngroup)
            if k >= 2:
                stage_c(c, k - 2)
            elif not first:
                stage_c(c - 1, k - 2 + ngroup)

    def drain(c):
        stage_b(ngroup - 1)
        stage_c(c, ngroup - 2)
        stage_c(c, ngroup - 1)

    last = nchunk - 1

    @pl.when(nchunk == 1)
    def _():
        chunk_steps(last, True, True)
        drain(last)

    @pl.when(nchunk > 1)
    def _():
        chunk_steps(0, False, True)

        def far_body(c, carry):
            chunk_steps(c, False, False)
            return carry

        lax.fori_loop(1, last, far_body, 0)
        chunk_steps(last, True, False)
        drain(last)

    for h in range(N_HEADS_A):
        rows = slice(h * Q_BLOCK, (h + 1) * Q_BLOCK)
        l = jnp.sum(lp_s[rows, :], axis=1, keepdims=True)
        o = (acc_s[rows, :] / l).astype(BF16)
        ya_ref[0, :, h * HEAD_DIM_A:(h + 1) * HEAD_DIM_A] = _dot(o, wuv_ref[h]).astype(ya_ref.dtype)


def _attention(qi, wt, kidx, ckv, ql, bt, wuv, batch, seq):
    nq = seq // Q_BLOCK
    hq = N_HEADS_A * Q_BLOCK
    ck = KEY_CHUNK
    grp = HEAD_GROUP * Q_BLOCK
    ngrp = N_HEADS_A // HEAD_GROUP
    assert seq % ck == 0
    top_k = min(TOP_K_MAX, seq // 4)
    kidx3 = kidx.reshape(batch, seq, LANES)
    ckv3 = ckv.reshape(batch, seq, KV_RANK)
    return pl.pallas_call(
        functools.partial(_attn_kernel, seq=seq, top_k=top_k),
        grid=(batch, nq),
        in_specs=[pl.BlockSpec((1, hq, LANES), lambda b, i: (b * nq + i, 0, 0)),
                  pl.BlockSpec((1, N_HEADS_IDX, Q_BLOCK), lambda b, i: (b * nq + i, 0, 0)),
                  pl.BlockSpec((1, seq, LANES), lambda b, i: (b, 0, 0)),
                  pl.BlockSpec((1, seq, KV_RANK), lambda b, i: (b, 0, 0)),
                  pl.BlockSpec((1, hq, KV_RANK), lambda b, i: (b * nq + i, 0, 0)),
                  pl.BlockSpec(bt.shape, lambda b, i: (0, 0, 0, 0)),
                  pl.BlockSpec(wuv.shape, lambda b, i: (0, 0, 0))],
        out_specs=pl.BlockSpec((1, Q_BLOCK, N_HEADS_A * HEAD_DIM_A), lambda b, i: (b, i, 0)),
        out_shape=jax.ShapeDtypeStruct((batch, seq, N_HEADS_A * HEAD_DIM_A), BF16),
        scratch_shapes=[pltpu.VMEM((seq // ck, ck, Q_BLOCK), I32),
                        pltpu.VMEM((seq // ck, ck, Q_BLOCK), I16),
                        pltpu.VMEM((seq // ck, ck, Q_BLOCK), I16),
                        pltpu.VMEM((seq // ck, ck, Q_BLOCK), I16),
                        pltpu.VMEM((nq + 1, Q_BLOCK, Q_BLOCK), F32),
                        pltpu.VMEM((2, ck, hq), F32),
                        pltpu.VMEM((ngrp, grp, ck), F32),
                        pltpu.VMEM((ngrp, grp, ck), BF16),
                        pltpu.VMEM((ngrp, grp, LANES), F32),
                        pltpu.VMEM((hq, KV_RANK), F32),
                        pltpu.VMEM((hq, LANES), F32),
                        pltpu.VMEM((hq, LANES), F32),
                        pltpu.VMEM((SUBLANES, Q_BLOCK), I32)],
        compiler_params=_cparams("parallel", "arbitrary"),
        name="dsa_attention",
    )(qi, wt, kidx3, ckv3, ql, bt, wuv)


def _xbc_conv_kernel(a_ref, w_ref, cw_ref, cb_ref, o_ref, tail_s, ya_s, yb_s, *, tm, per_b):
    i, j = pl.program_id(0), pl.program_id(1)

    @pl.when(i % per_b == 0)
    def _():
        tail_s[j] = jnp.zeros(tail_s.shape[1:], F32)

    nstrip = o_ref.shape[1] // CONV_STRIP
    y_bufs = (ya_s, yb_s)
    row0 = pl.multiple_of(jnp.minimum(i, 0) * SUBLANES, SUBLANES)

    def matmul_strip(s):
        cols = slice(s * CONV_STRIP, (s + 1) * CONV_STRIP)
        buf = y_bufs[s % 2]
        buf[0:SUBLANES, :] = tail_s[j, :, cols]
        buf[SUBLANES:SUBLANES + tm, :] = _dot(a_ref[...], w_ref[:, cols])

    def conv_strip(s):
        cols = slice(s * CONV_STRIP, (s + 1) * CONV_STRIP)
        buf = y_bufs[s % 2]
        tail_s[j, :, cols] = buf[pl.ds(row0 + tm, SUBLANES), :]
        for r in range(0, tm, CONV_ROWS):
            ext = buf[pl.ds(row0 + r, CONV_ROWS + SUBLANES), :]
            acc = ext * cw_ref[0:1, cols]
            for k in range(1, CONV_WIDTH):
                acc = pltpu.roll(acc, 1, axis=0) + ext * cw_ref[k:k + 1, cols]
            o_ref[r:r + CONV_ROWS, cols] = _silu(acc[SUBLANES:, :] + cb_ref[:, cols]).astype(o_ref.dtype)

    for s in range(nstrip):
        matmul_strip(s)
        if s >= 1:
            conv_strip(s - 1)
    conv_strip(nstrip - 1)


def _xbc_conv(a, w, conv_w, conv_b, seq, *, tm, tn):
    m, kdim = a.shape
    n = w.shape[1]
    assert m % tm == 0 and n % tn == 0 and seq % tm == 0
    return pl.pallas_call(
        functools.partial(_xbc_conv_kernel, tm=tm, per_b=seq // tm),
        grid=(m // tm, n // tn),
        in_specs=[pl.BlockSpec((tm, kdim), lambda i, j: (i, 0)),
                  pl.BlockSpec((kdim, tn), lambda i, j: (0, j)),
                  pl.BlockSpec((CONV_WIDTH, tn), lambda i, j: (0, j)),
                  pl.BlockSpec((1, tn), lambda i, j: (0, j))],
        out_specs=pl.BlockSpec((tm, tn), lambda i, j: (i, j)),
        out_shape=jax.ShapeDtypeStruct((m, n), BF16),
        scratch_shapes=[pltpu.VMEM((n // tn, SUBLANES, tn), F32),
                        pltpu.VMEM((SUBLANES + tm, CONV_STRIP), F32),
                        pltpu.VMEM((SUBLANES + tm, CONV_STRIP), F32)],
        compiler_params=_cparams("arbitrary", "arbitrary"),
        name="in_xbc_conv",
    )(a, w, conv_w, conv_b)


def _ssd_kernel(xc_ref, zs_ref, dt_ref, dtb_ref, aneg_ref, dskip_ref, ng_ref, e_ref, y_ref,
                state_s, de_s, *, d_inner):
    c = pl.program_id(1)
    hg = d_inner // SSM_GROUPS
    heads_g = hg // SSM_HEAD_DIM
    b_off = d_inner
    c_off = d_inner + SSM_GROUPS * D_STATE

    @pl.when(c == 0)
    def _():
        state_s[...] = jnp.zeros_like(state_s)

    x = dt_ref[0] + dtb_ref[...]
    dt = jnp.maximum(x, 0.0) + jnp.log1p(jnp.exp(-jnp.abs(x)))
    da = dt * aneg_ref[...]
    tri_i = lax.broadcasted_iota(I32, (CHUNK, CHUNK), 0) >= lax.broadcasted_iota(I32, (CHUNK, CHUNK), 1)
    tri = jnp.where(tri_i, 1.0, 0.0).astype(BF16)
    acs = sum(_dot(tri, part) for part in _split3(da))
    acs_t = acs.T
    row_t = acs_t - jnp.log(dt.T)
    acs_last = acs[CHUNK - 1:CHUNK, :]
    e = e_ref[...]
    de_hi, de_mid, _ = _split3(jnp.exp(acs_last - acs) * dt)
    el_terms = [jnp.broadcast_to(part, (2 * SUBLANES, LANES)) for part in _split3(jnp.exp(acs_last))]
    first = _dot(jnp.concatenate([de_hi] + el_terms, axis=0), e)
    de_s[...] = first[:CHUNK] + _dot(de_mid, e)
    el_x = (first[CHUNK:CHUNK + 1] + first[CHUNK + 2 * SUBLANES:CHUNK + 2 * SUBLANES + 1]
            + first[CHUNK + 4 * SUBLANES:CHUNK + 4 * SUBLANES + 1])
    lane = lax.broadcasted_iota(I32, (CHUNK, LANES), 1)

    for g in range(SSM_GROUPS):
        cols = slice(g * hg, (g + 1) * hg)
        xg = xc_ref[0, :, cols]
        bg = xc_ref[0, :, b_off + g * D_STATE:b_off + (g + 1) * D_STATE]
        cg = xc_ref[0, :, c_off + g * D_STATE:c_off + (g + 1) * D_STATE]
        cb = _dot_nt(cg, bg)
        cg_f = cg.astype(F32)
        sg = state_s[g]
        sg_b = sg.astype(BF16)
        tiles = []
        for pr in range(heads_g // 2):
            pcols = slice(pr * LANES, (pr + 1) * LANES)
            rhs = jnp.concatenate([xg[:, pcols], sg_b[:, pcols]], axis=0)
            outs = []
            for q in range(2):
                h = g * heads_g + pr * 2 + q
                col = jnp.broadcast_to(acs[:, h:h + 1], (CHUNK, CHUNK))
                w = cb * jnp.exp(jnp.where(tri_i, col - row_t[h:h + 1, :], NEG))
                lhs = jnp.concatenate([w.astype(BF16), (cg_f * jnp.exp(col)).astype(BF16)], axis=1)
                outs.append(_dot(lhs, rhs))
            tiles.append(jnp.where(lane < SSM_HEAD_DIM, outs[0], outs[1]))
        xg_f = xg.astype(F32)
        y = jnp.concatenate(tiles, axis=1) + dskip_ref[:, cols] * xg_f
        xd = (xg_f * de_s[:, cols]).astype(BF16)
        state_s[g] = sg * el_x[:, cols] + _dot(bg.astype(F32).T.astype(BF16), xd)
        yz = y * zs_ref[0, :, cols].astype(F32)
        ms = jnp.mean(yz * yz, axis=-1, keepdims=True)
        y_ref[0, :, cols] = (yz * lax.rsqrt(ms + EPS) * ng_ref[:, cols]).astype(y_ref.dtype)


def _ssd(xc, zs, small3, dtb, aneg, dskip_x, ng, expand, dt_block):
    batch, seq, d_xbc = xc.shape
    d_inner = zs.shape[2]
    nc = seq // CHUNK
    const2 = lambda b, c: (0, 0)
    return pl.pallas_call(
        functools.partial(_ssd_kernel, d_inner=d_inner),
        grid=(batch, nc),
        in_specs=[pl.BlockSpec((1, CHUNK, d_xbc), lambda b, c: (b, c, 0)),
                  pl.BlockSpec((1, CHUNK, d_inner), lambda b, c: (b, c, 0)),
                  pl.BlockSpec((1, CHUNK, LANES), lambda b, c: (b, c, dt_block)),
                  pl.BlockSpec(dtb.shape, const2),
                  pl.BlockSpec(aneg.shape, const2),
                  pl.BlockSpec(dskip_x.shape, const2),
                  pl.BlockSpec(ng.shape, const2),
                  pl.BlockSpec(expand.shape, const2)],
        out_specs=pl.BlockSpec((1, CHUNK, d_inner), lambda b, c: (b, c, 0)),
        out_shape=jax.ShapeDtypeStruct((batch, seq, d_inner), BF16),
        scratch_shapes=[pltpu.VMEM((SSM_GROUPS, D_STATE, d_inner // SSM_GROUPS), F32),
                        pltpu.VMEM((CHUNK, d_inner), F32)],
        compiler_params=_cparams("parallel", "arbitrary"),
        name="ssd",
    )(xc, zs, small3, dtb, aneg, dskip_x, ng, expand)


def _pad_cols(a, n):
    return jnp.pad(a, ((0, 0), (0, n - a.shape[1])))


def _layer(x2, mod, batch, seq, norm1_g, w_in, cq_norm_g, ckv_norm_g, kidx_norm_g, kidx_norm_b, w_uq, w_iq,
           w_uk, w_uv, rel_bias, conv_w, conv_b, dt_bias, a_log, d_skip, ssm_norm_g, w_proj_a, w_proj_b,
           w_out, norm2_g, w_gate, w_up, w_down, final_g):
    t, d = x2.shape
    n_heads_b = dt_bias.shape[0]
    d_inner = n_heads_b * SSM_HEAD_DIM
    d_xbc = d_inner + 2 * SSM_GROUPS * D_STATE
    ha = N_HEADS_A * HEAD_DIM_A
    sh1, sc1, g1, sh2, sc2, g2 = [m.reshape(batch, 1, d) for m in jnp.split(mod, 6, axis=-1)]

    o_z = Q_RANK + KV_RANK + HEAD_DIM_IDX + N_HEADS_IDX
    o_xbc = o_z + d_inner
    o_dt = o_xbc + d_xbc
    o_g = o_dt + n_heads_b
    small_w = 1024
    dt_col = 896
    assert o_z <= dt_col and dt_col + n_heads_b <= small_w and dt_col % LANES == 0
    w_small = jnp.concatenate(
        [_pad_cols(w_in[:, :o_z], dt_col), _pad_cols(w_in[:, o_dt:o_g], small_w - dt_col)], axis=1).astype(BF16)
    w_z = w_in[:, o_z:o_xbc].astype(BF16)
    w_xbc = w_in[:, o_xbc:o_dt].astype(BF16)
    w_g = w_in[:, o_g:].astype(BF16)

    tm = 1024
    per_b = seq // tm
    small, h = _norm_proj(x2, norm1_g.reshape(1, d), sc1, sh1, w_small, seq, tm=tm)
    zs = _matmul("in_z", h, w_z, BF16, tm=tm, tn=2048, epilogue=lambda acc: _silu(acc))
    xc = _xbc_conv(h, w_xbc, conv_w, conv_b.reshape(1, d_xbc), seq, tm=tm, tn=1024)
    sg = _matmul("in_gates", h, w_g, BF16, tm=tm, tn=2048, epilogue=lambda acc: jax.nn.sigmoid(acc))

    w_iq_p = jnp.pad(w_iq.reshape(Q_RANK, N_HEADS_IDX, HEAD_DIM_IDX),
                     ((0, 0), (0, 0), (0, LANES - HEAD_DIM_IDX))).reshape(Q_RANK, N_HEADS_IDX * LANES).astype(BF16)
    w_uk_t = jnp.transpose(w_uk.reshape(KV_RANK, N_HEADS_A, HEAD_DIM_A), (1, 2, 0)).astype(BF16)
    w_uv_h = jnp.transpose(w_uv.reshape(KV_RANK, N_HEADS_A, HEAD_DIM_A), (1, 0, 2)).astype(BF16)
    ql, qi, wt, ckv, kidx = _prep(
        small, cq_norm_g.reshape(1, Q_RANK), ckv_norm_g.reshape(1, KV_RANK),
        _pad_cols(kidx_norm_g.reshape(1, HEAD_DIM_IDX), LANES), _pad_cols(kidx_norm_b.reshape(1, HEAD_DIM_IDX), LANES),
        w_uq.astype(BF16), w_iq_p, w_uk_t)
    bt = _bias_tiles(rel_bias)
    y_a = _attention(qi, wt, kidx, ckv, ql, bt, w_uv_h, batch, seq).reshape(t, ha)

    expand = np.zeros((LANES, d_inner), np.float32)
    for hh in range(n_heads_b):
        expand[hh, hh * SSM_HEAD_DIM:(hh + 1) * SSM_HEAD_DIM] = 1.0
    aneg = _pad_cols(-jnp.exp(a_log.astype(F32)).reshape(1, n_heads_b), LANES)
    y_b = _ssd(xc.reshape(batch, seq, d_xbc), zs.reshape(batch, seq, d_inner), small.reshape(batch, seq, small_w),
               _pad_cols(dt_bias.reshape(1, n_heads_b), LANES), aneg,
               jnp.repeat(d_skip, SSM_HEAD_DIM).reshape(1, d_inner), ssm_norm_g.reshape(1, d_inner),
               jnp.asarray(expand, BF16), dt_col // LANES).reshape(t, d_inner)

    tn = 1024
    tp = 512
    merged = _matmul2("proj_ab", y_a, w_proj_a.astype(BF16), y_b, w_proj_b.astype(BF16), BF16, tm=tm, tn=tp,
                      epilogue=lambda ra, rb, sa_ref, sb_ref: (sa_ref[...].astype(F32) * ra
                                                               + sb_ref[...].astype(F32) * rb),
                      extras=(sg, sg),
                      extra_specs=(pl.BlockSpec((tm, tp), lambda i, j: (i, j)),
                                   pl.BlockSpec((tm, tp), lambda i, j: (i, j + d // tp))))
    x1 = _matmul("w_out", merged, w_out.astype(BF16), F32, tm=tm, tn=tn,
                 epilogue=lambda acc, x_ref, g_ref: x_ref[...] + g_ref[0] * acc,
                 extras=(x2, g1), extra_specs=(_tile_spec(tm, tn), _batch_row_spec(tn, per_b)))

    d_ff = w_gate.shape[1]
    act = _ffn_in(x1, norm2_g.reshape(1, d), sc2, sh2, w_gate, w_up, seq, tm=tm, tn=512)

    def down_epilogue(acc, x_ref, g_ref, f_ref):
        xo = x_ref[...] + g_ref[0] * acc
        ms = jnp.mean(xo * xo, axis=-1, keepdims=True)
        return xo * lax.rsqrt(ms + EPS) * f_ref[...]

    tmd = 512
    return _matmul("ffn_down", act, w_down.astype(BF16), F32, tm=tmd, tn=d,
                   epilogue=down_epilogue, extras=(x1, g2, final_g.reshape(1, d)),
                   extra_specs=(_tile_spec(tmd, d), _batch_row_spec(d, seq // tmd),
                                pl.BlockSpec((1, d), lambda i, j, k: (0, 0))))


def kernel(x, c, w_ada, b_ada, norm1_g, w_in, cq_norm_g, ckv_norm_g, kidx_norm_g, kidx_norm_b, w_uq, w_iq, w_uk, w_uv, rel_bias, conv_w, conv_b, dt_bias, a_log, d_skip, ssm_norm_g, w_proj_a, w_proj_b, w_out, norm2_g, w_gate, w_up, w_down, final_g):
    batch, seq, d = x.shape
    assert w_ada.shape[0] == 1, "single-layer block"
    c8 = jnp.pad(c, ((0, SUBLANES - batch % SUBLANES if batch % SUBLANES else 0), (0, 0)))
    mod = _ada(c8, w_ada[0], b_ada[0].reshape(1, -1))[:batch]
    out = _layer(x.reshape(batch * seq, d), mod, batch, seq, norm1_g[0], w_in[0], cq_norm_g[0], ckv_norm_g[0],
                 kidx_norm_g[0], kidx_norm_b[0], w_uq[0], w_iq[0], w_uk[0], w_uv[0], rel_bias, conv_w[0],
                 conv_b[0], dt_bias[0], a_log[0], d_skip[0], ssm_norm_g[0], w_proj_a[0], w_proj_b[0], w_out[0],
                 norm2_g[0], w_gate[0], w_up[0], w_down[0], final_g)
    return out.reshape(batch, seq, d)
```
